```python
import math
import jax, jax.numpy as jnp
from jax import lax
import numpy as np

D_MODEL = 1024
BATCH = 2
SEQ = 8192
DEPTH = 2

HEAD_DIM = 64
N_HEADS = D_MODEL // HEAD_DIM
N_HEADS_SB = N_HEADS // 2
N_HEADS_DIL = N_HEADS - N_HEADS_SB
D_SB = N_HEADS_SB * HEAD_DIM
D_DIL = N_HEADS_DIL * HEAD_DIM
DIL_PATTERNS = ((128, 1), (512, 4), (2048, 16))
BLOCK = 128
N_BUCKETS = 32
MAX_DISTANCE = 2048
N_EXPERTS = 16
N_GROUPS = 4
EXPERTS_PER_GROUP = N_EXPERTS // N_GROUPS
TOP_K = 2
D_FF_EXPERT = 1024
ALPHA = (2.0 * DEPTH) ** 0.25
BETA = (8.0 * DEPTH) ** -0.25
LN_EPS = 1e-5
NEG_INF = -1e30

kernel_name = "hymba_style_sb_dilated_grouped_moe_deepnorm"


def layer_norm(x, g, b):
    xf = x.astype(jnp.float32)
    mu = xf.mean(-1, keepdims=True)
    var = jnp.square(xf - mu).mean(-1, keepdims=True)
    return ((xf - mu) * lax.rsqrt(var + LN_EPS) * g.astype(jnp.float32) + b.astype(jnp.float32)).astype(x.dtype)


def head_rms_merge(o, g):
    b, h, s, dh = o.shape
    of = o.astype(jnp.float32)
    of = of * lax.rsqrt(jnp.mean(jnp.square(of), -1, keepdims=True) + 1e-6)
    of = of.transpose(0, 2, 1, 3).reshape(b, s, h * dh)
    return (of * g.astype(jnp.float32)).astype(o.dtype)


def t5_bucket(dist):
    max_exact = N_BUCKETS // 2
    d = jnp.maximum(dist, 0)
    large = max_exact + (jnp.log(jnp.maximum(d, 1).astype(jnp.float32) / max_exact)
                         / math.log(MAX_DISTANCE / max_exact) * (N_BUCKETS - max_exact)).astype(jnp.int32)
    large = jnp.minimum(large, N_BUCKETS - 1)
    return jnp.where(d < max_exact, d, large)


def stick_breaking_attention(q, k, v):
    b, h, s, dh = q.shape
    nb = s // BLOCK
    scale = dh ** -0.5
    qb = q.reshape(b, h, nb, BLOCK, dh).transpose(2, 0, 1, 3, 4)
    kpos = jnp.arange(s)

    def one_block(args):
        qblk, n = args
        z = jnp.einsum('bhqd,bhkd->bhqk', qblk, k).astype(jnp.float32) * scale
        qpos = n * BLOCK + jnp.arange(BLOCK)
        causal = kpos[None, :] < qpos[:, None]
        log_beta = jax.nn.log_sigmoid(z)
        log_1m = jnp.where(causal, jax.nn.log_sigmoid(-z), 0.0)
        between = lax.cumsum(log_1m, axis=3, reverse=True) - log_1m
        a = jnp.where(causal, jnp.exp(log_beta + between), 0.0)
        return jnp.einsum('bhqk,bhkd->bhqd', a.astype(v.dtype), v)

    out = lax.map(one_block, (qb, jnp.arange(nb)))
    return out.transpose(1, 2, 0, 3, 4).reshape(b, h, s, dh)


def dilated_branch(q, k, v, rel_bias, window, dilation):
    b, h, s, dh = q.shape
    span = window // dilation
    n_cls = s // dilation
    nb = -(-n_cls // BLOCK)
    pad = nb * BLOCK - n_cls

    def to_classes(t):
        t = t.reshape(b, h, n_cls, dilation, dh).transpose(0, 1, 3, 2, 4)
        t = jnp.pad(t, ((0, 0), (0, 0), (0, 0), (0, pad), (0, 0)))
        return t.reshape(b, h, dilation, nb, BLOCK, dh)

    def with_prev(t):
        prev = jnp.pad(t, ((0, 0), (0, 0), (0, 0), (1, 0), (0, 0), (0, 0)))[:, :, :, :nb]
        return jnp.concatenate([prev, t], axis=4)

    qc = to_classes(q)
    kk = with_prev(to_classes(k))
    vv = with_prev(to_classes(v))
    logits = jnp.einsum('bhrnqd,bhrnkd->bhrnqk', qc, kk).astype(jnp.float32) * dh ** -0.5

    qi = jnp.arange(BLOCK)[:, None]
    kj = jnp.arange(2 * BLOCK)[None, :]
    steps = qi + BLOCK - kj
    bias = rel_bias[t5_bucket(jnp.maximum(steps, 0) * dilation)].astype(jnp.float32)
    bias = bias.transpose(2, 0, 1)
    key_cls = (jnp.arange(nb)[:, None, None] - 1) * BLOCK + kj[None]
    valid = (steps >= 0) & (steps <= span) & (key_cls >= 0)
    logits = jnp.where(valid, logits + bias[None, :, None, None], NEG_INF)

    m = logits.max(-1, keepdims=True)
    p = jnp.exp(logits - m)
    denom = p.sum(-1, keepdims=True)
    o = jnp.einsum('bhrnqk,bhrnkd->bhrnqd', p.astype(vv.dtype), vv).astype(jnp.float32) / denom
    lse = (m + jnp.log(denom))[..., 0]

    def from_classes(t):
        t = t.reshape((b, h, dilation, nb * BLOCK) + t.shape[5:])[:, :, :, :n_cls]
        t = jnp.moveaxis(t, 2, 3)
        return t.reshape((b, h, s) + t.shape[4:])

    return from_classes(o), from_classes(lse)


def dilated_attention(q, k, v, rel_bias):
    outs, lses = [], []
    for window, dilation in DIL_PATTERNS:
        o, l = dilated_branch(q, k, v, rel_bias, window, dilation)
        outs.append(o)
        lses.append(l)
    w = jax.nn.softmax(jnp.stack(lses, 0), axis=0)
    return jnp.einsum('gbhs,gbhsd->bhsd', w, jnp.stack(outs, 0)).astype(q.dtype)


def token_mixer(x, w_in, g_sb, g_dil, w_out, rel_bias):
    b, s, _ = x.shape
    proj = jnp.einsum('bsd,de->bse', x, w_in)
    q_sb, k_sb, v_sb, q_dl, k_dl, v_dl = jnp.split(
        proj, [D_SB, 2 * D_SB, 3 * D_SB, 3 * D_SB + D_DIL, 3 * D_SB + 2 * D_DIL], axis=-1)
    to_heads = lambda t: t.reshape(b, s, -1, HEAD_DIM).transpose(0, 2, 1, 3)
    o_sb = stick_breaking_attention(to_heads(q_sb), to_heads(k_sb), to_heads(v_sb))
    o_dl = dilated_attention(to_heads(q_dl), to_heads(k_dl), to_heads(v_dl), rel_bias)
    y = jnp.concatenate([head_rms_merge(o_sb, g_sb), head_rms_merge(o_dl, g_dil)], axis=-1)
    return jnp.einsum('bse,ed->bsd', y, w_out)


def grouped_moe(x, w_router, b_router, w_gate, w_up, w_down):
    b, s, d = x.shape
    t = x.reshape(b * s, d)
    logits = (t @ w_router).astype(jnp.float32) + b_router.astype(jnp.float32)
    probs = jax.nn.softmax(logits, axis=-1)
    pg = probs.reshape(-1, N_GROUPS, EXPERTS_PER_GROUP)
    vals, idx = lax.top_k(pg, TOP_K)
    g_sel = jnp.argmax(vals.sum(-1), axis=-1)
    vals_sel = jnp.take_along_axis(vals, g_sel[:, None, None], axis=1)[:, 0]
    idx_sel = jnp.take_along_axis(idx, g_sel[:, None, None], axis=1)[:, 0]
    expert_ids = g_sel[:, None] * EXPERTS_PER_GROUP + idx_sel
    gates = vals_sel / vals_sel.sum(-1, keepdims=True)
    gate_dense = (jax.nn.one_hot(expert_ids, N_EXPERTS, dtype=jnp.float32) * gates[..., None]).sum(1)
    y = jnp.zeros_like(t)
    for e in range(N_EXPERTS):
        h = jax.nn.silu(t @ w_gate[e]) * (t @ w_up[e])
        y = y + gate_dense[:, e:e + 1].astype(t.dtype) * (h @ w_down[e])
    return y.reshape(b, s, d)


def setup_inputs(seed: int = 0) -> dict:
    key = jax.random.key(seed)
    ks = jax.random.split(key, 20)
    f32 = jnp.float32
    d = D_MODEL
    s_in = d ** -0.5
    nrm = lambda k, shape: jax.random.normal(k, shape, f32)
    x = nrm(ks[0], (BATCH, SEQ, d))

    def qkv_cols(k, width):
        kq, kk, kv = jax.random.split(k, 3)
        return [nrm(kq, (DEPTH, d, width)) * s_in,
                nrm(kk, (DEPTH, d, width)) * s_in,
                nrm(kv, (DEPTH, d, width)) * (s_in * BETA)]

    w_in = jnp.concatenate(qkv_cols(ks[1], D_SB) + qkv_cols(ks[2], D_DIL), axis=-1)
    g_sb = 1.0 + 0.05 * nrm(ks[3], (DEPTH, D_SB))
    g_dil = 1.0 + 0.05 * nrm(ks[4], (DEPTH, D_DIL))
    w_out = nrm(ks[5], (DEPTH, d, d)) * (s_in * BETA)
    ln1_g = 1.0 + 0.05 * nrm(ks[6], (DEPTH, d))
    ln1_b = 0.02 * nrm(ks[7], (DEPTH, d))
    ln2_g = 1.0 + 0.05 * nrm(ks[8], (DEPTH, d))
    ln2_b = 0.02 * nrm(ks[9], (DEPTH, d))
    rel_bias = 0.5 * nrm(ks[10], (N_BUCKETS, N_HEADS_DIL))
    w_router = nrm(ks[11], (d, N_EXPERTS)) * s_in
    b_router = 0.01 * nrm(ks[12], (N_EXPERTS,))
    w_gate = nrm(ks[13], (DEPTH, N_EXPERTS, d, D_FF_EXPERT)) * s_in
    w_up = nrm(ks[14], (DEPTH, N_EXPERTS, d, D_FF_EXPERT)) * (s_in * BETA)
    w_down = nrm(ks[15], (DEPTH, N_EXPERTS, D_FF_EXPERT, d)) * (D_FF_EXPERT ** -0.5 * BETA)
    return {"x": x, "w_in": w_in, "g_sb": g_sb, "g_dil": g_dil, "w_out": w_out,
            "ln1_g": ln1_g, "ln1_b": ln1_b, "ln2_g": ln2_g, "ln2_b": ln2_b,
            "rel_bias": rel_bias, "w_router": w_router, "b_router": b_router,
            "w_gate": w_gate, "w_up": w_up, "w_down": w_down}


def reference(x, w_in, g_sb, g_dil, w_out, ln1_g, ln1_b, ln2_g, ln2_b,
              rel_bias, w_router, b_router, w_gate, w_up, w_down):
    for l in range(DEPTH):
        h = token_mixer(x, w_in[l], g_sb[l], g_dil[l], w_out[l], rel_bias)
        x = layer_norm(ALPHA * x + h, ln1_g[l], ln1_b[l])
        h = grouped_moe(x, w_router, b_router, w_gate[l], w_up[l], w_down[l])
        x = layer_norm(ALPHA * x + h, ln2_g[l], ln2_b[l])
    return x
```

```python
import functools
import math

import jax
import jax.numpy as jnp
from jax import lax
from jax.experimental import pallas as pl
from jax.experimental.pallas import tpu as pltpu

D_MODEL = 1024
HEAD_DIM = 64
N_HEADS_SB = 8
N_HEADS_DIL = 8
D_SB = N_HEADS_SB * HEAD_DIM
D_DIL = N_HEADS_DIL * HEAD_DIM
DIL_PATTERNS = ((128, 1), (512, 4), (2048, 16))
BLOCK = 128
N_BUCKETS = 32
MAX_DISTANCE = 2048
N_EXPERTS = 16
N_GROUPS = 4
EXPERTS_PER_GROUP = N_EXPERTS // N_GROUPS
D_FF_EXPERT = 1024
LN_EPS = 1e-5
NEG_INF = -1e30

LANES = 128
N_PAIRS_SB = D_SB // LANES
N_PAIRS_DIL = D_DIL // LANES
VMEM_LIMIT = 56 * 1024 * 1024

ROW_TILE = 512
SB_TQ = 256
SB_TK = 256
SB_SKIP_BOUND = -110.0
DIL_SUPER = 2048
DIL_UNITS = DIL_SUPER // BLOCK
MOE_TM = 512

_NT = (((1,), (1,)), ((), ()))


def _bf16(x):
    return x.astype(jnp.bfloat16)


def _split_bf16(x):
    hi = _bf16(x)
    lo = _bf16(x - hi.astype(jnp.float32))
    return hi, lo


def _dot(a, b):
    return jnp.dot(a, b, preferred_element_type=jnp.float32)


def _dot_nt(a, b):
    return lax.dot_general(a, b, _NT, preferred_element_type=jnp.float32)


def _head_rms_gain(o, g):
    r = lax.broadcasted_iota(jnp.int32, (LANES, LANES), 0) // HEAD_DIM
    c = lax.broadcasted_iota(jnp.int32, (LANES, LANES), 1) // HEAD_DIM
    same_head = jnp.where(r == c, 1.0 / HEAD_DIM, 0.0).astype(jnp.bfloat16)
    hi, lo = _split_bf16(o * o)
    ms = _dot(hi, same_head) + _dot(lo, same_head)
    return o * lax.rsqrt(ms + 1e-6) * g


def _layer_norm(x, g, b):
    mu = jnp.mean(x, axis=-1, keepdims=True)
    xc = x - mu
    var = jnp.mean(xc * xc, axis=-1, keepdims=True)
    return xc * lax.rsqrt(var + LN_EPS) * g + b


def _in_proj_kernel(x_ref, w_ref, o_ref, *, n_scaled, scale):
    x = _bf16(x_ref[...])
    n_slabs = o_ref.shape[0]
    for j in range(0, n_slabs, 2):
        res = _dot(x, w_ref[:, j * LANES:(j + 2) * LANES])
        for jj in range(2):
            blk = res[:, jj * LANES:(jj + 1) * LANES]
            if j + jj < n_scaled:
                blk = blk * scale
            o_ref[j + jj] = blk.astype(o_ref.dtype)


def _in_proj(x2d, w, out_dtype, n_scaled, name):
    n, d = x2d.shape
    n_slabs = w.shape[1] // LANES
    return pl.pallas_call(
        functools.partial(_in_proj_kernel, n_scaled=n_scaled, scale=HEAD_DIM ** -0.5),
        grid=(n // ROW_TILE,),
        in_specs=[pl.BlockSpec((ROW_TILE, d), lambda i: (i, 0)),
                  pl.BlockSpec((d, w.shape[1]), lambda i: (0, 0))],
        out_specs=pl.BlockSpec((n_slabs, ROW_TILE, LANES), lambda i: (0, i, 0)),
        out_shape=jax.ShapeDtypeStruct((n_slabs, n, LANES), out_dtype),
        compiler_params=pltpu.CompilerParams(dimension_semantics=("arbitrary",),
                                             vmem_limit_bytes=VMEM_LIMIT),
        name=name,
    )(x2d, w)


def _sb_kernel(q_ref, k_ref, v_ref, g_ref, o_ref, carry_ref, acc_ref):
    i = pl.program_id(2)
    lane = lax.broadcasted_iota(jnp.int32, (1, LANES), 1)
    first_head = lane < HEAD_DIM
    q = q_ref[...]
    zero = jnp.zeros_like(q)
    q_heads = (jnp.where(first_head, q, zero), jnp.where(first_head, zero, q))

    jj = lax.broadcasted_iota(jnp.int32, (SB_TK, SB_TK + LANES), 0)
    ss = lax.broadcasted_iota(jnp.int32, (SB_TK, SB_TK + LANES), 1)
    suffix = jnp.where((jj > ss) | (ss >= SB_TK), 1.0, 0.0).astype(jnp.bfloat16)

    tq_i = lax.broadcasted_iota(jnp.int32, (SB_TQ, SB_TK), 0)
    ts_i = lax.broadcasted_iota(jnp.int32, (SB_TQ, SB_TK), 1)
    causal = ts_i < tq_i

    carry_ref[...] = jnp.zeros_like(carry_ref)
    acc_ref[...] = jnp.zeros_like(acc_ref)

    def block(kb, diag):
        off = pl.multiple_of(kb * SB_TK, SB_TK)
        kblk = k_ref[pl.ds(off, SB_TK), :]
        vblk = v_ref[pl.ds(off, SB_TK), :]
        for h in range(2):
            z = _dot_nt(q_heads[h], kblk)
            sp = jnp.log(1.0 + jnp.exp(-jnp.abs(z)))
            log_beta = jnp.minimum(z, 0.0) - sp
            log_1m = -jnp.maximum(z, 0.0) - sp
            if diag:
                log_1m = jnp.where(causal, log_1m, 0.0)
            hi, lo = _split_bf16(log_1m)
            sfx = _dot(hi, suffix) + _dot(lo, suffix)
            c = carry_ref[h]
            between = sfx[:, :SB_TK] + jnp.concatenate([c] * (SB_TK // LANES), axis=1)
            a = jnp.exp(log_beta + between)
            if diag:
                a = jnp.where(causal, a, 0.0)
            acc_ref[h] += _dot(_bf16(a), vblk)
            carry_ref[h] = c + sfx[:, SB_TK:]

    block(i, True)

    def cond(state):
        kb, mx = state
        return jnp.logical_and(kb >= 0, mx > SB_SKIP_BOUND)

    def body(state):
        kb, _ = state
        block(kb, False)
        return kb - 1, jnp.max(carry_ref[...])

    lax.while_loop(cond, body, (i - 1, jnp.max(carry_ref[...])))

    o = jnp.where(first_head, acc_ref[0], acc_ref[1])
    o_ref[...] = _head_rms_gain(o, g_ref[...]).astype(o_ref.dtype)


def _sb_attention(proj, g, batch, seq):
    n = batch * seq
    nq = seq // SB_TQ
    return pl.pallas_call(
        _sb_kernel,
        grid=(batch, N_PAIRS_SB, nq),
        in_specs=[
            pl.BlockSpec((None, SB_TQ, LANES), lambda b, p, i: (p, b * nq + i, 0)),
            pl.BlockSpec((None, seq, LANES), lambda b, p, i: (N_PAIRS_SB + p, b, 0)),
            pl.BlockSpec((None, seq, LANES), lambda b, p, i: (2 * N_PAIRS_SB + p, b, 0)),
            pl.BlockSpec((None, 1, LANES), lambda b, p, i: (p, 0, 0)),
        ],
        out_specs=pl.BlockSpec((None, SB_TQ, LANES), lambda b, p, i: (p, b * nq + i, 0)),
        out_shape=jax.ShapeDtypeStruct((N_PAIRS_SB, n, LANES), jnp.bfloat16),
        scratch_shapes=[pltpu.VMEM((2, SB_TQ, LANES), jnp.float32),
                        pltpu.VMEM((2, SB_TQ, LANES), jnp.float32)],
        compiler_params=pltpu.CompilerParams(
            dimension_semantics=("arbitrary", "arbitrary", "arbitrary"),
            vmem_limit_bytes=VMEM_LIMIT),
        name="sb_attention",
    )(proj, proj, proj, g)


def _dil_kernel(bucket_ref, rb_ref, q_ref, k_ref, v_ref, g_ref, o_ref,
                bias_ref, obr_ref, lse_ref):
    b = pl.program_id(0)
    p = pl.program_id(1)
    i = pl.program_id(2)
    n_br = len(DIL_PATTERNS)

    qi = lax.broadcasted_iota(jnp.int32, (BLOCK, 2 * BLOCK), 0)
    kj = lax.broadcasted_iota(jnp.int32, (BLOCK, 2 * BLOCK), 1)
    steps = qi + BLOCK - kj

    @pl.when((b == 0) & (p == 0) & (i == 0))
    def _():
        def per_head(h, _):
            for gidx, (window, dilation) in enumerate(DIL_PATTERNS):
                bk = bucket_ref[gidx]
                acc = jnp.zeros((BLOCK, 2 * BLOCK), jnp.float32)
                for bb in range(N_BUCKETS):
                    acc = jnp.where(bk == bb, rb_ref[bb, h], acc)
                valid = (steps >= 0) & (steps <= window // dilation)
                bias_ref[gidx * N_HEADS_DIL + h] = jnp.where(valid, acc, NEG_INF)
            return 0
        lax.fori_loop(0, N_HEADS_DIL, per_head, 0)

    lane = lax.broadcasted_iota(jnp.int32, (1, LANES), 1)
    first_head = lane < HEAD_DIM
    t0 = i * DIL_SUPER

    def strided(ref, start, dilation):
        if dilation == 1:
            return ref[pl.ds(start, BLOCK), :]
        return ref[pl.ds(start, BLOCK, stride=dilation), :]

    for gidx, (window, dilation) in enumerate(DIL_PATTERNS):
        units_per_residue = DIL_UNITS // dilation
        shift = units_per_residue.bit_length() - 1

        def unit(u, _, gidx=gidx, dilation=dilation, units_per_residue=units_per_residue, shift=shift):
            r = lax.shift_right_logical(u, shift)
            cb = u & (units_per_residue - 1)
            sq = r + cb * (dilation * BLOCK)
            cur = t0 + sq
            prev = cur - dilation * BLOCK
            prev_ok = prev >= 0
            prev_c = jnp.where(prev_ok, prev, cur)
            n_masked = jnp.where(prev_ok, 0, BLOCK)

            qb = _bf16(strided(q_ref, sq, dilation))
            kk = _bf16(jnp.concatenate([strided(k_ref, prev_c, dilation),
                                        strided(k_ref, cur, dilation)], axis=0))
            vv = _bf16(jnp.concatenate([strided(v_ref, prev_c, dilation),
                                        strided(v_ref, cur, dilation)], axis=0))
            zero = jnp.zeros_like(qb)
            outs, lses = [], []
            for h in range(2):
                qm = jnp.where(first_head, qb, zero) if h == 0 else jnp.where(first_head, zero, qb)
                z = _dot_nt(qm, kk)
                lg = z + bias_ref[gidx * N_HEADS_DIL + 2 * p + h]
                lg = jnp.where(kj < n_masked, NEG_INF, lg)
                m = jnp.max(lg, axis=-1, keepdims=True)
                pe = jnp.exp(lg - m)
                den = jnp.sum(pe, axis=-1, keepdims=True)
                outs.append(_dot(_bf16(pe), vv) / den)
                lses.append(m + jnp.log(den))
            o_tile = jnp.where(first_head, outs[0], outs[1])
            l_tile = jnp.where(first_head, lses[0], lses[1])
            if dilation == 1:
                idx = pl.ds(sq, BLOCK)
            else:
                idx = pl.ds(sq, BLOCK, stride=dilation)
            obr_ref.at[gidx][idx, :] = o_tile
            lse_ref.at[gidx][idx, :] = l_tile
            return 0

        lax.fori_loop(0, DIL_UNITS, unit, 0)

    chunk = 256

    def combine(c, _):
        rows = pl.ds(pl.multiple_of(c * chunk, chunk), chunk)
        ls = [lse_ref[gidx, rows, :] for gidx in range(n_br)]
        m = functools.reduce(jnp.maximum, ls)
        es = [jnp.exp(l - m) for l in ls]
        num = sum(e * obr_ref[gidx, rows, :] for gidx, e in enumerate(es))
        o = num / sum(es)
        o_ref[rows, :] = _head_rms_gain(o, g_ref[...]).astype(o_ref.dtype)
        return 0

    lax.fori_loop(0, DIL_SUPER // chunk, combine, 0)


def _dil_attention(proj, g, buckets, rel_bias, batch, seq):
    n = batch * seq
    ns = seq // DIL_SUPER
    n_br = len(DIL_PATTERNS)
    return pl.pallas_call(
        _dil_kernel,
        grid=(batch, N_PAIRS_DIL, ns),
        in_specs=[
            pl.BlockSpec((n_br, BLOCK, 2 * BLOCK), lambda b, p, i: (0, 0, 0)),
            pl.BlockSpec(memory_space=pltpu.SMEM),
            pl.BlockSpec((None, DIL_SUPER, LANES), lambda b, p, i: (p, b * ns + i, 0)),
            pl.BlockSpec((None, seq, LANES), lambda b, p, i: (N_PAIRS_DIL + p, b, 0)),
            pl.BlockSpec((None, seq, LANES), lambda b, p, i: (2 * N_PAIRS_DIL + p, b, 0)),
            pl.BlockSpec((None, 1, LANES), lambda b, p, i: (p, 0, 0)),
        ],
        out_specs=pl.BlockSpec((None, DIL_SUPER, LANES), lambda b, p, i: (p, b * ns + i, 0)),
        out_shape=jax.ShapeDtypeStruct((N_PAIRS_DIL, n, LANES), jnp.bfloat16),
        scratch_shapes=[pltpu.VMEM((n_br * N_HEADS_DIL, BLOCK, 2 * BLOCK), jnp.float32),
                        pltpu.VMEM((n_br, DIL_SUPER, LANES), jnp.float32),
                        pltpu.VMEM((n_br, DIL_SUPER, LANES), jnp.float32)],
        compiler_params=pltpu.CompilerParams(
            dimension_semantics=("arbitrary", "arbitrary", "arbitrary"),
            vmem_limit_bytes=VMEM_LIMIT),
        name="dilated_attention",
    )(buckets, rel_bias, proj, proj, proj, g)


def _router_gates(x1, wrt_hi, wrt_lo, br):
    xh, xl = _split_bf16(x1)
    lt = _dot_nt(wrt_hi, xh) + _dot_nt(wrt_hi, xl) + _dot_nt(wrt_lo, xh) + br
    rows = [lt[e:e + 1, :] for e in range(N_EXPERTS)]
    m = functools.reduce(jnp.maximum, rows)
    ex = [jnp.exp(r - m) for r in rows]
    tot = functools.reduce(lambda a, c: a + c, ex)
    probs = [e / tot for e in ex]

    keep, score = [], []
    for gi in range(N_GROUPS):
        members = range(gi * EXPERTS_PER_GROUP, (gi + 1) * EXPERTS_PER_GROUP)
        s = None
        for e in members:
            rank = None
            for o in members:
                if o == e:
                    continue
                beats = (probs[o] > probs[e]) | ((probs[o] == probs[e]) & (o < e))
                beats = beats.astype(jnp.int32)
                rank = beats if rank is None else rank + beats
            k = rank < 2
            keep.append(k)
            contrib = jnp.where(k, probs[e], 0.0)
            s = contrib if s is None else s + contrib
        score.append(s)

    gates = []
    for gi in range(N_GROUPS):
        chosen = None
        for o in range(N_GROUPS):
            if o == gi:
                continue
            c = (score[gi] > score[o]) if o < gi else (score[gi] >= score[o])
            chosen = c if chosen is None else (chosen & c)
        for e in range(gi * EXPERTS_PER_GROUP, (gi + 1) * EXPERTS_PER_GROUP):
            gates.append(jnp.where(keep[e] & chosen, probs[e] / score[gi], 0.0))

    row_id = lax.broadcasted_iota(jnp.int32, (N_EXPERTS, x1.shape[0]), 0)
    dense = jnp.zeros((N_EXPERTS, x1.shape[0]), jnp.float32)
    for e in range(N_EXPERTS):
        dense = jnp.where(row_id == e, gates[e], dense)
    return dense


def _out_proj_kernel(x_ref, ysb_ref, ydl_ref, w_ref, lng_ref, lnb_ref,
                     wrh_ref, wrl_ref, br_ref, x1_ref, gate_ref, *, alpha):
    y = jnp.concatenate([ysb_ref[j] for j in range(N_PAIRS_SB)]
                        + [ydl_ref[j] for j in range(N_PAIRS_DIL)], axis=1)
    h = _dot(y, w_ref[...])
    x1 = _layer_norm(alpha * x_ref[...] + h, lng_ref[...], lnb_ref[...])
    x1_ref[...] = x1
    dense = _router_gates(x1, wrh_ref[...], wrl_ref[...], br_ref[...])
    padded = jnp.concatenate(
        [dense, jnp.zeros((LANES - N_EXPERTS, dense.shape[1]), jnp.float32)], axis=0)
    gate_ref[...] = padded.T


def _out_proj(x2d, y_sb, y_dl, w_out, ln_g, ln_b, wrt_hi, wrt_lo, br, alpha):
    n, d = x2d.shape
    const = lambda i: (0, 0)
    return pl.pallas_call(
        functools.partial(_out_proj_kernel, alpha=alpha),
        grid=(n // ROW_TILE,),
        in_specs=[
            pl.BlockSpec((ROW_TILE, d), lambda i: (i, 0)),
            pl.BlockSpec((N_PAIRS_SB, ROW_TILE, LANES), lambda i: (0, i, 0)),
            pl.BlockSpec((N_PAIRS_DIL, ROW_TILE, LANES), lambda i: (0, i, 0)),
            pl.BlockSpec((d, d), const),
            pl.BlockSpec((1, d), const),
            pl.BlockSpec((1, d), const),
            pl.BlockSpec((N_EXPERTS, d), const),
            pl.BlockSpec((N_EXPERTS, d), const),
            pl.BlockSpec((N_EXPERTS, 1), const),
        ],
        out_specs=[pl.BlockSpec((ROW_TILE, d), lambda i: (i, 0)),
                   pl.BlockSpec((ROW_TILE, LANES), lambda i: (i, 0))],
        out_shape=[jax.ShapeDtypeStruct((n, d), jnp.float32),
                   jax.ShapeDtypeStruct((n, LANES), jnp.float32)],
        compiler_params=pltpu.CompilerParams(dimension_semantics=("arbitrary",),
                                             vmem_limit_bytes=VMEM_LIMIT),
        name="out_proj_ln_router",
    )(x2d, y_sb, y_dl, w_out, ln_g, ln_b, wrt_hi, wrt_lo, br)


def _moe_kernel(x_ref, gate_ref, wg_ref, wu_ref, wd_ref, lng_ref, lnb_ref, o_ref,
                xb_ref, acc_ref, *, alpha):
    e = pl.program_id(1)

    @pl.when(e == 0)
    def _():
        xb_ref[...] = _bf16(x_ref[...])
        acc_ref[...] = jnp.zeros_like(acc_ref)

    xb = xb_ref[...]
    gt = _dot(xb, wg_ref[...])
    up = _dot(xb, wu_ref[...])
    h = gt / (1.0 + jnp.exp(-gt)) * up
    y = _dot(_bf16(h), wd_ref[...])
    lane = lax.broadcasted_iota(jnp.int32, (1, LANES), 1)
    gate = jnp.sum(jnp.where(lane == e, gate_ref[...], 0.0), axis=-1, keepdims=True)
    acc_ref[...] += gate * y

    @pl.when(e == N_EXPERTS - 1)
    def _():
        o_ref[...] = _layer_norm(alpha * x_ref[...] + acc_ref[...], lng_ref[...], lnb_ref[...])


def _moe(x1, gate, w_gate, w_up, w_down, ln_g, ln_b, alpha):
    n, d = x1.shape
    f = w_gate.shape[-1]
    const = lambda i, e: (0, 0)
    return pl.pallas_call(
        functools.partial(_moe_kernel, alpha=alpha),
        grid=(n // MOE_TM, N_EXPERTS),
        in_specs=[
            pl.BlockSpec((MOE_TM, d), lambda i, e: (i, 0)),
            pl.BlockSpec((MOE_TM, LANES), lambda i, e: (i, 0)),
            pl.BlockSpec((None, d, f), lambda i, e: (e, 0, 0)),
            pl.BlockSpec((None, d, f), lambda i, e: (e, 0, 0)),
            pl.BlockSpec((None, f, d), lambda i, e: (e, 0, 0)),
            pl.BlockSpec((1, d), const),
            pl.BlockSpec((1, d), const),
        ],
        out_specs=pl.BlockSpec((MOE_TM, d), lambda i, e: (i, 0)),
        out_shape=jax.ShapeDtypeStruct((n, d), jnp.float32),
        scratch_shapes=[pltpu.VMEM((MOE_TM, d), jnp.bfloat16),
                        pltpu.VMEM((MOE_TM, d), jnp.float32)],
        compiler_params=pltpu.CompilerParams(dimension_semantics=("arbitrary", "arbitrary"),
                                             vmem_limit_bytes=VMEM_LIMIT),
        name="moe_ln",
    )(x1, gate, w_gate, w_up, w_down, ln_g, ln_b)


def _t5_bucket(dist):
    max_exact = N_BUCKETS // 2
    d = jnp.maximum(dist, 0)
    large = max_exact + (jnp.log(jnp.maximum(d, 1).astype(jnp.float32) / max_exact)
                         / math.log(MAX_DISTANCE / max_exact) * (N_BUCKETS - max_exact)).astype(jnp.int32)
    large = jnp.minimum(large, N_BUCKETS - 1)
    return jnp.where(d < max_exact, d, large)


def _bucket_tables():
    qi = jnp.arange(BLOCK)[:, None]
    kj = jnp.arange(2 * BLOCK)[None, :]
    steps = qi + BLOCK - kj
    return jnp.stack([_t5_bucket(jnp.maximum(steps, 0) * dilation)
                      for _, dilation in DIL_PATTERNS]).astype(jnp.int32)


def kernel(x, w_in, g_sb, g_dil, w_out, ln1_g, ln1_b, ln2_g, ln2_b, rel_bias,
           w_router, b_router, w_gate, w_up, w_down):
    batch, seq, d = x.shape
    depth = w_in.shape[0]
    alpha = (2.0 * depth) ** 0.25
    n = batch * seq
    x2d = x.reshape(n, d)

    buckets = _bucket_tables()
    wrt = w_router.T.astype(jnp.float32)
    wrt_hi = _bf16(wrt)
    wrt_lo = _bf16(wrt - wrt_hi.astype(jnp.float32))
    br = b_router.astype(jnp.float32).reshape(N_EXPERTS, 1)
    sb_cols = 3 * D_SB

    for l in range(depth):
        w_in_l = _bf16(w_in[l])
        proj_sb = _in_proj(x2d, w_in_l[:, :sb_cols], jnp.bfloat16, N_PAIRS_SB, "in_proj_sb")
        proj_dl = _in_proj(x2d, w_in_l[:, sb_cols:], jnp.float32, N_PAIRS_DIL, "in_proj_dil")
        y_sb = _sb_attention(proj_sb, g_sb[l].reshape(N_PAIRS_SB, 1, LANES), batch, seq)
        y_dl = _dil_attention(proj_dl, g_dil[l].reshape(N_PAIRS_DIL, 1, LANES),
                              buckets, rel_bias, batch, seq)
        x1, gate = _out_proj(x2d, y_sb, y_dl, _bf16(w_out[l]),
                             ln1_g[l].reshape(1, d), ln1_b[l].reshape(1, d),
                             wrt_hi, wrt_lo, br, alpha)
        x2d = _moe(x1, gate, _bf16(w_gate[l]), _bf16(w_up[l]), _bf16(w_down[l]),
                   ln2_g[l].reshape(1, d), ln2_b[l].reshape(1, d), alpha)
    return x2d.reshape(batch, seq, d)
```

```python
import functools
import math

import jax
import jax.numpy as jnp
from jax import lax
from jax.experimental import pallas as pl
from jax.experimental.pallas import tpu as pltpu

D_MODEL = 1024
HEAD_DIM = 64
N_HEADS_SB = 8
N_HEADS_DIL = 8
D_SB = N_HEADS_SB * HEAD_DIM
D_DIL = N_HEADS_DIL * HEAD_DIM
DIL_PATTERNS = ((128, 1), (512, 4), (2048, 16))
BLOCK = 128
N_BUCKETS = 32
MAX_DISTANCE = 2048
N_EXPERTS = 16
N_GROUPS = 4
EXPERTS_PER_GROUP = N_EXPERTS // N_GROUPS
D_FF_EXPERT = 1024
LN_EPS = 1e-5
NEG_INF = -1e30

LANES = 128
N_PAIRS_SB = D_SB // LANES
N_PAIRS_DIL = D_DIL // LANES
VMEM_LIMIT = 56 * 1024 * 1024

ROW_TILE = 512
SB_TQ = 256
SB_TK = 256
SB_SKIP_BOUND = -110.0
DIL_SUPER = 2048
DIL_UNITS = DIL_SUPER // BLOCK
MOE_TM = 512

_NT = (((1,), (1,)), ((), ()))


def _bf16(x):
    return x.astype(jnp.bfloat16)


def _split_bf16(x):
    hi = _bf16(x)
    lo = _bf16(x - hi.astype(jnp.float32))
    return hi, lo


def _dot(a, b):
    return jnp.dot(a, b, preferred_element_type=jnp.float32)


def _dot_nt(a, b):
    return lax.dot_general(a, b, _NT, preferred_element_type=jnp.float32)


def _head_rms_gain(o, g):
    r = lax.broadcasted_iota(jnp.int32, (LANES, LANES), 0) // HEAD_DIM
    c = lax.broadcasted_iota(jnp.int32, (LANES, LANES), 1) // HEAD_DIM
    same_head = jnp.where(r == c, 1.0 / HEAD_DIM, 0.0).astype(jnp.bfloat16)
    hi, lo = _split_bf16(o * o)
    ms = _dot(hi, same_head) + _dot(lo, same_head)
    return o * lax.rsqrt(ms + 1e-6) * g


def _layer_norm(x, g, b):
    mu = jnp.mean(x, axis=-1, keepdims=True)
    xc = x - mu
    var = jnp.mean(xc * xc, axis=-1, keepdims=True)
    return xc * lax.rsqrt(var + LN_EPS) * g + b


def _in_proj_kernel(x_ref, w_ref, o_ref, *, n_scaled, scale):
    x = _bf16(x_ref[...])
    n_slabs = o_ref.shape[0]
    for j in range(0, n_slabs, 2):
        res = _dot(x, w_ref[:, j * LANES:(j + 2) * LANES])
        for jj in range(2):
            blk = res[:, jj * LANES:(jj + 1) * LANES]
            if j + jj < n_scaled:
                blk = blk * scale
            o_ref[j + jj] = blk.astype(o_ref.dtype)


def _in_proj(x2d, w, out_dtype, n_scaled, name):
    n, d = x2d.shape
    n_slabs = w.shape[1] // LANES
    return pl.pallas_call(
        functools.partial(_in_proj_kernel, n_scaled=n_scaled, scale=HEAD_DIM ** -0.5),
        grid=(n // ROW_TILE,),
        in_specs=[pl.BlockSpec((ROW_TILE, d), lambda i: (i, 0)),
                  pl.BlockSpec((d, w.shape[1]), lambda i: (0, 0))],
        out_specs=pl.BlockSpec((n_slabs, ROW_TILE, LANES), lambda i: (0, i, 0)),
        out_shape=jax.ShapeDtypeStruct((n_slabs, n, LANES), out_dtype),
        compiler_params=pltpu.CompilerParams(dimension_semantics=("arbitrary",),
                                             vmem_limit_bytes=VMEM_LIMIT),
        name=name,
    )(x2d, w)


def _sb_kernel(q_ref, k_ref, v_ref, g_ref, o_ref, carry_ref, acc_ref):
    i = pl.program_id(2)
    lane = lax.broadcasted_iota(jnp.int32, (1, LANES), 1)
    first_head = lane < HEAD_DIM
    q = q_ref[...]
    zero = jnp.zeros_like(q)
    q_heads = (jnp.where(first_head, q, zero), jnp.where(first_head, zero, q))

    jj = lax.broadcasted_iota(jnp.int32, (SB_TK, SB_TK + LANES), 0)
    ss = lax.broadcasted_iota(jnp.int32, (SB_TK, SB_TK + LANES), 1)
    suffix = jnp.where((jj > ss) | (ss >= SB_TK), 1.0, 0.0).astype(jnp.bfloat16)

    tq_i = lax.broadcasted_iota(jnp.int32, (SB_TQ, SB_TK), 0)
    ts_i = lax.broadcasted_iota(jnp.int32, (SB_TQ, SB_TK), 1)
    causal = ts_i < tq_i

    carry_ref[...] = jnp.zeros_like(carry_ref)
    acc_ref[...] = jnp.zeros_like(acc_ref)

    def block(kb, diag):
        off = pl.multiple_of(kb * SB_TK, SB_TK)
        kblk = k_ref[pl.ds(off, SB_TK), :]
        vblk = v_ref[pl.ds(off, SB_TK), :]
        for h in range(2):
            z = _dot_nt(q_heads[h], kblk)
            sp = jnp.log(1.0 + jnp.exp(-jnp.abs(z)))
            log_beta = jnp.minimum(z, 0.0) - sp
            log_1m = -jnp.maximum(z, 0.0) - sp
            if diag:
                log_1m = jnp.where(causal, log_1m, 0.0)
            hi, lo = _split_bf16(log_1m)
            sfx = _dot(hi, suffix) + _dot(lo, suffix)
            c = carry_ref[h]
            between = sfx[:, :SB_TK] + jnp.concatenate([c] * (SB_TK // LANES), axis=1)
            a = jnp.exp(log_beta + between)
            if diag:
                a = jnp.where(causal, a, 0.0)
            acc_ref[h] += _dot(_bf16(a), vblk)
            carry_ref[h] = c + sfx[:, SB_TK:]

    block(i, True)

    def cond(state):
        kb, mx = state
        return jnp.logical_and(kb >= 0, mx > SB_SKIP_BOUND)

    def body(state):
        kb, _ = state
        block(kb, False)
        return kb - 1, jnp.max(carry_ref[...])

    lax.while_loop(cond, body, (i - 1, jnp.max(carry_ref[...])))

    o = jnp.where(first_head, acc_ref[0], acc_ref[1])
    o_ref[...] = _head_rms_gain(o, g_ref[...]).astype(o_ref.dtype)


def _sb_attention(proj, g, batch, seq):
    n = batch * seq
    nq = seq // SB_TQ
    return pl.pallas_call(
        _sb_kernel,
        grid=(batch, N_PAIRS_SB, nq),
        in_specs=[
            pl.BlockSpec((None, SB_TQ, LANES), lambda b, p, i: (p, b * nq + i, 0)),
            pl.BlockSpec((None, seq, LANES), lambda b, p, i: (N_PAIRS_SB + p, b, 0)),
            pl.BlockSpec((None, seq, LANES), lambda b, p, i: (2 * N_PAIRS_SB + p, b, 0)),
            pl.BlockSpec((None, 1, LANES), lambda b, p, i: (p, 0, 0)),
        ],
        out_specs=pl.BlockSpec((None, SB_TQ, LANES), lambda b, p, i: (p, b * nq + i, 0)),
        out_shape=jax.ShapeDtypeStruct((N_PAIRS_SB, n, LANES), jnp.bfloat16),
        scratch_shapes=[pltpu.VMEM((2, SB_TQ, LANES), jnp.float32),
                        pltpu.VMEM((2, SB_TQ, LANES), jnp.float32)],
        compiler_params=pltpu.CompilerParams(
            dimension_semantics=("arbitrary", "arbitrary", "arbitrary"),
            vmem_limit_bytes=VMEM_LIMIT),
        name="sb_attention",
    )(proj, proj, proj, g)


def _dil_kernel(bucket_ref, rb_ref, q_ref, k_ref, v_ref, g_ref, o_ref,
                bias_ref, obr_ref, lse_ref):
    b = pl.program_id(0)
    p = pl.program_id(1)
    i = pl.program_id(2)
    n_br = len(DIL_PATTERNS)

    qi = lax.broadcasted_iota(jnp.int32, (BLOCK, 2 * BLOCK), 0)
    kj = lax.broadcasted_iota(jnp.int32, (BLOCK, 2 * BLOCK), 1)
    steps = qi + BLOCK - kj

    @pl.when((b == 0) & (p == 0) & (i == 0))
    def _():
        def per_head(h, _):
            for gidx, (window, dilation) in enumerate(DIL_PATTERNS):
                bk = bucket_ref[gidx]
                acc = jnp.zeros((BLOCK, 2 * BLOCK), jnp.float32)
                for bb in range(N_BUCKETS):
                    acc = jnp.where(bk == bb, rb_ref[bb, h], acc)
                valid = (steps >= 0) & (steps <= window // dilation)
                slot = 2 * (gidx * N_HEADS_DIL + h)
                bias_ref[slot] = jnp.where(valid, acc, NEG_INF)
                bias_ref[slot + 1] = jnp.where(valid & (kj >= BLOCK), acc, NEG_INF)
            return 0
        lax.fori_loop(0, N_HEADS_DIL, per_head, 0)

    lane = lax.broadcasted_iota(jnp.int32, (1, LANES), 1)
    first_head = lane < HEAD_DIM
    t0 = i * DIL_SUPER

    def strided(ref, start, dilation):
        if dilation == 1:
            return ref[pl.ds(start, BLOCK), :]
        return ref[pl.ds(start, BLOCK, stride=dilation), :]

    for gidx, (window, dilation) in enumerate(DIL_PATTERNS):
        units_per_residue = DIL_UNITS // dilation
        shift = units_per_residue.bit_length() - 1

        def unit(u, _, gidx=gidx, dilation=dilation, units_per_residue=units_per_residue, shift=shift):
            r = lax.shift_right_logical(u, shift)
            cb = u & (units_per_residue - 1)
            sq = r + cb * (dilation * BLOCK)
            cur = t0 + sq
            prev = cur - dilation * BLOCK
            prev_ok = prev >= 0
            prev_c = jnp.where(prev_ok, prev, cur)
            no_prev = jnp.where(prev_ok, 0, 1)

            qb = _bf16(strided(q_ref, sq, dilation))
            kk = _bf16(jnp.concatenate([strided(k_ref, prev_c, dilation),
                                        strided(k_ref, cur, dilation)], axis=0))
            vv = _bf16(jnp.concatenate([strided(v_ref, prev_c, dilation),
                                        strided(v_ref, cur, dilation)], axis=0))
            zero = jnp.zeros_like(qb)
            outs, lses = [], []
            for h in range(2):
                qm = jnp.where(first_head, qb, zero) if h == 0 else jnp.where(first_head, zero, qb)
                z = _dot_nt(qm, kk)
                lg = z + bias_ref[2 * (gidx * N_HEADS_DIL + 2 * p + h) + no_prev]
                m = jnp.max(lg, axis=-1, keepdims=True)
                pe = jnp.exp(lg - m)
                den = jnp.sum(pe, axis=-1, keepdims=True)
                outs.append(_dot(_bf16(pe), vv) / den)
                lses.append(m + jnp.log(den))
            o_tile = jnp.where(first_head, outs[0], outs[1])
            l_tile = jnp.where(first_head, lses[0], lses[1])
            if dilation == 1:
                idx = pl.ds(sq, BLOCK)
            else:
                idx = pl.ds(sq, BLOCK, stride=dilation)
            obr_ref.at[gidx][idx, :] = o_tile
            lse_ref.at[gidx][idx, :] = l_tile
            return 0

        lax.fori_loop(0, DIL_UNITS, unit, 0, unroll=8)

    chunk = 256

    def combine(c, _):
        rows = pl.ds(pl.multiple_of(c * chunk, chunk), chunk)
        ls = [lse_ref[gidx, rows, :] for gidx in range(n_br)]
        m = functools.reduce(jnp.maximum, ls)
        es = [jnp.exp(l - m) for l in ls]
        num = sum(e * obr_ref[gidx, rows, :] for gidx, e in enumerate(es))
        o = num / sum(es)
        o_ref[rows, :] = _head_rms_gain(o, g_ref[...]).astype(o_ref.dtype)
        return 0

    lax.fori_loop(0, DIL_SUPER // chunk, combine, 0)


def _dil_attention(proj, g, buckets, rel_bias, batch, seq):
    n = batch * seq
    ns = seq // DIL_SUPER
    n_br = len(DIL_PATTERNS)
    return pl.pallas_call(
        _dil_kernel,
        grid=(batch, N_PAIRS_DIL, ns),
        in_specs=[
            pl.BlockSpec((n_br, BLOCK, 2 * BLOCK), lambda b, p, i: (0, 0, 0)),
            pl.BlockSpec(memory_space=pltpu.SMEM),
            pl.BlockSpec((None, DIL_SUPER, LANES), lambda b, p, i: (p, b * ns + i, 0)),
            pl.BlockSpec((None, seq, LANES), lambda b, p, i: (N_PAIRS_DIL + p, b, 0)),
            pl.BlockSpec((None, seq, LANES), lambda b, p, i: (2 * N_PAIRS_DIL + p, b, 0)),
            pl.BlockSpec((None, 1, LANES), lambda b, p, i: (p, 0, 0)),
        ],
        out_specs=pl.BlockSpec((None, DIL_SUPER, LANES), lambda b, p, i: (p, b * ns + i, 0)),
        out_shape=jax.ShapeDtypeStruct((N_PAIRS_DIL, n, LANES), jnp.bfloat16),
        scratch_shapes=[pltpu.VMEM((2 * n_br * N_HEADS_DIL, BLOCK, 2 * BLOCK), jnp.float32),
                        pltpu.VMEM((n_br, DIL_SUPER, LANES), jnp.float32),
                        pltpu.VMEM((n_br, DIL_SUPER, LANES), jnp.float32)],
        compiler_params=pltpu.CompilerParams(
            dimension_semantics=("arbitrary", "arbitrary", "arbitrary"),
            vmem_limit_bytes=VMEM_LIMIT),
        name="dilated_attention",
    )(buckets, rel_bias, proj, proj, proj, g)


def _router_gates(x1, wrt_hi, wrt_lo, br):
    xh, xl = _split_bf16(x1)
    lt = _dot_nt(wrt_hi, xh) + _dot_nt(wrt_hi, xl) + _dot_nt(wrt_lo, xh) + br
    rows = [lt[e:e + 1, :] for e in range(N_EXPERTS)]
    m = functools.reduce(jnp.maximum, rows)
    ex = [jnp.exp(r - m) for r in rows]
    tot = functools.reduce(lambda a, c: a + c, ex)
    probs = [e / tot for e in ex]

    keep, score = [], []
    for gi in range(N_GROUPS):
        members = range(gi * EXPERTS_PER_GROUP, (gi + 1) * EXPERTS_PER_GROUP)
        s = None
        for e in members:
            rank = None
            for o in members:
                if o == e:
                    continue
                beats = (probs[o] > probs[e]) | ((probs[o] == probs[e]) & (o < e))
                beats = beats.astype(jnp.int32)
                rank = beats if rank is None else rank + beats
            k = rank < 2
            keep.append(k)
            contrib = jnp.where(k, probs[e], 0.0)
            s = contrib if s is None else s + contrib
        score.append(s)

    gates = []
    for gi in range(N_GROUPS):
        chosen = None
        for o in range(N_GROUPS):
            if o == gi:
                continue
            c = (score[gi] > score[o]) if o < gi else (score[gi] >= score[o])
            chosen = c if chosen is None else (chosen & c)
        for e in range(gi * EXPERTS_PER_GROUP, (gi + 1) * EXPERTS_PER_GROUP):
            gates.append(jnp.where(keep[e] & chosen, probs[e] / score[gi], 0.0))

    row_id = lax.broadcasted_iota(jnp.int32, (N_EXPERTS, x1.shape[0]), 0)
    dense = jnp.zeros((N_EXPERTS, x1.shape[0]), jnp.float32)
    for e in range(N_EXPERTS):
        dense = jnp.where(row_id == e, gates[e], dense)
    return dense


def _out_proj_kernel(x_ref, ysb_ref, ydl_ref, w_ref, lng_ref, lnb_ref,
                     wrh_ref, wrl_ref, br_ref, x1_ref, gate_ref, *, alpha):
    y = jnp.concatenate([ysb_ref[j] for j in range(N_PAIRS_SB)]
                        + [ydl_ref[j] for j in range(N_PAIRS_DIL)], axis=1)
    h = _dot(y, w_ref[...])
    x1 = _layer_norm(alpha * x_ref[...] + h, lng_ref[...], lnb_ref[...])
    x1_ref[...] = x1
    dense = _router_gates(x1, wrh_ref[...], wrl_ref[...], br_ref[...])
    padded = jnp.concatenate(
        [dense, jnp.zeros((LANES - N_EXPERTS, dense.shape[1]), jnp.float32)], axis=0)
    gate_ref[...] = padded.T


def _out_proj(x2d, y_sb, y_dl, w_out, ln_g, ln_b, wrt_hi, wrt_lo, br, alpha):
    n, d = x2d.shape
    const = lambda i: (0, 0)
    return pl.pallas_call(
        functools.partial(_out_proj_kernel, alpha=alpha),
        grid=(n // ROW_TILE,),
        in_specs=[
            pl.BlockSpec((ROW_TILE, d), lambda i: (i, 0)),
            pl.BlockSpec((N_PAIRS_SB, ROW_TILE, LANES), lambda i: (0, i, 0)),
            pl.BlockSpec((N_PAIRS_DIL, ROW_TILE, LANES), lambda i: (0, i, 0)),
            pl.BlockSpec((d, d), const),
            pl.BlockSpec((1, d), const),
            pl.BlockSpec((1, d), const),
            pl.BlockSpec((N_EXPERTS, d), const),
            pl.BlockSpec((N_EXPERTS, d), const),
            pl.BlockSpec((N_EXPERTS, 1), const),
        ],
        out_specs=[pl.BlockSpec((ROW_TILE, d), lambda i: (i, 0)),
                   pl.BlockSpec((ROW_TILE, LANES), lambda i: (i, 0))],
        out_shape=[jax.ShapeDtypeStruct((n, d), jnp.float32),
                   jax.ShapeDtypeStruct((n, LANES), jnp.float32)],
        compiler_params=pltpu.CompilerParams(dimension_semantics=("arbitrary",),
                                             vmem_limit_bytes=VMEM_LIMIT),
        name="out_proj_ln_router",
    )(x2d, y_sb, y_dl, w_out, ln_g, ln_b, wrt_hi, wrt_lo, br)


def _moe_kernel(x_ref, gate_ref, wg_ref, wu_ref, wd_ref, lng_ref, lnb_ref, o_ref,
                xb_ref, acc_ref, *, alpha):
    e = pl.program_id(1)

    @pl.when(e == 0)
    def _():
        xb_ref[...] = _bf16(x_ref[...])
        acc_ref[...] = jnp.zeros_like(acc_ref)

    xb = xb_ref[...]
    gt = _dot(xb, wg_ref[...])
    up = _dot(xb, wu_ref[...])
    h = gt / (1.0 + jnp.exp(-gt)) * up
    y = _dot(_bf16(h), wd_ref[...])
    lane = lax.broadcasted_iota(jnp.int32, (1, LANES), 1)
    gate = jnp.sum(jnp.where(lane == e, gate_ref[...], 0.0), axis=-1, keepdims=True)
    acc_ref[...] += gate * y

    @pl.when(e == N_EXPERTS - 1)
    def _():
        o_ref[...] = _layer_norm(alpha * x_ref[...] + acc_ref[...], lng_ref[...], lnb_ref[...])


def _moe(x1, gate, w_gate, w_up, w_down, ln_g, ln_b, alpha):
    n, d = x1.shape
    f = w_gate.shape[-1]
    const = lambda i, e: (0, 0)
    return pl.pallas_call(
        functools.partial(_moe_kernel, alpha=alpha),
        grid=(n // MOE_TM, N_EXPERTS),
        in_specs=[
            pl.BlockSpec((MOE_TM, d), lambda i, e: (i, 0)),
            pl.BlockSpec((MOE_TM, LANES), lambda i, e: (i, 0)),
            pl.BlockSpec((None, d, f), lambda i, e: (e, 0, 0)),
            pl.BlockSpec((None, d, f), lambda i, e: (e, 0, 0)),
            pl.BlockSpec((None, f, d), lambda i, e: (e, 0, 0)),
            pl.BlockSpec((1, d), const),
            pl.BlockSpec((1, d), const),
        ],
        out_specs=pl.BlockSpec((MOE_TM, d), lambda i, e: (i, 0)),
        out_shape=jax.ShapeDtypeStruct((n, d), jnp.float32),
        scratch_shapes=[pltpu.VMEM((MOE_TM, d), jnp.bfloat16),
                        pltpu.VMEM((MOE_TM, d), jnp.float32)],
        compiler_params=pltpu.CompilerParams(dimension_semantics=("arbitrary", "arbitrary"),
                                             vmem_limit_bytes=VMEM_LIMIT),
        name="moe_ln",
    )(x1, gate, w_gate, w_up, w_down, ln_g, ln_b)


def _t5_bucket(dist):
    max_exact = N_BUCKETS // 2
    d = jnp.maximum(dist, 0)
    large = max_exact + (jnp.log(jnp.maximum(d, 1).astype(jnp.float32) / max_exact)
                         / math.log(MAX_DISTANCE / max_exact) * (N_BUCKETS - max_exact)).astype(jnp.int32)
    large = jnp.minimum(large, N_BUCKETS - 1)
    return jnp.where(d < max_exact, d, large)


def _bucket_tables():
    qi = jnp.arange(BLOCK)[:, None]
    kj = jnp.arange(2 * BLOCK)[None, :]
    steps = qi + BLOCK - kj
    return jnp.stack([_t5_bucket(jnp.maximum(steps, 0) * dilation)
                      for _, dilation in DIL_PATTERNS]).astype(jnp.int32)


def kernel(x, w_in, g_sb, g_dil, w_out, ln1_g, ln1_b, ln2_g, ln2_b, rel_bias,
           w_router, b_router, w_gate, w_up, w_down):
    batch, seq, d = x.shape
    depth = w_in.shape[0]
    alpha = (2.0 * depth) ** 0.25
    n = batch * seq
    x2d = x.reshape(n, d)

    buckets = _bucket_tables()
    wrt = w_router.T.astype(jnp.float32)
    wrt_hi = _bf16(wrt)
    wrt_lo = _bf16(wrt - wrt_hi.astype(jnp.float32))
    br = b_router.astype(jnp.float32).reshape(N_EXPERTS, 1)
    sb_cols = 3 * D_SB

    for l in range(depth):
        w_in_l = _bf16(w_in[l])
        proj_sb = _in_proj(x2d, w_in_l[:, :sb_cols], jnp.bfloat16, N_PAIRS_SB, "in_proj_sb")
        proj_dl = _in_proj(x2d, w_in_l[:, sb_cols:], jnp.float32, N_PAIRS_DIL, "in_proj_dil")
        y_sb = _sb_attention(proj_sb, g_sb[l].reshape(N_PAIRS_SB, 1, LANES), batch, seq)
        y_dl = _dil_attention(proj_dl, g_dil[l].reshape(N_PAIRS_DIL, 1, LANES),
                              buckets, rel_bias, batch, seq)
        x1, gate = _out_proj(x2d, y_sb, y_dl, _bf16(w_out[l]),
                             ln1_g[l].reshape(1, d), ln1_b[l].reshape(1, d),
                             wrt_hi, wrt_lo, br, alpha)
        x2d = _moe(x1, gate, _bf16(w_gate[l]), _bf16(w_up[l]), _bf16(w_down[l]),
                   ln2_g[l].reshape(1, d), ln2_b[l].reshape(1, d), alpha)
    return x2d.reshape(batch, seq, d)
```

```python
import functools
import math

import jax
import jax.numpy as jnp
from jax import lax
from jax.experimental import pallas as pl
from jax.experimental.pallas import tpu as pltpu

D_MODEL = 1024
HEAD_DIM = 64
N_HEADS_SB = 8
N_HEADS_DIL = 8
D_SB = N_HEADS_SB * HEAD_DIM
D_DIL = N_HEADS_DIL * HEAD_DIM
DIL_PATTERNS = ((128, 1), (512, 4), (2048, 16))
BLOCK = 128
N_BUCKETS = 32
MAX_DISTANCE = 2048
N_EXPERTS = 16
N_GROUPS = 4
EXPERTS_PER_GROUP = N_EXPERTS // N_GROUPS
D_FF_EXPERT = 1024
LN_EPS = 1e-5
NEG_INF = -1e30

LANES = 128
N_PAIRS_SB = D_SB // LANES
N_PAIRS_DIL = D_DIL // LANES
VMEM_LIMIT = 56 * 1024 * 1024

ROW_TILE = 512
SB_TQ = 256
SB_TK = 256
SB_SKIP_BOUND = -110.0
DIL_SUPER = 2048
DIL_UNITS = DIL_SUPER // BLOCK
TOKEN_TILE_ROWS = D_MODEL // LANES
MOE_TM = 256
MOE_TM_LOG2 = MOE_TM.bit_length() - 1
PERM_TP = 512

_NT = (((1,), (1,)), ((), ()))


def _bf16(x):
    return x.astype(jnp.bfloat16)


def _split_bf16(x):
    hi = _bf16(x)
    lo = _bf16(x - hi.astype(jnp.float32))
    return hi, lo


def _dot(a, b):
    return jnp.dot(a, b, preferred_element_type=jnp.float32)


def _dot_nt(a, b):
    return lax.dot_general(a, b, _NT, preferred_element_type=jnp.float32)


def _head_rms_gain(o, g):
    r = lax.broadcasted_iota(jnp.int32, (LANES, LANES), 0) // HEAD_DIM
    c = lax.broadcasted_iota(jnp.int32, (LANES, LANES), 1) // HEAD_DIM
    same_head = jnp.where(r == c, 1.0 / HEAD_DIM, 0.0).astype(jnp.bfloat16)
    hi, lo = _split_bf16(o * o)
    ms = _dot(hi, same_head) + _dot(lo, same_head)
    return o * lax.rsqrt(ms + 1e-6) * g


def _layer_norm(x, g, b):
    mu = jnp.mean(x, axis=-1, keepdims=True)
    xc = x - mu
    var = jnp.mean(xc * xc, axis=-1, keepdims=True)
    return xc * lax.rsqrt(var + LN_EPS) * g + b


def _in_proj_kernel(x_ref, w_ref, o_ref, *, n_scaled, scale):
    x = _bf16(x_ref[...])
    n_slabs = o_ref.shape[0]
    for j in range(0, n_slabs, 2):
        res = _dot(x, w_ref[:, j * LANES:(j + 2) * LANES])
        for jj in range(2):
            blk = res[:, jj * LANES:(jj + 1) * LANES]
            if j + jj < n_scaled:
                blk = blk * scale
            o_ref[j + jj] = blk.astype(o_ref.dtype)


def _in_proj(x2d, w, out_dtype, n_scaled, name):
    n, d = x2d.shape
    n_slabs = w.shape[1] // LANES
    return pl.pallas_call(
        functools.partial(_in_proj_kernel, n_scaled=n_scaled, scale=HEAD_DIM ** -0.5),
        grid=(n // ROW_TILE,),
        in_specs=[pl.BlockSpec((ROW_TILE, d), lambda i: (i, 0)),
                  pl.BlockSpec((d, w.shape[1]), lambda i: (0, 0))],
        out_specs=pl.BlockSpec((n_slabs, ROW_TILE, LANES), lambda i: (0, i, 0)),
        out_shape=jax.ShapeDtypeStruct((n_slabs, n, LANES), out_dtype),
        compiler_params=pltpu.CompilerParams(dimension_semantics=("arbitrary",),
                                             vmem_limit_bytes=VMEM_LIMIT),
        name=name,
    )(x2d, w)


def _sb_kernel(q_ref, k_ref, v_ref, g_ref, o_ref, carry_ref, acc_ref):
    i = pl.program_id(2)
    lane = lax.broadcasted_iota(jnp.int32, (1, LANES), 1)
    first_head = lane < HEAD_DIM
    q = q_ref[...]
    zero = jnp.zeros_like(q)
    q_heads = (jnp.where(first_head, q, zero), jnp.where(first_head, zero, q))

    jj = lax.broadcasted_iota(jnp.int32, (SB_TK, SB_TK + LANES), 0)
    ss = lax.broadcasted_iota(jnp.int32, (SB_TK, SB_TK + LANES), 1)
    suffix = jnp.where((jj > ss) | (ss >= SB_TK), 1.0, 0.0).astype(jnp.bfloat16)

    tq_i = lax.broadcasted_iota(jnp.int32, (SB_TQ, SB_TK), 0)
    ts_i = lax.broadcasted_iota(jnp.int32, (SB_TQ, SB_TK), 1)
    causal = ts_i < tq_i

    carry_ref[...] = jnp.zeros_like(carry_ref)
    acc_ref[...] = jnp.zeros_like(acc_ref)

    def block(kb, diag):
        off = pl.multiple_of(kb * SB_TK, SB_TK)
        kblk = k_ref[pl.ds(off, SB_TK), :]
        vblk = v_ref[pl.ds(off, SB_TK), :]
        for h in range(2):
            z = _dot_nt(q_heads[h], kblk)
            sp = jnp.log(1.0 + jnp.exp(-jnp.abs(z)))
            log_beta = jnp.minimum(z, 0.0) - sp
            log_1m = -jnp.maximum(z, 0.0) - sp
            if diag:
                log_1m = jnp.where(causal, log_1m, 0.0)
            hi, lo = _split_bf16(log_1m)
            sfx = _dot(hi, suffix) + _dot(lo, suffix)
            c = carry_ref[h]
            between = sfx[:, :SB_TK] + jnp.concatenate([c] * (SB_TK // LANES), axis=1)
            a = jnp.exp(log_beta + between)
            if diag:
                a = jnp.where(causal, a, 0.0)
            acc_ref[h] += _dot(_bf16(a), vblk)
            carry_ref[h] = c + sfx[:, SB_TK:]

    block(i, True)

    def cond(state):
        kb, mx = state
        return jnp.logical_and(kb >= 0, mx > SB_SKIP_BOUND)

    def body(state):
        kb, _ = state
        block(kb, False)
        return kb - 1, jnp.max(carry_ref[...])

    lax.while_loop(cond, body, (i - 1, jnp.max(carry_ref[...])))

    o = jnp.where(first_head, acc_ref[0], acc_ref[1])
    o_ref[...] = _head_rms_gain(o, g_ref[...]).astype(o_ref.dtype)


def _sb_attention(proj, g, batch, seq):
    n = batch * seq
    nq = seq // SB_TQ
    return pl.pallas_call(
        _sb_kernel,
        grid=(batch, N_PAIRS_SB, nq),
        in_specs=[
            pl.BlockSpec((None, SB_TQ, LANES), lambda b, p, i: (p, b * nq + i, 0)),
            pl.BlockSpec((None, seq, LANES), lambda b, p, i: (N_PAIRS_SB + p, b, 0)),
            pl.BlockSpec((None, seq, LANES), lambda b, p, i: (2 * N_PAIRS_SB + p, b, 0)),
            pl.BlockSpec((None, 1, LANES), lambda b, p, i: (p, 0, 0)),
        ],
        out_specs=pl.BlockSpec((None, SB_TQ, LANES), lambda b, p, i: (p, b * nq + i, 0)),
        out_shape=jax.ShapeDtypeStruct((N_PAIRS_SB, n, LANES), jnp.bfloat16),
        scratch_shapes=[pltpu.VMEM((2, SB_TQ, LANES), jnp.float32),
                        pltpu.VMEM((2, SB_TQ, LANES), jnp.float32)],
        compiler_params=pltpu.CompilerParams(
            dimension_semantics=("arbitrary", "arbitrary", "arbitrary"),
            vmem_limit_bytes=VMEM_LIMIT),
        name="sb_attention",
    )(proj, proj, proj, g)


def _dil_kernel(bucket_ref, rb_ref, q_ref, k_ref, v_ref, g_ref, o_ref,
                bias_ref, obr_ref, lse_ref):
    b = pl.program_id(0)
    p = pl.program_id(1)
    i = pl.program_id(2)
    n_br = len(DIL_PATTERNS)

    qi = lax.broadcasted_iota(jnp.int32, (BLOCK, 2 * BLOCK), 0)
    kj = lax.broadcasted_iota(jnp.int32, (BLOCK, 2 * BLOCK), 1)
    steps = qi + BLOCK - kj

    @pl.when((b == 0) & (p == 0) & (i == 0))
    def _():
        def per_head(h, _):
            for gidx, (window, dilation) in enumerate(DIL_PATTERNS):
                bk = bucket_ref[gidx]
                acc = jnp.zeros((BLOCK, 2 * BLOCK), jnp.float32)
                for bb in range(N_BUCKETS):
                    acc = jnp.where(bk == bb, rb_ref[bb, h], acc)
                valid = (steps >= 0) & (steps <= window // dilation)
                slot = 2 * (gidx * N_HEADS_DIL + h)
                bias_ref[slot] = jnp.where(valid, acc, NEG_INF)
                bias_ref[slot + 1] = jnp.where(valid & (kj >= BLOCK), acc, NEG_INF)
            return 0
        lax.fori_loop(0, N_HEADS_DIL, per_head, 0)

    lane = lax.broadcasted_iota(jnp.int32, (1, LANES), 1)
    first_head = lane < HEAD_DIM
    t0 = i * DIL_SUPER

    def strided(ref, start, dilation):
        if dilation == 1:
            return ref[pl.ds(start, BLOCK), :]
        return ref[pl.ds(start, BLOCK, stride=dilation), :]

    for gidx, (window, dilation) in enumerate(DIL_PATTERNS):
        units_per_residue = DIL_UNITS // dilation
        shift = units_per_residue.bit_length() - 1

        def unit(u, _, gidx=gidx, dilation=dilation, units_per_residue=units_per_residue, shift=shift):
            r = lax.shift_right_logical(u, shift)
            cb = u & (units_per_residue - 1)
            sq = r + cb * (dilation * BLOCK)
            cur = t0 + sq
            prev = cur - dilation * BLOCK
            prev_ok = prev >= 0
            prev_c = jnp.where(prev_ok, prev, cur)
            no_prev = jnp.where(prev_ok, 0, 1)

            qb = _bf16(strided(q_ref, sq, dilation))
            kk = _bf16(jnp.concatenate([strided(k_ref, prev_c, dilation),
                                        strided(k_ref, cur, dilation)], axis=0))
            vv = _bf16(jnp.concatenate([strided(v_ref, prev_c, dilation),
                                        strided(v_ref, cur, dilation)], axis=0))
            zero = jnp.zeros_like(qb)
            outs, lses = [], []
            for h in range(2):
                qm = jnp.where(first_head, qb, zero) if h == 0 else jnp.where(first_head, zero, qb)
                z = _dot_nt(qm, kk)
                lg = z + bias_ref[2 * (gidx * N_HEADS_DIL + 2 * p + h) + no_prev]
                m = jnp.max(lg, axis=-1, keepdims=True)
                pe = jnp.exp(lg - m)
                den = jnp.sum(pe, axis=-1, keepdims=True)
                outs.append(_dot(_bf16(pe), vv) / den)
                lses.append(m + jnp.log(den))
            o_tile = jnp.where(first_head, outs[0], outs[1])
            l_tile = jnp.where(first_head, lses[0], lses[1])
            if dilation == 1:
                idx = pl.ds(sq, BLOCK)
            else:
                idx = pl.ds(sq, BLOCK, stride=dilation)
            obr_ref.at[gidx][idx, :] = o_tile
            lse_ref.at[gidx][idx, :] = l_tile
            return 0

        lax.fori_loop(0, DIL_UNITS, unit, 0, unroll=8)

    chunk = 256

    def combine(c, _):
        rows = pl.ds(pl.multiple_of(c * chunk, chunk), chunk)
        ls = [lse_ref[gidx, rows, :] for gidx in range(n_br)]
        m = functools.reduce(jnp.maximum, ls)
        es = [jnp.exp(l - m) for l in ls]
        num = sum(e * obr_ref[gidx, rows, :] for gidx, e in enumerate(es))
        o = num / sum(es)
        o_ref[rows, :] = _head_rms_gain(o, g_ref[...]).astype(o_ref.dtype)
        return 0

    lax.fori_loop(0, DIL_SUPER // chunk, combine, 0)


def _dil_attention(proj, g, buckets, rel_bias, batch, seq):
    n = batch * seq
    ns = seq // DIL_SUPER
    n_br = len(DIL_PATTERNS)
    return pl.pallas_call(
        _dil_kernel,
        grid=(batch, N_PAIRS_DIL, ns),
        in_specs=[
            pl.BlockSpec((n_br, BLOCK, 2 * BLOCK), lambda b, p, i: (0, 0, 0)),
            pl.BlockSpec(memory_space=pltpu.SMEM),
            pl.BlockSpec((None, DIL_SUPER, LANES), lambda b, p, i: (p, b * ns + i, 0)),
            pl.BlockSpec((None, seq, LANES), lambda b, p, i: (N_PAIRS_DIL + p, b, 0)),
            pl.BlockSpec((None, seq, LANES), lambda b, p, i: (2 * N_PAIRS_DIL + p, b, 0)),
            pl.BlockSpec((None, 1, LANES), lambda b, p, i: (p, 0, 0)),
        ],
        out_specs=pl.BlockSpec((None, DIL_SUPER, LANES), lambda b, p, i: (p, b * ns + i, 0)),
        out_shape=jax.ShapeDtypeStruct((N_PAIRS_DIL, n, LANES), jnp.bfloat16),
        scratch_shapes=[pltpu.VMEM((2 * n_br * N_HEADS_DIL, BLOCK, 2 * BLOCK), jnp.float32),
                        pltpu.VMEM((n_br, DIL_SUPER, LANES), jnp.float32),
                        pltpu.VMEM((n_br, DIL_SUPER, LANES), jnp.float32)],
        compiler_params=pltpu.CompilerParams(
            dimension_semantics=("arbitrary", "arbitrary", "arbitrary"),
            vmem_limit_bytes=VMEM_LIMIT),
        name="dilated_attention",
    )(buckets, rel_bias, proj, proj, proj, g)


def _router(x1, wrt_hi, wrt_lo, br):
    tm = x1.shape[0]
    xh, xl = _split_bf16(x1)
    lt = _dot_nt(wrt_hi, xh) + _dot_nt(wrt_hi, xl) + _dot_nt(wrt_lo, xh) + br
    rows = [lt[e:e + 1, :] for e in range(N_EXPERTS)]
    m = functools.reduce(jnp.maximum, rows)
    ex = [jnp.exp(r - m) for r in rows]
    tot = functools.reduce(lambda a, c: a + c, ex)
    probs = [e / tot for e in ex]

    keep, score = [], []
    for gi in range(N_GROUPS):
        members = range(gi * EXPERTS_PER_GROUP, (gi + 1) * EXPERTS_PER_GROUP)
        s = None
        for e in members:
            rank = None
            for o in members:
                if o == e:
                    continue
                beats = (probs[o] > probs[e]) | ((probs[o] == probs[e]) & (o < e))
                beats = beats.astype(jnp.int32)
                rank = beats if rank is None else rank + beats
            k = rank < 2
            keep.append(k)
            contrib = jnp.where(k, probs[e], 0.0)
            s = contrib if s is None else s + contrib
        score.append(s)

    sel, gates = [], []
    for gi in range(N_GROUPS):
        chosen = None
        for o in range(N_GROUPS):
            if o == gi:
                continue
            c = (score[gi] > score[o]) if o < gi else (score[gi] >= score[o])
            chosen = c if chosen is None else (chosen & c)
        for e in range(gi * EXPERTS_PER_GROUP, (gi + 1) * EXPERTS_PER_GROUP):
            s = keep[e] & chosen
            sel.append(s)
            gates.append(jnp.where(s, probs[e] / score[gi], 0.0))

    idx0 = functools.reduce(jnp.minimum, [jnp.where(sel[e], e, N_EXPERTS) for e in range(N_EXPERTS)])
    idx1 = functools.reduce(jnp.maximum, [jnp.where(sel[e], e, -1) for e in range(N_EXPERTS)])
    g0 = sum(jnp.where(idx0 == e, gates[e], 0.0) for e in range(N_EXPERTS))
    g1 = sum(jnp.where(idx1 == e, gates[e], 0.0) for e in range(N_EXPERTS))

    row_id = lax.broadcasted_iota(jnp.int32, (N_EXPERTS, tm), 0)
    selmat = jnp.zeros((N_EXPERTS, tm), jnp.float32)
    for e in range(N_EXPERTS):
        selmat = jnp.where((row_id == e) & sel[e], 1.0, selmat)
    return idx0, idx1, g0, g1, selmat


def _to_token_tiles(ref, x):
    rows = x.shape[0]
    for s in range(TOKEN_TILE_ROWS):
        ref[pl.ds(s, rows, stride=TOKEN_TILE_ROWS), :] = x[:, s * LANES:(s + 1) * LANES]


def _from_token_tiles(ref, rows):
    return jnp.concatenate([ref[pl.ds(s, rows, stride=TOKEN_TILE_ROWS), :]
                            for s in range(TOKEN_TILE_ROWS)], axis=1)


def _out_proj_kernel(x_ref, ysb_ref, ydl_ref, w_ref, lng_ref, lnb_ref,
                     wrh_ref, wrl_ref, br_ref,
                     x1_ref, x1t_ref, meta_ref, gcol_ref, cnt_ref, carry_ref, *, alpha):
    i = pl.program_id(0)
    tm = x_ref.shape[0]

    @pl.when(i == 0)
    def _():
        carry_ref[...] = jnp.zeros_like(carry_ref)

    y = jnp.concatenate([ysb_ref[j] for j in range(N_PAIRS_SB)]
                        + [ydl_ref[j] for j in range(N_PAIRS_DIL)], axis=1)
    h = _dot(y, w_ref[...])
    x1 = _layer_norm(alpha * x_ref[...] + h, lng_ref[...], lnb_ref[...])
    x1_ref[...] = x1
    _to_token_tiles(x1t_ref, x1)

    idx0, idx1, g0, g1, selmat = _router(x1, wrh_ref[...], wrl_ref[...], br_ref[...])

    jj = lax.broadcasted_iota(jnp.int32, (tm, tm + LANES), 0)
    ss = lax.broadcasted_iota(jnp.int32, (tm, tm + LANES), 1)
    before = jnp.where((jj < ss) | (ss >= tm), 1.0, 0.0).astype(jnp.bfloat16)
    pc = _dot(_bf16(selmat), before)
    carry = carry_ref[...]
    rankmat = pc[:, :tm] + jnp.concatenate([carry] * (tm // LANES), axis=1)
    carry = carry + pc[:, tm:]
    carry_ref[...] = carry
    cnt_ref[...] = carry

    rank0 = sum(jnp.where(idx0 == e, rankmat[e:e + 1, :], 0.0) for e in range(N_EXPERTS))
    rank1 = sum(jnp.where(idx1 == e, rankmat[e:e + 1, :], 0.0) for e in range(N_EXPERTS))

    row8 = lax.broadcasted_iota(jnp.int32, (8, tm), 0)
    meta = jnp.zeros((8, tm), jnp.int32)
    for r, v in enumerate((idx0, idx1, rank0.astype(jnp.int32), rank1.astype(jnp.int32))):
        meta = jnp.where(row8 == r, v, meta)
    meta_ref[...] = meta

    rowl = lax.broadcasted_iota(jnp.int32, (LANES, tm), 0)
    gt = jnp.where(rowl == 0, g0, jnp.where(rowl == 1, g1, 0.0))
    gcol_ref[...] = gt.T


def _out_proj(x2d, y_sb, y_dl, w_out, ln_g, ln_b, wrt_hi, wrt_lo, br, alpha):
    n, d = x2d.shape
    const = lambda i: (0, 0)
    return pl.pallas_call(
        functools.partial(_out_proj_kernel, alpha=alpha),
        grid=(n // ROW_TILE,),
        in_specs=[
            pl.BlockSpec((ROW_TILE, d), lambda i: (i, 0)),
            pl.BlockSpec((N_PAIRS_SB, ROW_TILE, LANES), lambda i: (0, i, 0)),
            pl.BlockSpec((N_PAIRS_DIL, ROW_TILE, LANES), lambda i: (0, i, 0)),
            pl.BlockSpec((d, d), const),
            pl.BlockSpec((1, d), const),
            pl.BlockSpec((1, d), const),
            pl.BlockSpec((N_EXPERTS, d), const),
            pl.BlockSpec((N_EXPERTS, d), const),
            pl.BlockSpec((N_EXPERTS, 1), const),
        ],
        out_specs=[pl.BlockSpec((ROW_TILE, d), lambda i: (i, 0)),
                   pl.BlockSpec((ROW_TILE * TOKEN_TILE_ROWS, LANES), lambda i: (i, 0)),
                   pl.BlockSpec((8, ROW_TILE), lambda i: (0, i)),
                   pl.BlockSpec((ROW_TILE, LANES), lambda i: (i, 0)),
                   pl.BlockSpec((N_EXPERTS, LANES), const)],
        out_shape=[jax.ShapeDtypeStruct((n, d), jnp.float32),
                   jax.ShapeDtypeStruct((n * TOKEN_TILE_ROWS, LANES), jnp.float32),
                   jax.ShapeDtypeStruct((8, n), jnp.int32),
                   jax.ShapeDtypeStruct((n, LANES), jnp.float32),
                   jax.ShapeDtypeStruct((N_EXPERTS, LANES), jnp.float32)],
        scratch_shapes=[pltpu.VMEM((N_EXPERTS, LANES), jnp.float32)],
        compiler_params=pltpu.CompilerParams(dimension_semantics=("arbitrary",),
                                             vmem_limit_bytes=VMEM_LIMIT),
        name="out_proj_ln_router",
    )(x2d, y_sb, y_dl, w_out, ln_g, ln_b, wrt_hi, wrt_lo, br)


def _segment_offsets(cnt_ref, off_ref):
    acc = jnp.int32(0)
    padded = []
    for e in range(N_EXPERTS):
        off_ref[e] = acc
        pc = lax.shift_left(lax.shift_right_logical(cnt_ref[e] + (MOE_TM - 1), MOE_TM_LOG2),
                            MOE_TM_LOG2)
        padded.append(pc)
        acc = acc + pc
    return padded, acc


def _row_copy(src_ref, src_row, dst_ref, dst_row, sem):
    def tile(row):
        if isinstance(row, int):
            return pl.ds(row * TOKEN_TILE_ROWS, TOKEN_TILE_ROWS)
        return pl.ds(pl.multiple_of(row * TOKEN_TILE_ROWS, TOKEN_TILE_ROWS), TOKEN_TILE_ROWS)
    return pltpu.make_async_copy(src_ref.at[tile(src_row), :], dst_ref.at[tile(dst_row), :], sem)


def _wait_rows(src_ref, dst_ref, sem, count):
    def body(_, c):
        _row_copy(src_ref, 0, dst_ref, 0, sem).wait()
        return c
    lax.fori_loop(0, count, body, 0)


def _dispatch_kernel(cnt_ref, e0_ref, e1_ref, r0_ref, r1_ref, x_ref, xs_ref, te_ref,
                     off_ref, sems, *, max_tiles):
    i = pl.program_id(0)
    tp = e0_ref.shape[0]
    padded, total = _segment_offsets(cnt_ref, off_ref)
    sem = sems.at[0]

    @pl.when(i == 0)
    def _():
        n_tiles = lax.shift_right_logical(total, MOE_TM_LOG2)
        for e in range(N_EXPERTS):
            first = lax.shift_right_logical(off_ref[e], MOE_TM_LOG2)
            count = lax.shift_right_logical(padded[e], MOE_TM_LOG2)

            def fill(k, c, first=first, e=e):
                te_ref[first + k] = e
                return c
            lax.fori_loop(0, count, fill, 0)
        last = te_ref[n_tiles - 1]

        def fill_tail(k, c):
            te_ref[k] = last
            return c
        lax.fori_loop(n_tiles, max_tiles, fill_tail, 0)
        te_ref[max_tiles] = n_tiles

        tile_rows = MOE_TM * TOKEN_TILE_ROWS

        def init_tail(k, c):
            dst = pl.ds(pl.multiple_of(k * tile_rows, tile_rows), tile_rows)
            cp = pltpu.make_async_copy(x_ref.at[pl.ds(0, tile_rows), :], xs_ref.at[dst, :], sems.at[1])
            cp.start()
            cp.wait()
            return c
        lax.fori_loop(n_tiles, max_tiles, init_tail, 0)

        n_pad = jnp.int32(0)
        for e in range(N_EXPERTS):
            def pad(r, c, e=e):
                _row_copy(x_ref, 0, xs_ref, off_ref[e] + r, sem).start()
                return c
            lax.fori_loop(cnt_ref[e], padded[e], pad, 0)
            n_pad = n_pad + (padded[e] - cnt_ref[e])
        _wait_rows(x_ref, xs_ref, sem, n_pad)

    def issue(j, c):
        tok = i * tp + j
        _row_copy(x_ref, tok, xs_ref, off_ref[e0_ref[j]] + r0_ref[j], sem).start()
        _row_copy(x_ref, tok, xs_ref, off_ref[e1_ref[j]] + r1_ref[j], sem).start()
        return c
    lax.fori_loop(0, tp, issue, 0, unroll=8)
    _wait_rows(x_ref, xs_ref, sem, 2 * tp)


def _collect_kernel(cnt_ref, e0_ref, e1_ref, r0_ref, r1_ref, ys_ref, y0_ref, y1_ref,
                    off_ref, sem):
    i = pl.program_id(0)
    tp = e0_ref.shape[0]
    _segment_offsets(cnt_ref, off_ref)

    def issue(j, c):
        tok = i * tp + j
        _row_copy(ys_ref, off_ref[e0_ref[j]] + r0_ref[j], y0_ref, tok, sem.at[0]).start()
        _row_copy(ys_ref, off_ref[e1_ref[j]] + r1_ref[j], y1_ref, tok, sem.at[1]).start()
        return c
    lax.fori_loop(0, tp, issue, 0, unroll=8)
    _wait_rows(ys_ref, y0_ref, sem.at[0], tp)
    _wait_rows(ys_ref, y1_ref, sem.at[1], tp)


def _smem_vec(tp):
    return pl.BlockSpec((tp,), lambda i: (i,), memory_space=pltpu.SMEM)


def _dispatch(counts, e0, e1, r0, r1, x1t, max_tiles):
    n = e0.shape[0]
    any_spec = pl.BlockSpec(memory_space=pl.ANY)
    smem = pl.BlockSpec(memory_space=pltpu.SMEM)
    return pl.pallas_call(
        functools.partial(_dispatch_kernel, max_tiles=max_tiles),
        grid=(n // PERM_TP,),
        in_specs=[smem] + [_smem_vec(PERM_TP)] * 4 + [any_spec],
        out_specs=[any_spec, smem],
        out_shape=[jax.ShapeDtypeStruct((max_tiles * MOE_TM * TOKEN_TILE_ROWS, LANES), jnp.float32),
                   jax.ShapeDtypeStruct((max_tiles + 1,), jnp.int32)],
        scratch_shapes=[pltpu.SMEM((N_EXPERTS,), jnp.int32), pltpu.SemaphoreType.DMA((2,))],
        compiler_params=pltpu.CompilerParams(dimension_semantics=("arbitrary",)),
        name="moe_dispatch",
    )(counts, e0, e1, r0, r1, x1t)


def _collect(counts, e0, e1, r0, r1, ys):
    n = e0.shape[0]
    any_spec = pl.BlockSpec(memory_space=pl.ANY)
    smem = pl.BlockSpec(memory_space=pltpu.SMEM)
    out = jax.ShapeDtypeStruct((n * TOKEN_TILE_ROWS, LANES), jnp.float32)
    return pl.pallas_call(
        _collect_kernel,
        grid=(n // PERM_TP,),
        in_specs=[smem] + [_smem_vec(PERM_TP)] * 4 + [any_spec],
        out_specs=[any_spec, any_spec],
        out_shape=[out, out],
        scratch_shapes=[pltpu.SMEM((N_EXPERTS,), jnp.int32), pltpu.SemaphoreType.DMA((2,))],
        compiler_params=pltpu.CompilerParams(dimension_semantics=("arbitrary",)),
        name="moe_collect",
    )(counts, e0, e1, r0, r1, ys)


def _experts_kernel(te_ref, xs_ref, wg_ref, wu_ref, wd_ref, ys_ref, wgb_ref, wub_ref, wdb_ref,
                    *, max_tiles):
    i = pl.program_id(0)

    @pl.when(i < te_ref[max_tiles])
    def _():
        @pl.when((i == 0) | (te_ref[i] != te_ref[jnp.maximum(i - 1, 0)]))
        def _():
            wgb_ref[...] = _bf16(wg_ref[...])
            wub_ref[...] = _bf16(wu_ref[...])
            wdb_ref[...] = _bf16(wd_ref[...])

        xb = _bf16(_from_token_tiles(xs_ref, MOE_TM))
        gt = _dot(xb, wgb_ref[...])
        up = _dot(xb, wub_ref[...])
        h = gt / (1.0 + jnp.exp(-gt)) * up
        _to_token_tiles(ys_ref, _dot(_bf16(h), wdb_ref[...]))

    @pl.when(i >= te_ref[max_tiles])
    def _():
        ys_ref[...] = jnp.zeros_like(ys_ref)


def _experts(te, xs, w_gate, w_up, w_down, layer, max_tiles):
    d, f = w_gate.shape[-2:]
    rows = MOE_TM * TOKEN_TILE_ROWS
    tile = lambda i, te: (i, 0)
    expert = lambda i, te: (layer, te[i], 0, 0)
    return pl.pallas_call(
        functools.partial(_experts_kernel, max_tiles=max_tiles),
        grid_spec=pltpu.PrefetchScalarGridSpec(
            num_scalar_prefetch=1,
            grid=(max_tiles,),
            in_specs=[pl.BlockSpec((rows, LANES), tile),
                      pl.BlockSpec((None, None, d, f), expert),
                      pl.BlockSpec((None, None, d, f), expert),
                      pl.BlockSpec((None, None, f, d), expert)],
            out_specs=pl.BlockSpec((rows, LANES), tile),
            scratch_shapes=[pltpu.VMEM((d, f), jnp.bfloat16),
                            pltpu.VMEM((d, f), jnp.bfloat16),
                            pltpu.VMEM((f, d), jnp.bfloat16)]),
        out_shape=jax.ShapeDtypeStruct(xs.shape, jnp.float32),
        compiler_params=pltpu.CompilerParams(dimension_semantics=("arbitrary",),
                                             vmem_limit_bytes=VMEM_LIMIT),
        name="moe_experts",
    )(te, xs, w_gate, w_up, w_down)


def _combine_kernel(x_ref, y0_ref, y1_ref, gcol_ref, lng_ref, lnb_ref, o_ref, *, alpha):
    tm = x_ref.shape[0]
    lane = lax.broadcasted_iota(jnp.int32, (1, LANES), 1)
    gc = gcol_ref[...]
    g0 = jnp.sum(jnp.where(lane == 0, gc, 0.0), axis=-1, keepdims=True)
    g1 = jnp.sum(jnp.where(lane == 1, gc, 0.0), axis=-1, keepdims=True)
    y = g0 * _from_token_tiles(y0_ref, tm) + g1 * _from_token_tiles(y1_ref, tm)
    o_ref[...] = _layer_norm(alpha * x_ref[...] + y, lng_ref[...], lnb_ref[...])


def _combine(x1, y0t, y1t, gcol, ln_g, ln_b, alpha):
    n, d = x1.shape
    const = lambda i: (0, 0)
    tiles = pl.BlockSpec((ROW_TILE * TOKEN_TILE_ROWS, LANES), lambda i: (i, 0))
    return pl.pallas_call(
        functools.partial(_combine_kernel, alpha=alpha),
        grid=(n // ROW_TILE,),
        in_specs=[pl.BlockSpec((ROW_TILE, d), lambda i: (i, 0)), tiles, tiles,
                  pl.BlockSpec((ROW_TILE, LANES), lambda i: (i, 0)),
                  pl.BlockSpec((1, d), const), pl.BlockSpec((1, d), const)],
        out_specs=pl.BlockSpec((ROW_TILE, d), lambda i: (i, 0)),
        out_shape=jax.ShapeDtypeStruct((n, d), jnp.float32),
        compiler_params=pltpu.CompilerParams(dimension_semantics=("arbitrary",),
                                             vmem_limit_bytes=VMEM_LIMIT),
        name="moe_combine_ln",
    )(x1, y0t, y1t, gcol, ln_g, ln_b)


def _moe(x1, x1t, meta, gcol, cnt, w_gate, w_up, w_down, layer, ln_g, ln_b, alpha):
    n = x1.shape[0]
    max_tiles = (2 * n + N_EXPERTS * (MOE_TM - 1)) // MOE_TM
    counts = cnt[:, 0].astype(jnp.int32)
    e0, e1, r0, r1 = meta[0], meta[1], meta[2], meta[3]
    xs, te = _dispatch(counts, e0, e1, r0, r1, x1t, max_tiles)
    ys = _experts(te, xs, w_gate, w_up, w_down, layer, max_tiles)
    y0t, y1t = _collect(counts, e0, e1, r0, r1, ys)
    return _combine(x1, y0t, y1t, gcol, ln_g, ln_b, alpha)


def _t5_bucket(dist):
    max_exact = N_BUCKETS // 2
    d = jnp.maximum(dist, 0)
    large = max_exact + (jnp.log(jnp.maximum(d, 1).astype(jnp.float32) / max_exact)
                         / math.log(MAX_DISTANCE / max_exact) * (N_BUCKETS - max_exact)).astype(jnp.int32)
    large = jnp.minimum(large, N_BUCKETS - 1)
    return jnp.where(d < max_exact, d, large)


def _bucket_tables():
    qi = jnp.arange(BLOCK)[:, None]
    kj = jnp.arange(2 * BLOCK)[None, :]
    steps = qi + BLOCK - kj
    return jnp.stack([_t5_bucket(jnp.maximum(steps, 0) * dilation)
                      for _, dilation in DIL_PATTERNS]).astype(jnp.int32)


def kernel(x, w_in, g_sb, g_dil, w_out, ln1_g, ln1_b, ln2_g, ln2_b, rel_bias,
           w_router, b_router, w_gate, w_up, w_down):
    batch, seq, d = x.shape
    depth = w_in.shape[0]
    alpha = (2.0 * depth) ** 0.25
    n = batch * seq
    x2d = x.reshape(n, d)

    buckets = _bucket_tables()
    wrt = w_router.T.astype(jnp.float32)
    wrt_hi = _bf16(wrt)
    wrt_lo = _bf16(wrt - wrt_hi.astype(jnp.float32))
    br = b_router.astype(jnp.float32).reshape(N_EXPERTS, 1)
    sb_cols = 3 * D_SB

    for l in range(depth):
        w_in_l = _bf16(w_in[l])
        proj_sb = _in_proj(x2d, w_in_l[:, :sb_cols], jnp.bfloat16, N_PAIRS_SB, "in_proj_sb")
        proj_dl = _in_proj(x2d, w_in_l[:, sb_cols:], jnp.float32, N_PAIRS_DIL, "in_proj_dil")
        y_sb = _sb_attention(proj_sb, g_sb[l].reshape(N_PAIRS_SB, 1, LANES), batch, seq)
        y_dl = _dil_attention(proj_dl, g_dil[l].reshape(N_PAIRS_DIL, 1, LANES),
                              buckets, rel_bias, batch, seq)
        x1, x1t, meta, gcol, cnt = _out_proj(x2d, y_sb, y_dl, _bf16(w_out[l]),
                                             ln1_g[l].reshape(1, d), ln1_b[l].reshape(1, d),
                                             wrt_hi, wrt_lo, br, alpha)
        x2d = _moe(x1, x1t, meta, gcol, cnt, w_gate, w_up, w_down, l,
                   ln2_g[l].reshape(1, d), ln2_b[l].reshape(1, d), alpha)
    return x2d.reshape(batch, seq, d)
```

```python
import functools
import math

import jax
import jax.numpy as jnp
from jax import lax
from jax.experimental import pallas as pl
from jax.experimental.pallas import tpu as pltpu

D_MODEL = 1024
HEAD_DIM = 64
N_HEADS_SB = 8
N_HEADS_DIL = 8
D_SB = N_HEADS_SB * HEAD_DIM
D_DIL = N_HEADS_DIL * HEAD_DIM
DIL_PATTERNS = ((128, 1), (512, 4), (2048, 16))
BLOCK = 128
N_BUCKETS = 32
MAX_DISTANCE = 2048
N_EXPERTS = 16
N_GROUPS = 4
EXPERTS_PER_GROUP = N_EXPERTS // N_GROUPS
D_FF_EXPERT = 1024
LN_EPS = 1e-5
NEG_INF = -1e30

LANES = 128
N_PAIRS_SB = D_SB // LANES
N_PAIRS_DIL = D_DIL // LANES
VMEM_LIMIT = 56 * 1024 * 1024

ROW_TILE = 512
SB_TQ = 256
SB_TK = 256
SB_SKIP_BOUND = -110.0
DIL_SUPER = 2048
DIL_UNITS = DIL_SUPER // BLOCK
TOKEN_TILE_ROWS = D_MODEL // LANES
MOE_TM = 256
MOE_TM_LOG2 = MOE_TM.bit_length() - 1
PERM_TP = 512

_NT = (((1,), (1,)), ((), ()))


def _bf16(x):
    return x.astype(jnp.bfloat16)


def _split_bf16(x):
    hi = _bf16(x)
    lo = _bf16(x - hi.astype(jnp.float32))
    return hi, lo


def _dot(a, b):
    return jnp.dot(a, b, preferred_element_type=jnp.float32)


def _dot_nt(a, b):
    return lax.dot_general(a, b, _NT, preferred_element_type=jnp.float32)


def _head_rms_gain(o, g):
    r = lax.broadcasted_iota(jnp.int32, (LANES, LANES), 0) // HEAD_DIM
    c = lax.broadcasted_iota(jnp.int32, (LANES, LANES), 1) // HEAD_DIM
    same_head = jnp.where(r == c, 1.0 / HEAD_DIM, 0.0).astype(jnp.bfloat16)
    hi, lo = _split_bf16(o * o)
    ms = _dot(hi, same_head) + _dot(lo, same_head)
    return o * lax.rsqrt(ms + 1e-6) * g


def _layer_norm(x, g, b):
    mu = jnp.mean(x, axis=-1, keepdims=True)
    xc = x - mu
    var = jnp.mean(xc * xc, axis=-1, keepdims=True)
    return xc * lax.rsqrt(var + LN_EPS) * g + b


def _in_proj_kernel(x_ref, w_ref, o_ref, *, n_scaled, scale):
    x = _bf16(x_ref[...])
    n_slabs = o_ref.shape[0]
    for j in range(0, n_slabs, 2):
        res = _dot(x, w_ref[:, j * LANES:(j + 2) * LANES])
        for jj in range(2):
            blk = res[:, jj * LANES:(jj + 1) * LANES]
            if j + jj < n_scaled:
                blk = blk * scale
            o_ref[j + jj] = blk.astype(o_ref.dtype)


def _in_proj(x2d, w, out_dtype, n_scaled, name):
    n, d = x2d.shape
    n_slabs = w.shape[1] // LANES
    return pl.pallas_call(
        functools.partial(_in_proj_kernel, n_scaled=n_scaled, scale=HEAD_DIM ** -0.5),
        grid=(n // ROW_TILE,),
        in_specs=[pl.BlockSpec((ROW_TILE, d), lambda i: (i, 0)),
                  pl.BlockSpec((d, w.shape[1]), lambda i: (0, 0))],
        out_specs=pl.BlockSpec((n_slabs, ROW_TILE, LANES), lambda i: (0, i, 0)),
        out_shape=jax.ShapeDtypeStruct((n_slabs, n, LANES), out_dtype),
        compiler_params=pltpu.CompilerParams(dimension_semantics=("arbitrary",),
                                             vmem_limit_bytes=VMEM_LIMIT),
        name=name,
    )(x2d, w)


def _sb_kernel(q_ref, k_ref, v_ref, g_ref, o_ref, carry_ref, acc_ref):
    i = pl.program_id(2)
    lane = lax.broadcasted_iota(jnp.int32, (1, LANES), 1)
    first_head = lane < HEAD_DIM
    q = q_ref[...]
    zero = jnp.zeros_like(q)
    q_heads = (jnp.where(first_head, q, zero), jnp.where(first_head, zero, q))

    jj = lax.broadcasted_iota(jnp.int32, (SB_TK, SB_TK + LANES), 0)
    ss = lax.broadcasted_iota(jnp.int32, (SB_TK, SB_TK + LANES), 1)
    suffix = jnp.where((jj > ss) | (ss >= SB_TK), 1.0, 0.0).astype(jnp.bfloat16)

    tq_i = lax.broadcasted_iota(jnp.int32, (SB_TQ, SB_TK), 0)
    ts_i = lax.broadcasted_iota(jnp.int32, (SB_TQ, SB_TK), 1)
    causal = ts_i < tq_i

    carry_ref[...] = jnp.zeros_like(carry_ref)
    acc_ref[...] = jnp.zeros_like(acc_ref)

    def block(kb, diag):
        off = pl.multiple_of(kb * SB_TK, SB_TK)
        kblk = k_ref[pl.ds(off, SB_TK), :]
        vblk = v_ref[pl.ds(off, SB_TK), :]
        for h in range(2):
            z = _dot_nt(q_heads[h], kblk)
            sp = jnp.log(1.0 + jnp.exp(-jnp.abs(z)))
            log_beta = jnp.minimum(z, 0.0) - sp
            log_1m = -jnp.maximum(z, 0.0) - sp
            if diag:
                log_1m = jnp.where(causal, log_1m, 0.0)
            sfx = _dot(_bf16(log_1m), suffix)
            c = carry_ref[h]
            between = sfx[:, :SB_TK] + jnp.concatenate([c] * (SB_TK // LANES), axis=1)
            a = jnp.exp(log_beta + between)
            if diag:
                a = jnp.where(causal, a, 0.0)
            acc_ref[h] += _dot(_bf16(a), vblk)
            carry_ref[h] = c + sfx[:, SB_TK:]

    block(i, True)

    def cond(state):
        kb, mx = state
        return jnp.logical_and(kb >= 0, mx > SB_SKIP_BOUND)

    def body(state):
        kb, _ = state
        block(kb, False)
        return kb - 1, jnp.max(carry_ref[...])

    lax.while_loop(cond, body, (i - 1, jnp.max(carry_ref[...])))

    o = jnp.where(first_head, acc_ref[0], acc_ref[1])
    o_ref[...] = _head_rms_gain(o, g_ref[...]).astype(o_ref.dtype)


def _sb_attention(proj, g, batch, seq):
    n = batch * seq
    nq = seq // SB_TQ
    return pl.pallas_call(
        _sb_kernel,
        grid=(batch, N_PAIRS_SB, nq),
        in_specs=[
            pl.BlockSpec((None, SB_TQ, LANES), lambda b, p, i: (p, b * nq + i, 0)),
            pl.BlockSpec((None, seq, LANES), lambda b, p, i: (N_PAIRS_SB + p, b, 0)),
            pl.BlockSpec((None, seq, LANES), lambda b, p, i: (2 * N_PAIRS_SB + p, b, 0)),
            pl.BlockSpec((None, 1, LANES), lambda b, p, i: (p, 0, 0)),
        ],
        out_specs=pl.BlockSpec((None, SB_TQ, LANES), lambda b, p, i: (p, b * nq + i, 0)),
        out_shape=jax.ShapeDtypeStruct((N_PAIRS_SB, n, LANES), jnp.bfloat16),
        scratch_shapes=[pltpu.VMEM((2, SB_TQ, LANES), jnp.float32),
                        pltpu.VMEM((2, SB_TQ, LANES), jnp.float32)],
        compiler_params=pltpu.CompilerParams(
            dimension_semantics=("arbitrary", "arbitrary", "arbitrary"),
            vmem_limit_bytes=VMEM_LIMIT),
        name="sb_attention",
    )(proj, proj, proj, g)


def _dil_kernel(tbias_ref, q_ref, k_ref, v_ref, g_ref, o_ref,
                bias_ref, obr_ref, lse_ref):
    b = pl.program_id(0)
    p = pl.program_id(1)
    i = pl.program_id(2)
    n_br = len(DIL_PATTERNS)

    qi = lax.broadcasted_iota(jnp.int32, (BLOCK, 2 * BLOCK), 0)
    kj = lax.broadcasted_iota(jnp.int32, (BLOCK, 2 * BLOCK), 1)
    steps = qi + BLOCK - kj

    assert all(window // dilation == BLOCK for window, dilation in DIL_PATTERNS)
    valid = (steps >= 0) & (steps <= BLOCK)

    @pl.when((b == 0) & (p == 0) & (i == 0))
    def _():
        def per_tile(s, _):
            tile = tbias_ref[s]
            bias_ref[2 * s] = jnp.where(valid, tile, NEG_INF)
            bias_ref[2 * s + 1] = jnp.where(valid & (kj >= BLOCK), tile, NEG_INF)
            return 0
        lax.fori_loop(0, n_br * N_HEADS_DIL, per_tile, 0)

    lane = lax.broadcasted_iota(jnp.int32, (1, LANES), 1)
    first_head = lane < HEAD_DIM
    t0 = i * DIL_SUPER

    def strided(ref, start, dilation):
        if dilation == 1:
            return ref[pl.ds(start, BLOCK), :]
        return ref[pl.ds(start, BLOCK, stride=dilation), :]

    for gidx, (window, dilation) in enumerate(DIL_PATTERNS):
        units_per_residue = DIL_UNITS // dilation
        shift = units_per_residue.bit_length() - 1

        def unit(u, _, gidx=gidx, dilation=dilation, units_per_residue=units_per_residue, shift=shift):
            r = lax.shift_right_logical(u, shift)
            cb = u & (units_per_residue - 1)
            sq = r + cb * (dilation * BLOCK)
            cur = t0 + sq
            prev = cur - dilation * BLOCK
            prev_ok = prev >= 0
            prev_c = jnp.where(prev_ok, prev, cur)
            no_prev = jnp.where(prev_ok, 0, 1)

            qb = _bf16(strided(q_ref, sq, dilation))
            kk = _bf16(jnp.concatenate([strided(k_ref, prev_c, dilation),
                                        strided(k_ref, cur, dilation)], axis=0))
            vv = _bf16(jnp.concatenate([strided(v_ref, prev_c, dilation),
                                        strided(v_ref, cur, dilation)], axis=0))
            zero = jnp.zeros_like(qb)
            outs, lses = [], []
            for h in range(2):
                qm = jnp.where(first_head, qb, zero) if h == 0 else jnp.where(first_head, zero, qb)
                z = _dot_nt(qm, kk)
                lg = z + bias_ref[2 * (gidx * N_HEADS_DIL + 2 * p + h) + no_prev]
                m = jnp.max(lg, axis=-1, keepdims=True)
                pe = jnp.exp(lg - m)
                den = jnp.sum(pe, axis=-1, keepdims=True)
                outs.append(_dot(_bf16(pe), vv) / den)
                lses.append(m + jnp.log(den))
            o_tile = jnp.where(first_head, outs[0], outs[1])
            l_tile = jnp.where(first_head, lses[0], lses[1])
            if dilation == 1:
                idx = pl.ds(sq, BLOCK)
            else:
                idx = pl.ds(sq, BLOCK, stride=dilation)
            obr_ref.at[gidx][idx, :] = o_tile
            lse_ref.at[gidx][idx, :] = l_tile
            return 0

        lax.fori_loop(0, DIL_UNITS, unit, 0, unroll=8)

    chunk = 256

    def combine(c, _):
        rows = pl.ds(pl.multiple_of(c * chunk, chunk), chunk)
        ls = [lse_ref[gidx, rows, :] for gidx in range(n_br)]
        m = functools.reduce(jnp.maximum, ls)
        es = [jnp.exp(l - m) for l in ls]
        num = sum(e * obr_ref[gidx, rows, :] for gidx, e in enumerate(es))
        o = num / sum(es)
        o_ref[rows, :] = _head_rms_gain(o, g_ref[...]).astype(o_ref.dtype)
        return 0

    lax.fori_loop(0, DIL_SUPER // chunk, combine, 0)


def _dil_attention(proj, g, tile_bias, batch, seq):
    n = batch * seq
    ns = seq // DIL_SUPER
    n_br = len(DIL_PATTERNS)
    return pl.pallas_call(
        _dil_kernel,
        grid=(batch, N_PAIRS_DIL, ns),
        in_specs=[
            pl.BlockSpec((n_br * N_HEADS_DIL, BLOCK, 2 * BLOCK), lambda b, p, i: (0, 0, 0)),
            pl.BlockSpec((None, DIL_SUPER, LANES), lambda b, p, i: (p, b * ns + i, 0)),
            pl.BlockSpec((None, seq, LANES), lambda b, p, i: (N_PAIRS_DIL + p, b, 0)),
            pl.BlockSpec((None, seq, LANES), lambda b, p, i: (2 * N_PAIRS_DIL + p, b, 0)),
            pl.BlockSpec((None, 1, LANES), lambda b, p, i: (p, 0, 0)),
        ],
        out_specs=pl.BlockSpec((None, DIL_SUPER, LANES), lambda b, p, i: (p, b * ns + i, 0)),
        out_shape=jax.ShapeDtypeStruct((N_PAIRS_DIL, n, LANES), jnp.bfloat16),
        scratch_shapes=[pltpu.VMEM((2 * n_br * N_HEADS_DIL, BLOCK, 2 * BLOCK), jnp.float32),
                        pltpu.VMEM((n_br, DIL_SUPER, LANES), jnp.float32),
                        pltpu.VMEM((n_br, DIL_SUPER, LANES), jnp.float32)],
        compiler_params=pltpu.CompilerParams(
            dimension_semantics=("arbitrary", "arbitrary", "arbitrary"),
            vmem_limit_bytes=VMEM_LIMIT),
        name="dilated_attention",
    )(tile_bias, proj, proj, proj, g)


def _router(x1, wrt_hi, wrt_lo, br):
    tm = x1.shape[0]
    xh, xl = _split_bf16(x1)
    lt = _dot_nt(wrt_hi, xh) + _dot_nt(wrt_hi, xl) + _dot_nt(wrt_lo, xh) + br
    rows = [lt[e:e + 1, :] for e in range(N_EXPERTS)]
    m = functools.reduce(jnp.maximum, rows)
    ex = [jnp.exp(r - m) for r in rows]
    tot = functools.reduce(lambda a, c: a + c, ex)
    probs = [e / tot for e in ex]

    keep, score = [], []
    for gi in range(N_GROUPS):
        members = range(gi * EXPERTS_PER_GROUP, (gi + 1) * EXPERTS_PER_GROUP)
        s = None
        for e in members:
            rank = None
            for o in members:
                if o == e:
                    continue
                beats = (probs[o] > probs[e]) | ((probs[o] == probs[e]) & (o < e))
                beats = beats.astype(jnp.int32)
                rank = beats if rank is None else rank + beats
            k = rank < 2
            keep.append(k)
            contrib = jnp.where(k, probs[e], 0.0)
            s = contrib if s is None else s + contrib
        score.append(s)

    sel, gates = [], []
    for gi in range(N_GROUPS):
        chosen = None
        for o in range(N_GROUPS):
            if o == gi:
                continue
            c = (score[gi] > score[o]) if o < gi else (score[gi] >= score[o])
            chosen = c if chosen is None else (chosen & c)
        for e in range(gi * EXPERTS_PER_GROUP, (gi + 1) * EXPERTS_PER_GROUP):
            s = keep[e] & chosen
            sel.append(s)
            gates.append(jnp.where(s, probs[e] / score[gi], 0.0))

    idx0 = functools.reduce(jnp.minimum, [jnp.where(sel[e], e, N_EXPERTS) for e in range(N_EXPERTS)])
    idx1 = functools.reduce(jnp.maximum, [jnp.where(sel[e], e, -1) for e in range(N_EXPERTS)])
    g0 = sum(jnp.where(idx0 == e, gates[e], 0.0) for e in range(N_EXPERTS))
    g1 = sum(jnp.where(idx1 == e, gates[e], 0.0) for e in range(N_EXPERTS))

    row_id = lax.broadcasted_iota(jnp.int32, (N_EXPERTS, tm), 0)
    selmat = jnp.zeros((N_EXPERTS, tm), jnp.float32)
    for e in range(N_EXPERTS):
        selmat = jnp.where((row_id == e) & sel[e], 1.0, selmat)
    return idx0, idx1, g0, g1, selmat


def _to_token_tiles(ref, x):
    rows = x.shape[0]
    for s in range(TOKEN_TILE_ROWS):
        ref[pl.ds(s, rows, stride=TOKEN_TILE_ROWS), :] = x[:, s * LANES:(s + 1) * LANES]


def _from_token_tiles(ref, rows):
    return jnp.concatenate([ref[pl.ds(s, rows, stride=TOKEN_TILE_ROWS), :]
                            for s in range(TOKEN_TILE_ROWS)], axis=1)


def _out_proj_kernel(x_ref, ysb_ref, ydl_ref, w_ref, lng_ref, lnb_ref,
                     wrh_ref, wrl_ref, br_ref,
                     x1_ref, x1t_ref, meta_ref, gcol_ref, cnt_ref, carry_ref, *, alpha):
    i = pl.program_id(0)
    tm = x_ref.shape[0]

    @pl.when(i == 0)
    def _():
        carry_ref[...] = jnp.zeros_like(carry_ref)

    y = jnp.concatenate([ysb_ref[j] for j in range(N_PAIRS_SB)]
                        + [ydl_ref[j] for j in range(N_PAIRS_DIL)], axis=1)
    h = _dot(y, w_ref[...])
    x1 = _layer_norm(alpha * x_ref[...] + h, lng_ref[...], lnb_ref[...])
    x1_ref[...] = x1
    _to_token_tiles(x1t_ref, x1)

    idx0, idx1, g0, g1, selmat = _router(x1, wrh_ref[...], wrl_ref[...], br_ref[...])

    jj = lax.broadcasted_iota(jnp.int32, (tm, tm + LANES), 0)
    ss = lax.broadcasted_iota(jnp.int32, (tm, tm + LANES), 1)
    before = jnp.where((jj < ss) | (ss >= tm), 1.0, 0.0).astype(jnp.bfloat16)
    pc = _dot(_bf16(selmat), before)
    carry = carry_ref[...]
    rankmat = pc[:, :tm] + jnp.concatenate([carry] * (tm // LANES), axis=1)
    carry = carry + pc[:, tm:]
    carry_ref[...] = carry
    cnt_ref[...] = carry

    rank0 = sum(jnp.where(idx0 == e, rankmat[e:e + 1, :], 0.0) for e in range(N_EXPERTS))
    rank1 = sum(jnp.where(idx1 == e, rankmat[e:e + 1, :], 0.0) for e in range(N_EXPERTS))

    row8 = lax.broadcasted_iota(jnp.int32, (8, tm), 0)
    meta = jnp.zeros((8, tm), jnp.int32)
    for r, v in enumerate((idx0, idx1, rank0.astype(jnp.int32), rank1.astype(jnp.int32))):
        meta = jnp.where(row8 == r, v, meta)
    meta_ref[...] = meta

    rowl = lax.broadcasted_iota(jnp.int32, (LANES, tm), 0)
    gt = jnp.where(rowl == 0, g0, jnp.where(rowl == 1, g1, 0.0))
    gcol_ref[...] = gt.T


def _out_proj(x2d, y_sb, y_dl, w_out, ln_g, ln_b, wrt_hi, wrt_lo, br, alpha):
    n, d = x2d.shape
    const = lambda i: (0, 0)
    return pl.pallas_call(
        functools.partial(_out_proj_kernel, alpha=alpha),
        grid=(n // ROW_TILE,),
        in_specs=[
            pl.BlockSpec((ROW_TILE, d), lambda i: (i, 0)),
            pl.BlockSpec((N_PAIRS_SB, ROW_TILE, LANES), lambda i: (0, i, 0)),
            pl.BlockSpec((N_PAIRS_DIL, ROW_TILE, LANES), lambda i: (0, i, 0)),
            pl.BlockSpec((d, d), const),
            pl.BlockSpec((1, d), const),
            pl.BlockSpec((1, d), const),
            pl.BlockSpec((N_EXPERTS, d), const),
            pl.BlockSpec((N_EXPERTS, d), const),
            pl.BlockSpec((N_EXPERTS, 1), const),
        ],
        out_specs=[pl.BlockSpec((ROW_TILE, d), lambda i: (i, 0)),
                   pl.BlockSpec((ROW_TILE * TOKEN_TILE_ROWS, LANES), lambda i: (i, 0)),
                   pl.BlockSpec((8, ROW_TILE), lambda i: (0, i)),
                   pl.BlockSpec((ROW_TILE, LANES), lambda i: (i, 0)),
                   pl.BlockSpec((N_EXPERTS, LANES), const)],
        out_shape=[jax.ShapeDtypeStruct((n, d), jnp.float32),
                   jax.ShapeDtypeStruct((n * TOKEN_TILE_ROWS, LANES), jnp.float32),
                   jax.ShapeDtypeStruct((8, n), jnp.int32),
                   jax.ShapeDtypeStruct((n, LANES), jnp.float32),
                   jax.ShapeDtypeStruct((N_EXPERTS, LANES), jnp.float32)],
        scratch_shapes=[pltpu.VMEM((N_EXPERTS, LANES), jnp.float32)],
        compiler_params=pltpu.CompilerParams(dimension_semantics=("arbitrary",),
                                             vmem_limit_bytes=VMEM_LIMIT),
        name="out_proj_ln_router",
    )(x2d, y_sb, y_dl, w_out, ln_g, ln_b, wrt_hi, wrt_lo, br)


def _segment_offsets(cnt_ref, off_ref):
    acc = jnp.int32(0)
    padded = []
    for e in range(N_EXPERTS):
        off_ref[e] = acc
        pc = lax.div(cnt_ref[e] + (MOE_TM - 1), MOE_TM) * MOE_TM
        padded.append(pc)
        acc = acc + pc
    return padded, acc


def _row_copy(src_ref, src_row, dst_ref, dst_row, sem):
    def tile(row):
        if isinstance(row, int):
            return pl.ds(row * TOKEN_TILE_ROWS, TOKEN_TILE_ROWS)
        return pl.ds(pl.multiple_of(row * TOKEN_TILE_ROWS, TOKEN_TILE_ROWS), TOKEN_TILE_ROWS)
    return pltpu.make_async_copy(src_ref.at[tile(src_row), :], dst_ref.at[tile(dst_row), :], sem)


def _plan_kernel(cnt_ref, e0_ref, e1_ref, r0_ref, r1_ref, tok_ref, te_ref, nv_ref, off_ref,
                 *, max_tiles, n_tokens):
    i = pl.program_id(0)
    tp = e0_ref.shape[0]
    padded, total = _segment_offsets(cnt_ref, off_ref)

    @pl.when(i == 0)
    def _():
        def clear(k, c):
            tok_ref[k] = 0
            return c
        lax.fori_loop(0, max_tiles * MOE_TM, clear, 0, unroll=8)

        n_tiles = lax.div(total, MOE_TM)
        for e in range(N_EXPERTS):
            first = lax.div(off_ref[e], MOE_TM)
            count = lax.div(padded[e], MOE_TM)

            def fill(k, c, first=first, e=e):
                te_ref[first + k] = e
                nv_ref[first + k] = jnp.minimum(cnt_ref[e] - k * MOE_TM, MOE_TM)
                return c
            lax.fori_loop(0, count, fill, 0)
        last = te_ref[n_tiles - 1]

        def fill_tail(k, c):
            te_ref[k] = last
            nv_ref[k] = 0
            return c
        lax.fori_loop(n_tiles, max_tiles, fill_tail, 0)
        te_ref[max_tiles] = n_tiles

    def place(j, c):
        token = i * tp + j
        tok_ref[off_ref[e0_ref[j]] + r0_ref[j]] = token
        tok_ref[off_ref[e1_ref[j]] + r1_ref[j]] = token + n_tokens
        return c
    lax.fori_loop(0, tp, place, 0, unroll=8)


def _plan(counts, e0, e1, r0, r1, max_tiles):
    n = e0.shape[0]
    smem = pl.BlockSpec(memory_space=pltpu.SMEM)
    vec = pl.BlockSpec((PERM_TP,), lambda i: (i,), memory_space=pltpu.SMEM)
    return pl.pallas_call(
        functools.partial(_plan_kernel, max_tiles=max_tiles, n_tokens=n),
        grid=(n // PERM_TP,),
        in_specs=[smem, vec, vec, vec, vec],
        out_specs=[smem, smem, smem],
        out_shape=[jax.ShapeDtypeStruct((max_tiles * MOE_TM,), jnp.int32),
                   jax.ShapeDtypeStruct((max_tiles + 1,), jnp.int32),
                   jax.ShapeDtypeStruct((max_tiles,), jnp.int32)],
        scratch_shapes=[pltpu.SMEM((N_EXPERTS,), jnp.int32)],
        compiler_params=pltpu.CompilerParams(dimension_semantics=("arbitrary",)),
        name="moe_plan",
    )(counts, e0, e1, r0, r1)


def _experts_kernel(tok_ref, te_ref, nv_ref, layer_ref, x_ref, wg_ref, wu_ref, wd_ref, y_ref,
                    xbuf_ref, ybuf_ref, wgb_ref, wub_ref, wdb_ref, gsem, ssem, *, max_tiles):
    i = pl.program_id(0)
    n_tiles = te_ref[max_tiles]
    slot = i & 1
    tile_rows = MOE_TM * TOKEN_TILE_ROWS
    n_tokens = x_ref.shape[0] // TOKEN_TILE_ROWS

    def gather(tile, s):
        base = tile * MOE_TM

        def body(j, c):
            code = tok_ref[base + j]
            token = jnp.where(code >= n_tokens, code - n_tokens, code)
            _row_copy(x_ref, token, xbuf_ref.at[s], j, gsem.at[s]).start()
            return c
        lax.fori_loop(0, MOE_TM, body, 0, unroll=8)

    def scatter_copy(tile, s, j):
        return _row_copy(ybuf_ref.at[s], j, y_ref, tok_ref[tile * MOE_TM + j], ssem.at[s])

    def issue_scatter(tile, s):
        def body(j, c):
            scatter_copy(tile, s, j).start()
            return c
        full = nv_ref[tile] == MOE_TM

        @pl.when(full)
        def _():
            lax.fori_loop(0, MOE_TM, body, 0, unroll=8)

        @pl.when(jnp.logical_not(full))
        def _():
            lax.fori_loop(0, nv_ref[tile], body, 0)

    def wait_scatter(tile, s):
        def body(j, c):
            scatter_copy(tile, s, j).wait()
            return c
        full = nv_ref[tile] == MOE_TM

        @pl.when(full)
        def _():
            pltpu.make_async_copy(ybuf_ref.at[s], y_ref.at[pl.ds(0, tile_rows), :],
                                  ssem.at[s]).wait()

        @pl.when(jnp.logical_not(full))
        def _():
            lax.fori_loop(0, nv_ref[tile], body, 0)

    @pl.when(i == 0)
    def _():
        gather(0, 0)

    @pl.when(i + 1 < n_tiles)
    def _():
        gather(i + 1, 1 - slot)

    @pl.when(i < n_tiles)
    def _():
        pltpu.make_async_copy(x_ref.at[pl.ds(0, tile_rows), :], xbuf_ref.at[slot],
                              gsem.at[slot]).wait()

        @pl.when(i >= 2)
        def _():
            wait_scatter(i - 2, slot)

        @pl.when((i == 0) | (te_ref[i] != te_ref[jnp.maximum(i - 1, 0)]))
        def _():
            wgb_ref[...] = _bf16(wg_ref[...])
            wub_ref[...] = _bf16(wu_ref[...])
            wdb_ref[...] = _bf16(wd_ref[...])

        xb = _bf16(_from_token_tiles(xbuf_ref.at[slot], MOE_TM))
        gt = _dot(xb, wgb_ref[...])
        up = _dot(xb, wub_ref[...])
        h = gt / (1.0 + jnp.exp(-gt)) * up
        _to_token_tiles(ybuf_ref.at[slot], _dot(_bf16(h), wdb_ref[...]))

        issue_scatter(i, slot)

        @pl.when(i == n_tiles - 1)
        def _():
            @pl.when(i >= 1)
            def _():
                wait_scatter(i - 1, 1 - slot)
            wait_scatter(i, slot)


def _experts(tok, te, nv, x1t, w_gate, w_up, w_down, layer, max_tiles):
    d, f = w_gate.shape[-2:]
    rows = MOE_TM * TOKEN_TILE_ROWS
    any_spec = pl.BlockSpec(memory_space=pl.ANY)
    expert = lambda i, tok, te, nv, lay: (lay[0], te[i], 0, 0)
    return pl.pallas_call(
        functools.partial(_experts_kernel, max_tiles=max_tiles),
        grid_spec=pltpu.PrefetchScalarGridSpec(
            num_scalar_prefetch=4,
            grid=(max_tiles,),
            in_specs=[any_spec,
                      pl.BlockSpec((None, None, d, f), expert),
                      pl.BlockSpec((None, None, d, f), expert),
                      pl.BlockSpec((None, None, f, d), expert)],
            out_specs=any_spec,
            scratch_shapes=[pltpu.VMEM((2, rows, LANES), jnp.float32),
                            pltpu.VMEM((2, rows, LANES), jnp.float32),
                            pltpu.VMEM((d, f), jnp.bfloat16),
                            pltpu.VMEM((d, f), jnp.bfloat16),
                            pltpu.VMEM((f, d), jnp.bfloat16),
                            pltpu.SemaphoreType.DMA((2,)),
                            pltpu.SemaphoreType.DMA((2,))]),
        out_shape=jax.ShapeDtypeStruct((2 * x1t.shape[0], LANES), jnp.float32),
        compiler_params=pltpu.CompilerParams(dimension_semantics=("arbitrary",),
                                             vmem_limit_bytes=VMEM_LIMIT),
        name="moe_experts",
    )(tok, te, nv, jnp.full((1,), layer, jnp.int32), x1t, w_gate, w_up, w_down)


def _combine_kernel(x_ref, y0_ref, y1_ref, gcol_ref, lng_ref, lnb_ref, o_ref, *, alpha):
    tm = x_ref.shape[0]
    lane = lax.broadcasted_iota(jnp.int32, (1, LANES), 1)
    gc = gcol_ref[...]
    g0 = jnp.sum(jnp.where(lane == 0, gc, 0.0), axis=-1, keepdims=True)
    g1 = jnp.sum(jnp.where(lane == 1, gc, 0.0), axis=-1, keepdims=True)
    y = g0 * _from_token_tiles(y0_ref, tm) + g1 * _from_token_tiles(y1_ref, tm)
    o_ref[...] = _layer_norm(alpha * x_ref[...] + y, lng_ref[...], lnb_ref[...])


def _combine(x1, y, gcol, ln_g, ln_b, alpha):
    n, d = x1.shape
    const = lambda i: (0, 0)
    rows = ROW_TILE * TOKEN_TILE_ROWS
    return pl.pallas_call(
        functools.partial(_combine_kernel, alpha=alpha),
        grid=(n // ROW_TILE,),
        in_specs=[pl.BlockSpec((ROW_TILE, d), lambda i: (i, 0)),
                  pl.BlockSpec((rows, LANES), lambda i: (i, 0)),
                  pl.BlockSpec((rows, LANES), lambda i: (i + n // ROW_TILE, 0)),
                  pl.BlockSpec((ROW_TILE, LANES), lambda i: (i, 0)),
                  pl.BlockSpec((1, d), const), pl.BlockSpec((1, d), const)],
        out_specs=pl.BlockSpec((ROW_TILE, d), lambda i: (i, 0)),
        out_shape=jax.ShapeDtypeStruct((n, d), jnp.float32),
        compiler_params=pltpu.CompilerParams(dimension_semantics=("arbitrary",),
                                             vmem_limit_bytes=VMEM_LIMIT),
        name="moe_combine_ln",
    )(x1, y, y, gcol, ln_g, ln_b)


def _moe(x1, x1t, meta, gcol, cnt, w_gate, w_up, w_down, layer, ln_g, ln_b, alpha):
    n = x1.shape[0]
    max_tiles = (2 * n + N_EXPERTS * (MOE_TM - 1)) // MOE_TM
    counts = cnt[:, 0].astype(jnp.int32)
    e0, e1, r0, r1 = meta[0], meta[1], meta[2], meta[3]
    tok, te, nv = _plan(counts, e0, e1, r0, r1, max_tiles)
    y = _experts(tok, te, nv, x1t, w_gate, w_up, w_down, layer, max_tiles)
    return _combine(x1, y, gcol, ln_g, ln_b, alpha)


def _t5_bucket(dist):
    max_exact = N_BUCKETS // 2
    d = jnp.maximum(dist, 0)
    large = max_exact + (jnp.log(jnp.maximum(d, 1).astype(jnp.float32) / max_exact)
                         / math.log(MAX_DISTANCE / max_exact) * (N_BUCKETS - max_exact)).astype(jnp.int32)
    large = jnp.minimum(large, N_BUCKETS - 1)
    return jnp.where(d < max_exact, d, large)


def _tile_bias(rel_bias):
    qi = jnp.arange(BLOCK)[:, None]
    kj = jnp.arange(2 * BLOCK)[None, :]
    steps = qi + BLOCK - kj
    tiles = jnp.stack([rel_bias[_t5_bucket(jnp.maximum(steps, 0) * dilation)]
                       for _, dilation in DIL_PATTERNS])
    return tiles.astype(jnp.float32).transpose(0, 3, 1, 2).reshape(-1, BLOCK, 2 * BLOCK)


def kernel(x, w_in, g_sb, g_dil, w_out, ln1_g, ln1_b, ln2_g, ln2_b, rel_bias,
           w_router, b_router, w_gate, w_up, w_down):
    batch, seq, d = x.shape
    depth = w_in.shape[0]
    alpha = (2.0 * depth) ** 0.25
    n = batch * seq
    x2d = x.reshape(n, d)

    tile_bias = _tile_bias(rel_bias)
    wrt =w_router.T.astype(jnp.float32)
    wrt_hi = _bf16(wrt)
    wrt_lo = _bf16(wrt - wrt_hi.astype(jnp.float32))
    br = b_router.astype(jnp.float32).reshape(N_EXPERTS, 1)
    sb_cols = 3 * D_SB

    for l in range(depth):
        w_in_l = _bf16(w_in[l])
        proj_sb = _in_proj(x2d, w_in_l[:, :sb_cols], jnp.bfloat16, N_PAIRS_SB, "in_proj_sb")
        proj_dl = _in_proj(x2d, w_in_l[:, sb_cols:], jnp.float32, N_PAIRS_DIL, "in_proj_dil")
        y_sb = _sb_attention(proj_sb, g_sb[l].reshape(N_PAIRS_SB, 1, LANES), batch, seq)
        y_dl = _dil_attention(proj_dl, g_dil[l].reshape(N_PAIRS_DIL, 1, LANES),
                              tile_bias, batch, seq)
        x1, x1t, meta, gcol, cnt = _out_proj(x2d, y_sb, y_dl, _bf16(w_out[l]),
                                             ln1_g[l].reshape(1, d), ln1_b[l].reshape(1, d),
                                             wrt_hi, wrt_lo, br, alpha)
        x2d = _moe(x1, x1t, meta, gcol, cnt, w_gate, w_up, w_down, l,
                   ln2_g[l].reshape(1, d), ln2_b[l].reshape(1, d), alpha)
    return x2d.reshape(batch, seq, d)
```

```python
import functools
import math

import jax
import jax.numpy as jnp
from jax import lax
from jax.experimental import pallas as pl
from jax.experimental.pallas import tpu as pltpu

D_MODEL = 1024
HEAD_DIM = 64
N_HEADS_SB = 8
N_HEADS_DIL = 8
D_SB = N_HEADS_SB * HEAD_DIM
D_DIL = N_HEADS_DIL * HEAD_DIM
DIL_PATTERNS = ((128, 1), (512, 4), (2048, 16))
BLOCK = 128
N_BUCKETS = 32
MAX_DISTANCE = 2048
N_EXPERTS = 16
N_GROUPS = 4
EXPERTS_PER_GROUP = N_EXPERTS // N_GROUPS
D_FF_EXPERT = 1024
LN_EPS = 1e-5
NEG_INF = -1e30

LANES = 128
N_PAIRS_SB = D_SB // LANES
N_PAIRS_DIL = D_DIL // LANES
VMEM_LIMIT = 56 * 1024 * 1024

ROW_TILE = 512
SB_TQ = 256
SB_TK = 256
SB_SKIP_BOUND = -110.0
DIL_SUPER = 2048
DIL_UNITS = DIL_SUPER // BLOCK
TOKEN_TILE_ROWS = D_MODEL // LANES
MOE_TM = 256
MOE_TM_LOG2 = MOE_TM.bit_length() - 1
PERM_TP = 512

_NT = (((1,), (1,)), ((), ()))


def _bf16(x):
    return x.astype(jnp.bfloat16)


def _split_bf16(x):
    hi = _bf16(x)
    lo = _bf16(x - hi.astype(jnp.float32))
    return hi, lo


def _dot(a, b):
    return jnp.dot(a, b, preferred_element_type=jnp.float32)


def _dot_nt(a, b):
    return lax.dot_general(a, b, _NT, preferred_element_type=jnp.float32)


def _head_rms_gain(o, g):
    r = lax.broadcasted_iota(jnp.int32, (LANES, LANES), 0) // HEAD_DIM
    c = lax.broadcasted_iota(jnp.int32, (LANES, LANES), 1) // HEAD_DIM
    same_head = jnp.where(r == c, 1.0 / HEAD_DIM, 0.0).astype(jnp.bfloat16)
    hi, lo = _split_bf16(o * o)
    ms = _dot(hi, same_head) + _dot(lo, same_head)
    return o * lax.rsqrt(ms + 1e-6) * g


def _layer_norm(x, g, b):
    mu = jnp.mean(x, axis=-1, keepdims=True)
    xc = x - mu
    var = jnp.mean(xc * xc, axis=-1, keepdims=True)
    return xc * lax.rsqrt(var + LN_EPS) * g + b


def _in_proj_kernel(x_ref, w_ref, o_ref, *, n_scaled, scale):
    x = _bf16(x_ref[...])
    n_slabs = o_ref.shape[0]
    for j in range(0, n_slabs, 2):
        res = _dot(x, w_ref[:, j * LANES:(j + 2) * LANES])
        for jj in range(2):
            blk = res[:, jj * LANES:(jj + 1) * LANES]
            if j + jj < n_scaled:
                blk = blk * scale
            o_ref[j + jj] = blk.astype(o_ref.dtype)


def _in_proj(x2d, w, out_dtype, n_scaled, name):
    n, d = x2d.shape
    n_slabs = w.shape[1] // LANES
    return pl.pallas_call(
        functools.partial(_in_proj_kernel, n_scaled=n_scaled, scale=HEAD_DIM ** -0.5),
        grid=(n // ROW_TILE,),
        in_specs=[pl.BlockSpec((ROW_TILE, d), lambda i: (i, 0)),
                  pl.BlockSpec((d, w.shape[1]), lambda i: (0, 0))],
        out_specs=pl.BlockSpec((n_slabs, ROW_TILE, LANES), lambda i: (0, i, 0)),
        out_shape=jax.ShapeDtypeStruct((n_slabs, n, LANES), out_dtype),
        compiler_params=pltpu.CompilerParams(dimension_semantics=("arbitrary",),
                                             vmem_limit_bytes=VMEM_LIMIT),
        name=name,
    )(x2d, w)


def _sb_kernel(q_ref, k_ref, v_ref, g_ref, o_ref, carry_ref, acc_ref):
    i = pl.program_id(2)
    lane = lax.broadcasted_iota(jnp.int32, (1, LANES), 1)
    first_head = lane < HEAD_DIM
    q = q_ref[...]
    zero = jnp.zeros_like(q)
    q_heads = (jnp.where(first_head, q, zero), jnp.where(first_head, zero, q))

    jj = lax.broadcasted_iota(jnp.int32, (SB_TK, SB_TK + LANES), 0)
    ss = lax.broadcasted_iota(jnp.int32, (SB_TK, SB_TK + LANES), 1)
    suffix = jnp.where((jj > ss) | (ss >= SB_TK), 1.0, 0.0).astype(jnp.bfloat16)

    tq_i = lax.broadcasted_iota(jnp.int32, (SB_TQ, SB_TK), 0)
    ts_i = lax.broadcasted_iota(jnp.int32, (SB_TQ, SB_TK), 1)
    causal = ts_i < tq_i

    def block(kb, diag, carry, acc, weight=None):
        off = pl.multiple_of(kb * SB_TK, SB_TK)
        kblk = k_ref[pl.ds(off, SB_TK), :]
        vblk = v_ref[pl.ds(off, SB_TK), :]
        new_carry, new_acc = [], []
        for h in range(2):
            z = _dot_nt(q_heads[h], kblk)
            sp = jnp.log(1.0 + jnp.exp(-jnp.abs(z)))
            log_beta = jnp.minimum(z, 0.0) - sp
            log_1m = -jnp.maximum(z, 0.0) - sp
            if diag:
                log_1m = jnp.where(causal, log_1m, 0.0)
            sfx = _dot(_bf16(log_1m), suffix)
            c = carry[h]
            between = sfx[:, :SB_TK] + jnp.concatenate([c] * (SB_TK // LANES), axis=1)
            a = jnp.exp(log_beta + between)
            if diag:
                a = jnp.where(causal, a, 0.0)
            out = _dot(_bf16(a), vblk)
            tot = sfx[:, SB_TK:]
            if weight is not None:
                out = out * weight
                tot = tot * weight
            new_acc.append(acc[h] + out)
            new_carry.append(c + tot)
        return new_carry, new_acc

    zeros = jnp.zeros((SB_TQ, LANES), jnp.float32)
    carry, acc = block(i, True, [zeros, zeros], [zeros, zeros])
    has_prev = jnp.where(i >= 1, 1.0, 0.0)
    carry, acc = block(jnp.maximum(i - 1, 0), False, carry, acc, weight=has_prev)
    for h in range(2):
        carry_ref[h] = carry[h]
        acc_ref[h] = acc[h]

    def cond(state):
        kb, mx = state
        return jnp.logical_and(kb >= 0, mx > SB_SKIP_BOUND)

    def body(state):
        kb, _ = state
        c, a = block(kb, False, [carry_ref[0], carry_ref[1]], [acc_ref[0], acc_ref[1]])
        for h in range(2):
            carry_ref[h] = c[h]
            acc_ref[h] = a[h]
        return kb - 1, jnp.maximum(jnp.max(c[0]), jnp.max(c[1]))

    lax.while_loop(cond, body, (i - 2, jnp.maximum(jnp.max(carry[0]), jnp.max(carry[1]))))

    o = jnp.where(first_head, acc_ref[0], acc_ref[1])
    o_ref[...] = _head_rms_gain(o, g_ref[...]).astype(o_ref.dtype)


def _sb_attention(proj, g, batch, seq):
    n = batch * seq
    nq = seq // SB_TQ
    return pl.pallas_call(
        _sb_kernel,
        grid=(batch, N_PAIRS_SB, nq),
        in_specs=[
            pl.BlockSpec((None, SB_TQ, LANES), lambda b, p, i: (p, b * nq + i, 0)),
            pl.BlockSpec((None, seq, LANES), lambda b, p, i: (N_PAIRS_SB + p, b, 0)),
            pl.BlockSpec((None, seq, LANES), lambda b, p, i: (2 * N_PAIRS_SB + p, b, 0)),
            pl.BlockSpec((None, 1, LANES), lambda b, p, i: (p, 0, 0)),
        ],
        out_specs=pl.BlockSpec((None, SB_TQ, LANES), lambda b, p, i: (p, b * nq + i, 0)),
        out_shape=jax.ShapeDtypeStruct((N_PAIRS_SB, n, LANES), jnp.bfloat16),
        scratch_shapes=[pltpu.VMEM((2, SB_TQ, LANES), jnp.float32),
                        pltpu.VMEM((2, SB_TQ, LANES), jnp.float32)],
        compiler_params=pltpu.CompilerParams(
            dimension_semantics=("arbitrary", "arbitrary", "arbitrary"),
            vmem_limit_bytes=VMEM_LIMIT),
        name="sb_attention",
    )(proj, proj, proj, g)


def _dil_kernel(tbias_ref, q_ref, k_ref, v_ref, g_ref, o_ref,
                bias_ref, obr_ref, lse_ref):
    b = pl.program_id(0)
    p = pl.program_id(1)
    i = pl.program_id(2)
    n_br = len(DIL_PATTERNS)

    qi = lax.broadcasted_iota(jnp.int32, (BLOCK, 2 * BLOCK), 0)
    kj = lax.broadcasted_iota(jnp.int32, (BLOCK, 2 * BLOCK), 1)
    steps = qi + BLOCK - kj

    assert all(window // dilation == BLOCK for window, dilation in DIL_PATTERNS)
    valid = (steps >= 0) & (steps <= BLOCK)

    @pl.when((b == 0) & (p == 0) & (i == 0))
    def _():
        def per_tile(s, _):
            tile = tbias_ref[s]
            bias_ref[2 * s] = jnp.where(valid, tile, NEG_INF)
            bias_ref[2 * s + 1] = jnp.where(valid & (kj >= BLOCK), tile, NEG_INF)
            return 0
        lax.fori_loop(0, n_br * N_HEADS_DIL, per_tile, 0)

    lane = lax.broadcasted_iota(jnp.int32, (1, LANES), 1)
    first_head = lane < HEAD_DIM
    t0 = i * DIL_SUPER

    def strided(ref, start, dilation):
        if dilation == 1:
            return ref[pl.ds(start, BLOCK), :]
        return ref[pl.ds(start, BLOCK, stride=dilation), :]

    for gidx, (window, dilation) in enumerate(DIL_PATTERNS):
        units_per_residue = DIL_UNITS // dilation
        shift = units_per_residue.bit_length() - 1

        def unit(u, _, gidx=gidx, dilation=dilation, units_per_residue=units_per_residue, shift=shift):
            r = lax.shift_right_logical(u, shift)
            cb = u & (units_per_residue - 1)
            sq = r + cb * (dilation * BLOCK)
            cur = t0 + sq
            prev = cur - dilation * BLOCK
            prev_ok = prev >= 0
            prev_c = jnp.where(prev_ok, prev, cur)
            no_prev = jnp.where(prev_ok, 0, 1)

            qb = _bf16(strided(q_ref, sq, dilation))
            kk = _bf16(jnp.concatenate([strided(k_ref, prev_c, dilation),
                                        strided(k_ref, cur, dilation)], axis=0))
            vv = _bf16(jnp.concatenate([strided(v_ref, prev_c, dilation),
                                        strided(v_ref, cur, dilation)], axis=0))
            zero = jnp.zeros_like(qb)
            outs, lses = [], []
            for h in range(2):
                qm = jnp.where(first_head, qb, zero) if h == 0 else jnp.where(first_head, zero, qb)
                z = _dot_nt(qm, kk)
                lg = z + bias_ref[2 * (gidx * N_HEADS_DIL + 2 * p + h) + no_prev]
                m = jnp.max(lg, axis=-1, keepdims=True)
                pe = jnp.exp(lg - m)
                den = jnp.sum(pe, axis=-1, keepdims=True)
                outs.append(_dot(_bf16(pe), vv) / den)
                lses.append(m + jnp.log(den))
            o_tile = jnp.where(first_head, outs[0], outs[1])
            l_tile = jnp.where(first_head, lses[0], lses[1])
            if dilation == 1:
                idx = pl.ds(sq, BLOCK)
            else:
                idx = pl.ds(sq, BLOCK, stride=dilation)
            obr_ref.at[gidx][idx, :] = o_tile
            lse_ref.at[gidx][idx, :] = l_tile
            return 0

        lax.fori_loop(0, DIL_UNITS, unit, 0, unroll=8)

    chunk = 256

    def combine(c, _):
        rows = pl.ds(pl.multiple_of(c * chunk, chunk), chunk)
        ls = [lse_ref[gidx, rows, :] for gidx in range(n_br)]
        m = functools.reduce(jnp.maximum, ls)
        es = [jnp.exp(l - m) for l in ls]
        num = sum(e * obr_ref[gidx, rows, :] for gidx, e in enumerate(es))
        o = num / sum(es)
        o_ref[rows, :] = _head_rms_gain(o, g_ref[...]).astype(o_ref.dtype)
        return 0

    lax.fori_loop(0, DIL_SUPER // chunk, combine, 0)


def _dil_attention(proj, g, tile_bias, batch, seq):
    n = batch * seq
    ns = seq // DIL_SUPER
    n_br = len(DIL_PATTERNS)
    return pl.pallas_call(
        _dil_kernel,
        grid=(batch, N_PAIRS_DIL, ns),
        in_specs=[
            pl.BlockSpec((n_br * N_HEADS_DIL, BLOCK, 2 * BLOCK), lambda b, p, i: (0, 0, 0)),
            pl.BlockSpec((None, DIL_SUPER, LANES), lambda b, p, i: (p, b * ns + i, 0)),
            pl.BlockSpec((None, seq, LANES), lambda b, p, i: (N_PAIRS_DIL + p, b, 0)),
            pl.BlockSpec((None, seq, LANES), lambda b, p, i: (2 * N_PAIRS_DIL + p, b, 0)),
            pl.BlockSpec((None, 1, LANES), lambda b, p, i: (p, 0, 0)),
        ],
        out_specs=pl.BlockSpec((None, DIL_SUPER, LANES), lambda b, p, i: (p, b * ns + i, 0)),
        out_shape=jax.ShapeDtypeStruct((N_PAIRS_DIL, n, LANES), jnp.bfloat16),
        scratch_shapes=[pltpu.VMEM((2 * n_br * N_HEADS_DIL, BLOCK, 2 * BLOCK), jnp.float32),
                        pltpu.VMEM((n_br, DIL_SUPER, LANES), jnp.float32),
                        pltpu.VMEM((n_br, DIL_SUPER, LANES), jnp.float32)],
        compiler_params=pltpu.CompilerParams(
            dimension_semantics=("arbitrary", "arbitrary", "arbitrary"),
            vmem_limit_bytes=VMEM_LIMIT),
        name="dilated_attention",
    )(tile_bias, proj, proj, proj, g)


def _router(x1, wrt_hi, wrt_lo, br):
    tm = x1.shape[0]
    xh, xl = _split_bf16(x1)
    lt = _dot_nt(wrt_hi, xh) + _dot_nt(wrt_hi, xl) + _dot_nt(wrt_lo, xh) + br
    rows = [lt[e:e + 1, :] for e in range(N_EXPERTS)]
    m = functools.reduce(jnp.maximum, rows)
    ex = [jnp.exp(r - m) for r in rows]
    tot = functools.reduce(lambda a, c: a + c, ex)
    probs = [e / tot for e in ex]

    keep, score = [], []
    for gi in range(N_GROUPS):
        members = range(gi * EXPERTS_PER_GROUP, (gi + 1) * EXPERTS_PER_GROUP)
        s = None
        for e in members:
            rank = None
            for o in members:
                if o == e:
                    continue
                beats = (probs[o] > probs[e]) | ((probs[o] == probs[e]) & (o < e))
                beats = beats.astype(jnp.int32)
                rank = beats if rank is None else rank + beats
            k = rank < 2
            keep.append(k)
            contrib = jnp.where(k, probs[e], 0.0)
            s = contrib if s is None else s + contrib
        score.append(s)

    sel, gates = [], []
    for gi in range(N_GROUPS):
        chosen = None
        for o in range(N_GROUPS):
            if o == gi:
                continue
            c = (score[gi] > score[o]) if o < gi else (score[gi] >= score[o])
            chosen = c if chosen is None else (chosen & c)
        for e in range(gi * EXPERTS_PER_GROUP, (gi + 1) * EXPERTS_PER_GROUP):
            s = keep[e] & chosen
            sel.append(s)
            gates.append(jnp.where(s, probs[e] / score[gi], 0.0))

    idx0 = functools.reduce(jnp.minimum, [jnp.where(sel[e], e, N_EXPERTS) for e in range(N_EXPERTS)])
    idx1 = functools.reduce(jnp.maximum, [jnp.where(sel[e], e, -1) for e in range(N_EXPERTS)])
    g0 = sum(jnp.where(idx0 == e, gates[e], 0.0) for e in range(N_EXPERTS))
    g1 = sum(jnp.where(idx1 == e, gates[e], 0.0) for e in range(N_EXPERTS))

    row_id = lax.broadcasted_iota(jnp.int32, (N_EXPERTS, tm), 0)
    selmat = jnp.zeros((N_EXPERTS, tm), jnp.float32)
    for e in range(N_EXPERTS):
        selmat = jnp.where((row_id == e) & sel[e], 1.0, selmat)
    return idx0, idx1, g0, g1, selmat


def _to_token_tiles(ref, x):
    rows = x.shape[0]
    for s in range(TOKEN_TILE_ROWS):
        ref[pl.ds(s, rows, stride=TOKEN_TILE_ROWS), :] = x[:, s * LANES:(s + 1) * LANES]


def _from_token_tiles(ref, rows):
    return jnp.concatenate([ref[pl.ds(s, rows, stride=TOKEN_TILE_ROWS), :]
                            for s in range(TOKEN_TILE_ROWS)], axis=1)


def _out_proj_kernel(x_ref, ysb_ref, ydl_ref, w_ref, lng_ref, lnb_ref,
                     wrh_ref, wrl_ref, br_ref,
                     x1_ref, x1t_ref, meta_ref, gcol_ref, cnt_ref, carry_ref, *, alpha):
    i = pl.program_id(0)
    tm = x_ref.shape[0]

    @pl.when(i == 0)
    def _():
        carry_ref[...] = jnp.zeros_like(carry_ref)

    y = jnp.concatenate([ysb_ref[j] for j in range(N_PAIRS_SB)]
                        + [ydl_ref[j] for j in range(N_PAIRS_DIL)], axis=1)
    h = _dot(y, w_ref[...])
    x1 = _layer_norm(alpha * x_ref[...] + h, lng_ref[...], lnb_ref[...])
    x1_ref[...] = x1
    _to_token_tiles(x1t_ref, x1)

    idx0, idx1, g0, g1, selmat = _router(x1, wrh_ref[...], wrl_ref[...], br_ref[...])

    jj = lax.broadcasted_iota(jnp.int32, (tm, tm + LANES), 0)
    ss = lax.broadcasted_iota(jnp.int32, (tm, tm + LANES), 1)
    before = jnp.where((jj < ss) | (ss >= tm), 1.0, 0.0).astype(jnp.bfloat16)
    pc = _dot(_bf16(selmat), before)
    carry = carry_ref[...]
    rankmat = pc[:, :tm] + jnp.concatenate([carry] * (tm // LANES), axis=1)
    carry = carry + pc[:, tm:]
    carry_ref[...] = carry
    cnt_ref[...] = carry

    rank0 = sum(jnp.where(idx0 == e, rankmat[e:e + 1, :], 0.0) for e in range(N_EXPERTS))
    rank1 = sum(jnp.where(idx1 == e, rankmat[e:e + 1, :], 0.0) for e in range(N_EXPERTS))

    row8 = lax.broadcasted_iota(jnp.int32, (8, tm), 0)
    meta = jnp.zeros((8, tm), jnp.int32)
    for r, v in enumerate((idx0, idx1, rank0.astype(jnp.int32), rank1.astype(jnp.int32))):
        meta = jnp.where(row8 == r, v, meta)
    meta_ref[...] = meta

    rowl = lax.broadcasted_iota(jnp.int32, (LANES, tm), 0)
    gt = jnp.where(rowl == 0, g0, jnp.where(rowl == 1, g1, 0.0))
    gcol_ref[...] = gt.T


def _out_proj(x2d, y_sb, y_dl, w_out, ln_g, ln_b, wrt_hi, wrt_lo, br, alpha):
    n, d = x2d.shape
    const = lambda i: (0, 0)
    return pl.pallas_call(
        functools.partial(_out_proj_kernel, alpha=alpha),
        grid=(n // ROW_TILE,),
        in_specs=[
            pl.BlockSpec((ROW_TILE, d), lambda i: (i, 0)),
            pl.BlockSpec((N_PAIRS_SB, ROW_TILE, LANES), lambda i: (0, i, 0)),
            pl.BlockSpec((N_PAIRS_DIL, ROW_TILE, LANES), lambda i: (0, i, 0)),
            pl.BlockSpec((d, d), const),
            pl.BlockSpec((1, d), const),
            pl.BlockSpec((1, d), const),
            pl.BlockSpec((N_EXPERTS, d), const),
            pl.BlockSpec((N_EXPERTS, d), const),
            pl.BlockSpec((N_EXPERTS, 1), const),
        ],
        out_specs=[pl.BlockSpec((ROW_TILE, d), lambda i: (i, 0)),
                   pl.BlockSpec((ROW_TILE * TOKEN_TILE_ROWS, LANES), lambda i: (i, 0)),
                   pl.BlockSpec((8, ROW_TILE), lambda i: (0, i)),
                   pl.BlockSpec((ROW_TILE, LANES), lambda i: (i, 0)),
                   pl.BlockSpec((N_EXPERTS, LANES), const)],
        out_shape=[jax.ShapeDtypeStruct((n, d), jnp.float32),
                   jax.ShapeDtypeStruct((n * TOKEN_TILE_ROWS, LANES), jnp.float32),
                   jax.ShapeDtypeStruct((8, n), jnp.int32),
                   jax.ShapeDtypeStruct((n, LANES), jnp.float32),
                   jax.ShapeDtypeStruct((N_EXPERTS, LANES), jnp.float32)],
        scratch_shapes=[pltpu.VMEM((N_EXPERTS, LANES), jnp.float32)],
        compiler_params=pltpu.CompilerParams(dimension_semantics=("arbitrary",),
                                             vmem_limit_bytes=VMEM_LIMIT),
        name="out_proj_ln_router",
    )(x2d, y_sb, y_dl, w_out, ln_g, ln_b, wrt_hi, wrt_lo, br)


def _segment_offsets(cnt_ref, off_ref):
    acc = jnp.int32(0)
    padded = []
    for e in range(N_EXPERTS):
        off_ref[e] = acc
        pc = lax.div(cnt_ref[e] + (MOE_TM - 1), MOE_TM) * MOE_TM
        padded.append(pc)
        acc = acc + pc
    return padded, acc


def _row_copy(src_ref, src_row, dst_ref, dst_row, sem):
    def tile(row):
        if isinstance(row, int):
            return pl.ds(row * TOKEN_TILE_ROWS, TOKEN_TILE_ROWS)
        return pl.ds(pl.multiple_of(row * TOKEN_TILE_ROWS, TOKEN_TILE_ROWS), TOKEN_TILE_ROWS)
    return pltpu.make_async_copy(src_ref.at[tile(src_row), :], dst_ref.at[tile(dst_row), :], sem)


def _plan_kernel(cnt_ref, e0_ref, e1_ref, r0_ref, r1_ref, tok_ref, te_ref, nv_ref, off_ref,
                 *, max_tiles, n_tokens):
    i = pl.program_id(0)
    tp = e0_ref.shape[0]
    padded, total = _segment_offsets(cnt_ref, off_ref)

    @pl.when(i == 0)
    def _():
        def clear(k, c):
            tok_ref[k] = 0
            return c
        for e in range(N_EXPERTS):
            lax.fori_loop(off_ref[e] + cnt_ref[e], off_ref[e] + padded[e], clear, 0)
        lax.fori_loop(total, max_tiles * MOE_TM, clear, 0)

        n_tiles = lax.div(total, MOE_TM)
        for e in range(N_EXPERTS):
            first = lax.div(off_ref[e], MOE_TM)
            count = lax.div(padded[e], MOE_TM)

            def fill(k, c, first=first, e=e):
                te_ref[first + k] = e
                nv_ref[first + k] = jnp.minimum(cnt_ref[e] - k * MOE_TM, MOE_TM)
                return c
            lax.fori_loop(0, count, fill, 0)
        last = te_ref[n_tiles - 1]

        def fill_tail(k, c):
            te_ref[k] = last
            nv_ref[k] = 0
            return c
        lax.fori_loop(n_tiles, max_tiles, fill_tail, 0)
        te_ref[max_tiles] = n_tiles

    def place(j, c):
        token = i * tp + j
        tok_ref[off_ref[e0_ref[j]] + r0_ref[j]] = token
        tok_ref[off_ref[e1_ref[j]] + r1_ref[j]] = token + n_tokens
        return c
    lax.fori_loop(0, tp, place, 0, unroll=8)


def _plan(counts, e0, e1, r0, r1, max_tiles):
    n = e0.shape[0]
    smem = pl.BlockSpec(memory_space=pltpu.SMEM)
    vec = pl.BlockSpec((PERM_TP,), lambda i: (i,), memory_space=pltpu.SMEM)
    return pl.pallas_call(
        functools.partial(_plan_kernel, max_tiles=max_tiles, n_tokens=n),
        grid=(n // PERM_TP,),
        in_specs=[smem, vec, vec, vec, vec],
        out_specs=[smem, smem, smem],
        out_shape=[jax.ShapeDtypeStruct((max_tiles * MOE_TM,), jnp.int32),
                   jax.ShapeDtypeStruct((max_tiles + 1,), jnp.int32),
                   jax.ShapeDtypeStruct((max_tiles,), jnp.int32)],
        scratch_shapes=[pltpu.SMEM((N_EXPERTS,), jnp.int32)],
        compiler_params=pltpu.CompilerParams(dimension_semantics=("arbitrary",)),
        name="moe_plan",
    )(counts, e0, e1, r0, r1)


def _experts_kernel(tok_ref, te_ref, nv_ref, layer_ref, x_ref, wg_ref, wu_ref, wd_ref, y_ref,
                    xbuf_ref, ybuf_ref, wgb_ref, wub_ref, wdb_ref, gsem, ssem, *, max_tiles):
    i = pl.program_id(0)
    n_tiles = te_ref[max_tiles]
    slot = i & 1
    tile_rows = MOE_TM * TOKEN_TILE_ROWS
    n_tokens = x_ref.shape[0] // TOKEN_TILE_ROWS

    def gather(tile, s):
        base = tile * MOE_TM

        def body(j, c):
            code = tok_ref[base + j]
            token = jnp.where(code >= n_tokens, code - n_tokens, code)
            _row_copy(x_ref, token, xbuf_ref.at[s], j, gsem.at[s]).start()
            return c
        lax.fori_loop(0, MOE_TM, body, 0, unroll=8)

    def scatter_copy(tile, s, j):
        return _row_copy(ybuf_ref.at[s], j, y_ref, tok_ref[tile * MOE_TM + j], ssem.at[s])

    def issue_scatter(tile, s):
        def body(j, c):
            scatter_copy(tile, s, j).start()
            return c
        full = nv_ref[tile] == MOE_TM

        @pl.when(full)
        def _():
            lax.fori_loop(0, MOE_TM, body, 0, unroll=8)

        @pl.when(jnp.logical_not(full))
        def _():
            lax.fori_loop(0, nv_ref[tile], body, 0)

    def wait_scatter(tile, s):
        def body(j, c):
            scatter_copy(tile, s, j).wait()
            return c
        full = nv_ref[tile] == MOE_TM

        @pl.when(full)
        def _():
            pltpu.make_async_copy(ybuf_ref.at[s], y_ref.at[pl.ds(0, tile_rows), :],
                                  ssem.at[s]).wait()

        @pl.when(jnp.logical_not(full))
        def _():
            lax.fori_loop(0, nv_ref[tile], body, 0)

    @pl.when(i == 0)
    def _():
        gather(0, 0)

    @pl.when(i + 1 < n_tiles)
    def _():
        gather(i + 1, 1 - slot)

    @pl.when(i < n_tiles)
    def _():
        pltpu.make_async_copy(x_ref.at[pl.ds(0, tile_rows), :], xbuf_ref.at[slot],
                              gsem.at[slot]).wait()

        @pl.when(i >= 2)
        def _():
            wait_scatter(i - 2, slot)

        @pl.when((i == 0) | (te_ref[i] != te_ref[jnp.maximum(i - 1, 0)]))
        def _():
            wgb_ref[...] = _bf16(wg_ref[...])
            wub_ref[...] = _bf16(wu_ref[...])
            wdb_ref[...] = _bf16(wd_ref[...])

        xb = _bf16(_from_token_tiles(xbuf_ref.at[slot], MOE_TM))
        gt = _dot(xb, wgb_ref[...])
        up = _dot(xb, wub_ref[...])
        h = gt / (1.0 + jnp.exp(-gt)) * up
        _to_token_tiles(ybuf_ref.at[slot], _dot(_bf16(h), wdb_ref[...]))

        issue_scatter(i, slot)

        @pl.when(i == n_tiles - 1)
        def _():
            @pl.when(i >= 1)
            def _():
                wait_scatter(i - 1, 1 - slot)
            wait_scatter(i, slot)


def _experts(tok, te, nv, x1t, w_gate, w_up, w_down, layer, max_tiles):
    d, f = w_gate.shape[-2:]
    rows = MOE_TM * TOKEN_TILE_ROWS
    any_spec = pl.BlockSpec(memory_space=pl.ANY)
    expert = lambda i, tok, te, nv, lay: (lay[0], te[i], 0, 0)
    return pl.pallas_call(
        functools.partial(_experts_kernel, max_tiles=max_tiles),
        grid_spec=pltpu.PrefetchScalarGridSpec(
            num_scalar_prefetch=4,
            grid=(max_tiles,),
            in_specs=[any_spec,
                      pl.BlockSpec((None, None, d, f), expert),
                      pl.BlockSpec((None, None, d, f), expert),
                      pl.BlockSpec((None, None, f, d), expert)],
            out_specs=any_spec,
            scratch_shapes=[pltpu.VMEM((2, rows, LANES), jnp.float32),
                            pltpu.VMEM((2, rows, LANES), jnp.float32),
                            pltpu.VMEM((d, f), jnp.bfloat16),
                            pltpu.VMEM((d, f), jnp.bfloat16),
                            pltpu.VMEM((f, d), jnp.bfloat16),
                            pltpu.SemaphoreType.DMA((2,)),
                            pltpu.SemaphoreType.DMA((2,))]),
        out_shape=jax.ShapeDtypeStruct((2 * x1t.shape[0], LANES), jnp.float32),
        compiler_params=pltpu.CompilerParams(dimension_semantics=("arbitrary",),
                                             vmem_limit_bytes=VMEM_LIMIT),
        name="moe_experts",
    )(tok, te, nv, jnp.full((1,), layer, jnp.int32), x1t, w_gate, w_up, w_down)


def _combine_kernel(x_ref, y0_ref, y1_ref, gcol_ref, lng_ref, lnb_ref, o_ref, *, alpha):
    tm = x_ref.shape[0]
    lane = lax.broadcasted_iota(jnp.int32, (1, LANES), 1)
    gc = gcol_ref[...]
    g0 = jnp.sum(jnp.where(lane == 0, gc, 0.0), axis=-1, keepdims=True)
    g1 = jnp.sum(jnp.where(lane == 1, gc, 0.0), axis=-1, keepdims=True)
    y = g0 * _from_token_tiles(y0_ref, tm) + g1 * _from_token_tiles(y1_ref, tm)
    o_ref[...] = _layer_norm(alpha * x_ref[...] + y, lng_ref[...], lnb_ref[...])


def _combine(x1, y, gcol, ln_g, ln_b, alpha):
    n, d = x1.shape
    const = lambda i: (0, 0)
    rows = ROW_TILE * TOKEN_TILE_ROWS
    return pl.pallas_call(
        functools.partial(_combine_kernel, alpha=alpha),
        grid=(n // ROW_TILE,),
        in_specs=[pl.BlockSpec((ROW_TILE, d), lambda i: (i, 0)),
                  pl.BlockSpec((rows, LANES), lambda i: (i, 0)),
                  pl.BlockSpec((rows, LANES), lambda i: (i + n // ROW_TILE, 0)),
                  pl.BlockSpec((ROW_TILE, LANES), lambda i: (i, 0)),
                  pl.BlockSpec((1, d), const), pl.BlockSpec((1, d), const)],
        out_specs=pl.BlockSpec((ROW_TILE, d), lambda i: (i, 0)),
        out_shape=jax.ShapeDtypeStruct((n, d), jnp.float32),
        compiler_params=pltpu.CompilerParams(dimension_semantics=("arbitrary",),
                                             vmem_limit_bytes=VMEM_LIMIT),
        name="moe_combine_ln",
    )(x1, y, y, gcol, ln_g, ln_b)


def _moe(x1, x1t, meta, gcol, cnt, w_gate, w_up, w_down, layer, ln_g, ln_b, alpha):
    n = x1.shape[0]
    max_tiles = (2 * n + N_EXPERTS * (MOE_TM - 1)) // MOE_TM
    counts = cnt[:, 0].astype(jnp.int32)
    e0, e1, r0, r1 = meta[0], meta[1], meta[2], meta[3]
    tok, te, nv = _plan(counts, e0, e1, r0, r1, max_tiles)
    y = _experts(tok, te, nv, x1t, w_gate, w_up, w_down, layer, max_tiles)
    return _combine(x1, y, gcol, ln_g, ln_b, alpha)


def _t5_bucket(dist):
    max_exact = N_BUCKETS // 2
    d = jnp.maximum(dist, 0)
    large = max_exact + (jnp.log(jnp.maximum(d, 1).astype(jnp.float32) / max_exact)
                         / math.log(MAX_DISTANCE / max_exact) * (N_BUCKETS - max_exact)).astype(jnp.int32)
    large = jnp.minimum(large, N_BUCKETS - 1)
    return jnp.where(d < max_exact, d, large)


def _tile_bias_kernel(table_ref, o_ref):
    slot = pl.program_id(0)
    qi = lax.broadcasted_iota(jnp.int32, (BLOCK, 2 * BLOCK), 0)
    kj = lax.broadcasted_iota(jnp.int32, (BLOCK, 2 * BLOCK), 1)
    steps = qi + BLOCK - kj

    def per_step(s, acc):
        return jnp.where(steps == s, table_ref[slot, s], acc)
    o_ref[...] = lax.fori_loop(0, BLOCK + 1, per_step,
                               jnp.zeros((BLOCK, 2 * BLOCK), jnp.float32))


def _tile_bias(rel_bias):
    dist = jnp.arange(BLOCK + 1)
    rows = jnp.stack([rel_bias[_t5_bucket(dist * dilation)] for _, dilation in DIL_PATTERNS])
    n_br, _, heads = rows.shape
    table = rows.astype(jnp.float32).transpose(0, 2, 1).reshape(n_br * heads, BLOCK + 1)
    return pl.pallas_call(
        _tile_bias_kernel,
        grid=(n_br * heads,),
        in_specs=[pl.BlockSpec(memory_space=pltpu.SMEM)],
        out_specs=pl.BlockSpec((None, BLOCK, 2 * BLOCK), lambda s: (s, 0, 0)),
        out_shape=jax.ShapeDtypeStruct((n_br * heads, BLOCK, 2 * BLOCK), jnp.float32),
        compiler_params=pltpu.CompilerParams(dimension_semantics=("arbitrary",)),
        name="tile_bias",
    )(table)


def kernel(x, w_in, g_sb, g_dil, w_out, ln1_g, ln1_b, ln2_g, ln2_b, rel_bias,
           w_router, b_router, w_gate, w_up, w_down):
    batch, seq, d = x.shape
    depth = w_in.shape[0]
    alpha = (2.0 * depth) ** 0.25
    n = batch * seq
    x2d = x.reshape(n, d)

    tile_bias = _tile_bias(rel_bias)
    wrt =w_router.T.astype(jnp.float32)
    wrt_hi = _bf16(wrt)
    wrt_lo = _bf16(wrt - wrt_hi.astype(jnp.float32))
    br = b_router.astype(jnp.float32).reshape(N_EXPERTS, 1)
    sb_cols = 3 * D_SB

    for l in range(depth):
        w_in_l = _bf16(w_in[l])
        proj_sb = _in_proj(x2d, w_in_l[:, :sb_cols], jnp.bfloat16, N_PAIRS_SB, "in_proj_sb")
        proj_dl = _in_proj(x2d, w_in_l[:, sb_cols:], jnp.float32, N_PAIRS_DIL, "in_proj_dil")
        y_sb = _sb_attention(proj_sb, g_sb[l].reshape(N_PAIRS_SB, 1, LANES), batch, seq)
        y_dl = _dil_attention(proj_dl, g_dil[l].reshape(N_PAIRS_DIL, 1, LANES),
                              tile_bias, batch, seq)
        x1, x1t, meta, gcol, cnt = _out_proj(x2d, y_sb, y_dl, _bf16(w_out[l]),
                                             ln1_g[l].reshape(1, d), ln1_b[l].reshape(1, d),
                                             wrt_hi, wrt_lo, br, alpha)
        x2d = _moe(x1, x1t, meta, gcol, cnt, w_gate, w_up, w_down, l,
                   ln2_g[l].reshape(1, d), ln2_b[l].reshape(1, d), alpha)
    return x2d.reshape(batch, seq, d)
```

```python
import functools
import math

import jax
import jax.numpy as jnp
from jax import lax
from jax.experimental import pallas as pl
from jax.experimental.pallas import tpu as pltpu

D_MODEL = 1024
HEAD_DIM = 64
N_HEADS_SB = 8
N_HEADS_DIL = 8
D_SB = N_HEADS_SB * HEAD_DIM
D_DIL = N_HEADS_DIL * HEAD_DIM
DIL_PATTERNS = ((128, 1), (512, 4), (2048, 16))
BLOCK = 128
N_BUCKETS = 32
MAX_DISTANCE = 2048
N_EXPERTS = 16
N_GROUPS = 4
EXPERTS_PER_GROUP = N_EXPERTS // N_GROUPS
D_FF_EXPERT = 1024
LN_EPS = 1e-5
NEG_INF = -1e30

LANES = 128
N_PAIRS_SB = D_SB // LANES
N_PAIRS_DIL = D_DIL // LANES
VMEM_LIMIT = 56 * 1024 * 1024

ROW_TILE = 512
SB_TQ = 256
SB_TK = 256
SB_TILES_PER_STEP = 2
SB_SKIP_BOUND = -110.0
DIL_SUPER = 2048
DIL_UNITS = DIL_SUPER // BLOCK
TOKEN_TILE_ROWS = D_MODEL // LANES
MOE_TM = 256
MOE_TM_LOG2 = MOE_TM.bit_length() - 1
PERM_TP = 512

_NT = (((1,), (1,)), ((), ()))


def _bf16(x):
    return x.astype(jnp.bfloat16)


def _split_bf16(x):
    hi = _bf16(x)
    lo = _bf16(x - hi.astype(jnp.float32))
    return hi, lo


def _dot(a, b):
    return jnp.dot(a, b, preferred_element_type=jnp.float32)


def _dot_nt(a, b):
    return lax.dot_general(a, b, _NT, preferred_element_type=jnp.float32)


def _head_rms_gain(o, g):
    r = lax.broadcasted_iota(jnp.int32, (LANES, LANES), 0) // HEAD_DIM
    c = lax.broadcasted_iota(jnp.int32, (LANES, LANES), 1) // HEAD_DIM
    same_head = jnp.where(r == c, 1.0 / HEAD_DIM, 0.0).astype(jnp.bfloat16)
    hi, lo = _split_bf16(o * o)
    ms = _dot(hi, same_head) + _dot(lo, same_head)
    return o * lax.rsqrt(ms + 1e-6) * g


def _layer_norm(x, g, b):
    mu = jnp.mean(x, axis=-1, keepdims=True)
    xc = x - mu
    var = jnp.mean(xc * xc, axis=-1, keepdims=True)
    return xc * lax.rsqrt(var + LN_EPS) * g + b


def _in_proj_kernel(x_ref, w_ref, o_ref, *, n_scaled, scale):
    x = _bf16(x_ref[...])
    n_slabs = o_ref.shape[0]
    for j in range(0, n_slabs, 2):
        res = _dot(x, w_ref[:, j * LANES:(j + 2) * LANES])
        for jj in range(2):
            blk = res[:, jj * LANES:(jj + 1) * LANES]
            if j + jj < n_scaled:
                blk = blk * scale
            o_ref[j + jj] = blk.astype(o_ref.dtype)


def _in_proj(x2d, w, out_dtype, n_scaled, name):
    n, d = x2d.shape
    n_slabs = w.shape[1] // LANES
    return pl.pallas_call(
        functools.partial(_in_proj_kernel, n_scaled=n_scaled, scale=HEAD_DIM ** -0.5),
        grid=(n // ROW_TILE,),
        in_specs=[pl.BlockSpec((ROW_TILE, d), lambda i: (i, 0)),
                  pl.BlockSpec((d, w.shape[1]), lambda i: (0, 0))],
        out_specs=pl.BlockSpec((n_slabs, ROW_TILE, LANES), lambda i: (0, i, 0)),
        out_shape=jax.ShapeDtypeStruct((n_slabs, n, LANES), out_dtype),
        compiler_params=pltpu.CompilerParams(dimension_semantics=("arbitrary",),
                                             vmem_limit_bytes=VMEM_LIMIT),
        name=name,
    )(x2d, w)


def _sb_kernel(q_ref, k_ref, v_ref, g_ref, o_ref, carry_ref, acc_ref):
    step = pl.program_id(2)
    lane = lax.broadcasted_iota(jnp.int32, (1, LANES), 1)
    first_head = lane < HEAD_DIM

    jj = lax.broadcasted_iota(jnp.int32, (SB_TK, SB_TK + LANES), 0)
    ss = lax.broadcasted_iota(jnp.int32, (SB_TK, SB_TK + LANES), 1)
    suffix = jnp.where((jj > ss) | (ss >= SB_TK), 1.0, 0.0).astype(jnp.bfloat16)

    tq_i = lax.broadcasted_iota(jnp.int32, (2 * SB_TQ, SB_TK), 0) % SB_TQ
    ts_i = lax.broadcasted_iota(jnp.int32, (2 * SB_TQ, SB_TK), 1)
    causal = ts_i < tq_i

    def block(q2, kb, diag, carry, acc, weight=None):
        off = pl.multiple_of(kb * SB_TK, SB_TK)
        kblk = k_ref[pl.ds(off, SB_TK), :]
        vblk = v_ref[pl.ds(off, SB_TK), :]
        z = _dot_nt(q2, kblk)
        sp = jnp.log(1.0 + jnp.exp(-jnp.abs(z)))
        log_beta = jnp.minimum(z, 0.0) - sp
        log_1m = log_beta - z
        if diag:
            log_1m = jnp.where(causal, log_1m, 0.0)
        sfx = _dot(_bf16(log_1m), suffix)
        between = sfx[:, :SB_TK] + jnp.concatenate([carry] * (SB_TK // LANES), axis=1)
        a = jnp.exp(log_beta + between)
        if diag:
            a = jnp.where(causal, a, 0.0)
        out = _dot(_bf16(a), vblk)
        tot = sfx[:, SB_TK:]
        if weight is not None:
            out = out * weight
            tot = tot * weight
        return carry + tot, acc + out

    zeros = jnp.zeros((2 * SB_TQ, LANES), jnp.float32)
    tiles = []
    for t in range(SB_TILES_PER_STEP):
        i = step * SB_TILES_PER_STEP + t
        q = q_ref[t * SB_TQ:(t + 1) * SB_TQ, :]
        zero = jnp.zeros_like(q)
        q2 = jnp.concatenate([jnp.where(first_head, q, zero), jnp.where(first_head, zero, q)], axis=0)
        carry, acc = block(q2, i, True, zeros, zeros)
        has_prev = jnp.where(i >= 1, 1.0, 0.0)
        carry, acc = block(q2, jnp.maximum(i - 1, 0), False, carry, acc, weight=has_prev)
        carry_ref[t] = carry
        acc_ref[t] = acc
        tiles.append((i, q2, jnp.max(carry)))

    for t, (i, q2, mx0) in enumerate(tiles):
        def cond(state):
            kb, mx = state
            return jnp.logical_and(kb >= 0, mx > SB_SKIP_BOUND)

        def body(state, t=t, q2=q2):
            kb, _ = state
            c, a = block(q2, kb, False, carry_ref[t], acc_ref[t])
            carry_ref[t] = c
            acc_ref[t] = a
            return kb - 1, jnp.max(c)

        lax.while_loop(cond, body, (i - 2, mx0))

        o = jnp.where(first_head, acc_ref[t, :SB_TQ, :], acc_ref[t, SB_TQ:, :])
        o_ref[t * SB_TQ:(t + 1) * SB_TQ, :] = _head_rms_gain(o, g_ref[...]).astype(o_ref.dtype)


def _sb_attention(proj, g, batch, seq):
    n = batch * seq
    rows = SB_TQ * SB_TILES_PER_STEP
    nq = seq // rows
    return pl.pallas_call(
        _sb_kernel,
        grid=(batch, N_PAIRS_SB, nq),
        in_specs=[
            pl.BlockSpec((None, rows, LANES), lambda b, p, i: (p, b * nq + i, 0)),
            pl.BlockSpec((None, seq, LANES), lambda b, p, i: (N_PAIRS_SB + p, b, 0)),
            pl.BlockSpec((None, seq, LANES), lambda b, p, i: (2 * N_PAIRS_SB + p, b, 0)),
            pl.BlockSpec((None, 1, LANES), lambda b, p, i: (p, 0, 0)),
        ],
        out_specs=pl.BlockSpec((None, rows, LANES), lambda b, p, i: (p, b * nq + i, 0)),
        out_shape=jax.ShapeDtypeStruct((N_PAIRS_SB, n, LANES), jnp.bfloat16),
        scratch_shapes=[pltpu.VMEM((SB_TILES_PER_STEP, 2 * SB_TQ, LANES), jnp.float32),
                        pltpu.VMEM((SB_TILES_PER_STEP, 2 * SB_TQ, LANES), jnp.float32)],
        compiler_params=pltpu.CompilerParams(
            dimension_semantics=("arbitrary", "arbitrary", "arbitrary"),
            vmem_limit_bytes=VMEM_LIMIT),
        name="sb_attention",
    )(proj, proj, proj, g)


def _dil_kernel(tbias_ref, q_ref, k_ref, v_ref, g_ref, o_ref,
                bias_ref, obr_ref, lse_ref):
    b = pl.program_id(0)
    p = pl.program_id(1)
    i = pl.program_id(2)
    n_br = len(DIL_PATTERNS)

    qi = lax.broadcasted_iota(jnp.int32, (BLOCK, 2 * BLOCK), 0)
    kj = lax.broadcasted_iota(jnp.int32, (BLOCK, 2 * BLOCK), 1)
    steps = qi + BLOCK - kj

    assert all(window // dilation == BLOCK for window, dilation in DIL_PATTERNS)
    valid = (steps >= 0) & (steps <= BLOCK)

    @pl.when((b == 0) & (p == 0) & (i == 0))
    def _():
        def per_tile(s, _):
            tile = tbias_ref[s]
            bias_ref[2 * s] = jnp.where(valid, tile, NEG_INF)
            bias_ref[2 * s + 1] = jnp.where(valid & (kj >= BLOCK), tile, NEG_INF)
            return 0
        lax.fori_loop(0, n_br * N_HEADS_DIL, per_tile, 0)

    lane = lax.broadcasted_iota(jnp.int32, (1, LANES), 1)
    first_head = lane < HEAD_DIM
    t0 = i * DIL_SUPER

    def strided(ref, start, dilation):
        if dilation == 1:
            return ref[pl.ds(start, BLOCK), :]
        return ref[pl.ds(start, BLOCK, stride=dilation), :]

    for gidx, (window, dilation) in enumerate(DIL_PATTERNS):
        units_per_residue = DIL_UNITS // dilation
        shift = units_per_residue.bit_length() - 1

        def unit(u, _, gidx=gidx, dilation=dilation, units_per_residue=units_per_residue, shift=shift):
            r = lax.shift_right_logical(u, shift)
            cb = u & (units_per_residue - 1)
            sq = r + cb * (dilation * BLOCK)
            cur = t0 + sq
            prev = cur - dilation * BLOCK
            prev_ok = prev >= 0
            prev_c = jnp.where(prev_ok, prev, cur)
            no_prev = jnp.where(prev_ok, 0, 1)

            qb = _bf16(strided(q_ref, sq, dilation))
            kk = _bf16(jnp.concatenate([strided(k_ref, prev_c, dilation),
                                        strided(k_ref, cur, dilation)], axis=0))
            vv = _bf16(jnp.concatenate([strided(v_ref, prev_c, dilation),
                                        strided(v_ref, cur, dilation)], axis=0))
            zero = jnp.zeros_like(qb)
            outs, lses = [], []
            for h in range(2):
                qm = jnp.where(first_head, qb, zero) if h == 0 else jnp.where(first_head, zero, qb)
                z = _dot_nt(qm, kk)
                lg = z + bias_ref[2 * (gidx * N_HEADS_DIL + 2 * p + h) + no_prev]
                m = jnp.max(lg, axis=-1, keepdims=True)
                pe = jnp.exp(lg - m)
                den = jnp.sum(pe, axis=-1, keepdims=True)
                outs.append(_dot(_bf16(pe), vv) / den)
                lses.append(m + jnp.log(den))
            o_tile = jnp.where(first_head, outs[0], outs[1])
            l_tile = jnp.where(first_head, lses[0], lses[1])
            if dilation == 1:
                idx = pl.ds(sq, BLOCK)
            else:
                idx = pl.ds(sq, BLOCK, stride=dilation)
            obr_ref.at[gidx][idx, :] = o_tile
            lse_ref.at[gidx][idx, :] = l_tile
            return 0

        lax.fori_loop(0, DIL_UNITS, unit, 0, unroll=8)

    chunk = 256

    def combine(c, _):
        rows = pl.ds(pl.multiple_of(c * chunk, chunk), chunk)
        ls = [lse_ref[gidx, rows, :] for gidx in range(n_br)]
        m = functools.reduce(jnp.maximum, ls)
        es = [jnp.exp(l - m) for l in ls]
        num = sum(e * obr_ref[gidx, rows, :] for gidx, e in enumerate(es))
        o = num / sum(es)
        o_ref[rows, :] = _head_rms_gain(o, g_ref[...]).astype(o_ref.dtype)
        return 0

    lax.fori_loop(0, DIL_SUPER // chunk, combine, 0)


def _dil_attention(proj, g, tile_bias, batch, seq):
    n = batch * seq
    ns = seq // DIL_SUPER
    n_br = len(DIL_PATTERNS)
    return pl.pallas_call(
        _dil_kernel,
        grid=(batch, N_PAIRS_DIL, ns),
        in_specs=[
            pl.BlockSpec((n_br * N_HEADS_DIL, BLOCK, 2 * BLOCK), lambda b, p, i: (0, 0, 0)),
            pl.BlockSpec((None, DIL_SUPER, LANES), lambda b, p, i: (p, b * ns + i, 0)),
            pl.BlockSpec((None, seq, LANES), lambda b, p, i: (N_PAIRS_DIL + p, b, 0)),
            pl.BlockSpec((None, seq, LANES), lambda b, p, i: (2 * N_PAIRS_DIL + p, b, 0)),
            pl.BlockSpec((None, 1, LANES), lambda b, p, i: (p, 0, 0)),
        ],
        out_specs=pl.BlockSpec((None, DIL_SUPER, LANES), lambda b, p, i: (p, b * ns + i, 0)),
        out_shape=jax.ShapeDtypeStruct((N_PAIRS_DIL, n, LANES), jnp.bfloat16),
        scratch_shapes=[pltpu.VMEM((2 * n_br * N_HEADS_DIL, BLOCK, 2 * BLOCK), jnp.float32),
                        pltpu.VMEM((n_br, DIL_SUPER, LANES), jnp.float32),
                        pltpu.VMEM((n_br, DIL_SUPER, LANES), jnp.float32)],
        compiler_params=pltpu.CompilerParams(
            dimension_semantics=("arbitrary", "arbitrary", "arbitrary"),
            vmem_limit_bytes=VMEM_LIMIT),
        name="dilated_attention",
    )(tile_bias, proj, proj, proj, g)


def _router(x1, wrt_hi, wrt_lo, br):
    tm = x1.shape[0]
    xh, xl = _split_bf16(x1)
    lt = _dot_nt(wrt_hi, xh) + _dot_nt(wrt_hi, xl) + _dot_nt(wrt_lo, xh) + br
    rows = [lt[e:e + 1, :] for e in range(N_EXPERTS)]
    m = functools.reduce(jnp.maximum, rows)
    ex = [jnp.exp(r - m) for r in rows]
    tot = functools.reduce(lambda a, c: a + c, ex)
    probs = [e / tot for e in ex]

    keep, score = [], []
    for gi in range(N_GROUPS):
        members = range(gi * EXPERTS_PER_GROUP, (gi + 1) * EXPERTS_PER_GROUP)
        s = None
        for e in members:
            rank = None
            for o in members:
                if o == e:
                    continue
                beats = (probs[o] > probs[e]) | ((probs[o] == probs[e]) & (o < e))
                beats = beats.astype(jnp.int32)
                rank = beats if rank is None else rank + beats
            k = rank < 2
            keep.append(k)
            contrib = jnp.where(k, probs[e], 0.0)
            s = contrib if s is None else s + contrib
        score.append(s)

    sel, gates = [], []
    for gi in range(N_GROUPS):
        chosen = None
        for o in range(N_GROUPS):
            if o == gi:
                continue
            c = (score[gi] > score[o]) if o < gi else (score[gi] >= score[o])
            chosen = c if chosen is None else (chosen & c)
        for e in range(gi * EXPERTS_PER_GROUP, (gi + 1) * EXPERTS_PER_GROUP):
            s = keep[e] & chosen
            sel.append(s)
            gates.append(jnp.where(s, probs[e] / score[gi], 0.0))

    idx0 = functools.reduce(jnp.minimum, [jnp.where(sel[e], e, N_EXPERTS) for e in range(N_EXPERTS)])
    idx1 = functools.reduce(jnp.maximum, [jnp.where(sel[e], e, -1) for e in range(N_EXPERTS)])
    g0 = sum(jnp.where(idx0 == e, gates[e], 0.0) for e in range(N_EXPERTS))
    g1 = sum(jnp.where(idx1 == e, gates[e], 0.0) for e in range(N_EXPERTS))

    row_id = lax.broadcasted_iota(jnp.int32, (N_EXPERTS, tm), 0)
    selmat = jnp.zeros((N_EXPERTS, tm), jnp.float32)
    for e in range(N_EXPERTS):
        selmat = jnp.where((row_id == e) & sel[e], 1.0, selmat)
    return idx0, idx1, g0, g1, selmat


def _to_token_tiles(ref, x):
    rows = x.shape[0]
    for s in range(TOKEN_TILE_ROWS):
        ref[pl.ds(s, rows, stride=TOKEN_TILE_ROWS), :] = x[:, s * LANES:(s + 1) * LANES]


def _from_token_tiles(ref, rows):
    return jnp.concatenate([ref[pl.ds(s, rows, stride=TOKEN_TILE_ROWS), :]
                            for s in range(TOKEN_TILE_ROWS)], axis=1)


def _out_proj_kernel(x_ref, ysb_ref, ydl_ref, w_ref, lng_ref, lnb_ref,
                     wrh_ref, wrl_ref, br_ref,
                     x1_ref, x1t_ref, meta_ref, gcol_ref, cnt_ref, carry_ref, *, alpha):
    i = pl.program_id(0)
    tm = x_ref.shape[0]

    @pl.when(i == 0)
    def _():
        carry_ref[...] = jnp.zeros_like(carry_ref)

    y = jnp.concatenate([ysb_ref[j] for j in range(N_PAIRS_SB)]
                        + [ydl_ref[j] for j in range(N_PAIRS_DIL)], axis=1)
    h = _dot(y, w_ref[...])
    x1 = _layer_norm(alpha * x_ref[...] + h, lng_ref[...], lnb_ref[...])
    x1_ref[...] = x1
    _to_token_tiles(x1t_ref, x1)

    idx0, idx1, g0, g1, selmat = _router(x1, wrh_ref[...], wrl_ref[...], br_ref[...])

    jj = lax.broadcasted_iota(jnp.int32, (tm, tm + LANES), 0)
    ss = lax.broadcasted_iota(jnp.int32, (tm, tm + LANES), 1)
    before = jnp.where((jj < ss) | (ss >= tm), 1.0, 0.0).astype(jnp.bfloat16)
    pc = _dot(_bf16(selmat), before)
    carry = carry_ref[...]
    rankmat = pc[:, :tm] + jnp.concatenate([carry] * (tm // LANES), axis=1)
    carry = carry + pc[:, tm:]
    carry_ref[...] = carry
    cnt_ref[...] = carry

    rank0 = sum(jnp.where(idx0 == e, rankmat[e:e + 1, :], 0.0) for e in range(N_EXPERTS))
    rank1 = sum(jnp.where(idx1 == e, rankmat[e:e + 1, :], 0.0) for e in range(N_EXPERTS))

    row8 = lax.broadcasted_iota(jnp.int32, (8, tm), 0)
    meta = jnp.zeros((8, tm), jnp.int32)
    for r, v in enumerate((idx0, idx1, rank0.astype(jnp.int32), rank1.astype(jnp.int32))):
        meta = jnp.where(row8 == r, v, meta)
    meta_ref[...] = meta

    rowl = lax.broadcasted_iota(jnp.int32, (LANES, tm), 0)
    gt = jnp.where(rowl == 0, g0, jnp.where(rowl == 1, g1, 0.0))
    gcol_ref[...] = gt.T


def _out_proj(x2d, y_sb, y_dl, w_out, ln_g, ln_b, wrt_hi, wrt_lo, br, alpha):
    n, d = x2d.shape
    const = lambda i: (0, 0)
    return pl.pallas_call(
        functools.partial(_out_proj_kernel, alpha=alpha),
        grid=(n // ROW_TILE,),
        in_specs=[
            pl.BlockSpec((ROW_TILE, d), lambda i: (i, 0)),
            pl.BlockSpec((N_PAIRS_SB, ROW_TILE, LANES), lambda i: (0, i, 0)),
            pl.BlockSpec((N_PAIRS_DIL, ROW_TILE, LANES), lambda i: (0, i, 0)),
            pl.BlockSpec((d, d), const),
            pl.BlockSpec((1, d), const),
            pl.BlockSpec((1, d), const),
            pl.BlockSpec((N_EXPERTS, d), const),
            pl.BlockSpec((N_EXPERTS, d), const),
            pl.BlockSpec((N_EXPERTS, 1), const),
        ],
        out_specs=[pl.BlockSpec((ROW_TILE, d), lambda i: (i, 0)),
                   pl.BlockSpec((ROW_TILE * TOKEN_TILE_ROWS, LANES), lambda i: (i, 0)),
                   pl.BlockSpec((8, ROW_TILE), lambda i: (0, i)),
                   pl.BlockSpec((ROW_TILE, LANES), lambda i: (i, 0)),
                   pl.BlockSpec((N_EXPERTS, LANES), const)],
        out_shape=[jax.ShapeDtypeStruct((n, d), jnp.float32),
                   jax.ShapeDtypeStruct((n * TOKEN_TILE_ROWS, LANES), jnp.float32),
                   jax.ShapeDtypeStruct((8, n), jnp.int32),
                   jax.ShapeDtypeStruct((n, LANES), jnp.float32),
                   jax.ShapeDtypeStruct((N_EXPERTS, LANES), jnp.float32)],
        scratch_shapes=[pltpu.VMEM((N_EXPERTS, LANES), jnp.float32)],
        compiler_params=pltpu.CompilerParams(dimension_semantics=("arbitrary",),
                                             vmem_limit_bytes=VMEM_LIMIT),
        name="out_proj_ln_router",
    )(x2d, y_sb, y_dl, w_out, ln_g, ln_b, wrt_hi, wrt_lo, br)


def _segment_offsets(cnt_ref, off_ref):
    acc = jnp.int32(0)
    padded = []
    for e in range(N_EXPERTS):
        off_ref[e] = acc
        pc = lax.div(cnt_ref[e] + (MOE_TM - 1), MOE_TM) * MOE_TM
        padded.append(pc)
        acc = acc + pc
    return padded, acc


def _row_copy(src_ref, src_row, dst_ref, dst_row, sem):
    def tile(row):
        if isinstance(row, int):
            return pl.ds(row * TOKEN_TILE_ROWS, TOKEN_TILE_ROWS)
        return pl.ds(pl.multiple_of(row * TOKEN_TILE_ROWS, TOKEN_TILE_ROWS), TOKEN_TILE_ROWS)
    return pltpu.make_async_copy(src_ref.at[tile(src_row), :], dst_ref.at[tile(dst_row), :], sem)


def _plan_kernel(cnt_ref, e0_ref, e1_ref, r0_ref, r1_ref, tok_ref, te_ref, nv_ref, off_ref,
                 *, max_tiles, n_tokens):
    i = pl.program_id(0)
    tp = e0_ref.shape[0]
    padded, total = _segment_offsets(cnt_ref, off_ref)

    @pl.when(i == 0)
    def _():
        def clear(k, c):
            tok_ref[k] = 0
            return c
        for e in range(N_EXPERTS):
            lax.fori_loop(off_ref[e] + cnt_ref[e], off_ref[e] + padded[e], clear, 0)
        lax.fori_loop(total, max_tiles * MOE_TM, clear, 0)

        n_tiles = lax.div(total, MOE_TM)
        for e in range(N_EXPERTS):
            first = lax.div(off_ref[e], MOE_TM)
            count = lax.div(padded[e], MOE_TM)

            def fill(k, c, first=first, e=e):
                te_ref[first + k] = e
                nv_ref[first + k] = jnp.minimum(cnt_ref[e] - k * MOE_TM, MOE_TM)
                return c
            lax.fori_loop(0, count, fill, 0)
        last = te_ref[n_tiles - 1]

        def fill_tail(k, c):
            te_ref[k] = last
            nv_ref[k] = 0
            return c
        lax.fori_loop(n_tiles, max_tiles, fill_tail, 0)
        te_ref[max_tiles] = n_tiles

    def place(j, c):
        token = i * tp + j
        tok_ref[off_ref[e0_ref[j]] + r0_ref[j]] = token
        tok_ref[off_ref[e1_ref[j]] + r1_ref[j]] = token + n_tokens
        return c
    lax.fori_loop(0, tp, place, 0, unroll=8)


def _plan(counts, e0, e1, r0, r1, max_tiles):
    n = e0.shape[0]
    smem = pl.BlockSpec(memory_space=pltpu.SMEM)
    vec = pl.BlockSpec((PERM_TP,), lambda i: (i,), memory_space=pltpu.SMEM)
    return pl.pallas_call(
        functools.partial(_plan_kernel, max_tiles=max_tiles, n_tokens=n),
        grid=(n // PERM_TP,),
        in_specs=[smem, vec, vec, vec, vec],
        out_specs=[smem, smem, smem],
        out_shape=[jax.ShapeDtypeStruct((max_tiles * MOE_TM,), jnp.int32),
                   jax.ShapeDtypeStruct((max_tiles + 1,), jnp.int32),
                   jax.ShapeDtypeStruct((max_tiles,), jnp.int32)],
        scratch_shapes=[pltpu.SMEM((N_EXPERTS,), jnp.int32)],
        compiler_params=pltpu.CompilerParams(dimension_semantics=("arbitrary",)),
        name="moe_plan",
    )(counts, e0, e1, r0, r1)


def _experts_kernel(tok_ref, te_ref, nv_ref, layer_ref, x_ref, wg_ref, wu_ref, wd_ref, y_ref,
                    xbuf_ref, ybuf_ref, wgb_ref, wub_ref, wdb_ref, gsem, ssem, *, max_tiles):
    i = pl.program_id(0)
    n_tiles = te_ref[max_tiles]
    slot = i & 1
    tile_rows = MOE_TM * TOKEN_TILE_ROWS
    n_tokens = x_ref.shape[0] // TOKEN_TILE_ROWS

    def gather(tile, s):
        base = tile * MOE_TM

        def body(j, c):
            code = tok_ref[base + j]
            token = jnp.where(code >= n_tokens, code - n_tokens, code)
            _row_copy(x_ref, token, xbuf_ref.at[s], j, gsem.at[s]).start()
            return c
        lax.fori_loop(0, MOE_TM, body, 0, unroll=8)

    def scatter_copy(tile, s, j):
        return _row_copy(ybuf_ref.at[s], j, y_ref, tok_ref[tile * MOE_TM + j], ssem.at[s])

    def issue_scatter(tile, s):
        def body(j, c):
            scatter_copy(tile, s, j).start()
            return c
        full = nv_ref[tile] == MOE_TM

        @pl.when(full)
        def _():
            lax.fori_loop(0, MOE_TM, body, 0, unroll=8)

        @pl.when(jnp.logical_not(full))
        def _():
            lax.fori_loop(0, nv_ref[tile], body, 0)

    def wait_scatter(tile, s):
        def body(j, c):
            scatter_copy(tile, s, j).wait()
            return c
        full = nv_ref[tile] == MOE_TM

        @pl.when(full)
        def _():
            pltpu.make_async_copy(ybuf_ref.at[s], y_ref.at[pl.ds(0, tile_rows), :],
                                  ssem.at[s]).wait()

        @pl.when(jnp.logical_not(full))
        def _():
            lax.fori_loop(0, nv_ref[tile], body, 0)

    @pl.when(i == 0)
    def _():
        gather(0, 0)

    @pl.when(i + 1 < n_tiles)
    def _():
        gather(i + 1, 1 - slot)

    @pl.when(i < n_tiles)
    def _():
        pltpu.make_async_copy(x_ref.at[pl.ds(0, tile_rows), :], xbuf_ref.at[slot],
                              gsem.at[slot]).wait()

        @pl.when(i >= 2)
        def _():
            wait_scatter(i - 2, slot)

        @pl.when((i == 0) | (te_ref[i] != te_ref[jnp.maximum(i - 1, 0)]))
        def _():
            wgb_ref[...] = _bf16(wg_ref[...])
            wub_ref[...] = _bf16(wu_ref[...])
            wdb_ref[...] = _bf16(wd_ref[...])

        xb = _bf16(_from_token_tiles(xbuf_ref.at[slot], MOE_TM))
        gt = _dot(xb, wgb_ref[...])
        up = _dot(xb, wub_ref[...])
        h = gt / (1.0 + jnp.exp(-gt)) * up
        _to_token_tiles(ybuf_ref.at[slot], _dot(_bf16(h), wdb_ref[...]))

        issue_scatter(i, slot)

        @pl.when(i == n_tiles - 1)
        def _():
            @pl.when(i >= 1)
            def _():
                wait_scatter(i - 1, 1 - slot)
            wait_scatter(i, slot)


def _experts(tok, te, nv, x1t, w_gate, w_up, w_down, layer, max_tiles):
    d, f = w_gate.shape[-2:]
    rows = MOE_TM * TOKEN_TILE_ROWS
    any_spec = pl.BlockSpec(memory_space=pl.ANY)
    expert = lambda i, tok, te, nv, lay: (lay[0], te[i], 0, 0)
    return pl.pallas_call(
        functools.partial(_experts_kernel, max_tiles=max_tiles),
        grid_spec=pltpu.PrefetchScalarGridSpec(
            num_scalar_prefetch=4,
            grid=(max_tiles,),
            in_specs=[any_spec,
                      pl.BlockSpec((None, None, d, f), expert),
                      pl.BlockSpec((None, None, d, f), expert),
                      pl.BlockSpec((None, None, f, d), expert)],
            out_specs=any_spec,
            scratch_shapes=[pltpu.VMEM((2, rows, LANES), jnp.float32),
                            pltpu.VMEM((2, rows, LANES), jnp.float32),
                            pltpu.VMEM((d, f), jnp.bfloat16),
                            pltpu.VMEM((d, f), jnp.bfloat16),
                            pltpu.VMEM((f, d), jnp.bfloat16),
                            pltpu.SemaphoreType.DMA((2,)),
                            pltpu.SemaphoreType.DMA((2,))]),
        out_shape=jax.ShapeDtypeStruct((2 * x1t.shape[0], LANES), jnp.float32),
        compiler_params=pltpu.CompilerParams(dimension_semantics=("arbitrary",),
                                             vmem_limit_bytes=VMEM_LIMIT),
        name="moe_experts",
    )(tok, te, nv, jnp.full((1,), layer, jnp.int32), x1t, w_gate, w_up, w_down)


def _combine_kernel(x_ref, y0_ref, y1_ref, gcol_ref, lng_ref, lnb_ref, o_ref, *, alpha):
    tm = x_ref.shape[0]
    lane = lax.broadcasted_iota(jnp.int32, (1, LANES), 1)
    gc = gcol_ref[...]
    g0 = jnp.sum(jnp.where(lane == 0, gc, 0.0), axis=-1, keepdims=True)
    g1 = jnp.sum(jnp.where(lane == 1, gc, 0.0), axis=-1, keepdims=True)
    y = g0 * _from_token_tiles(y0_ref, tm) + g1 * _from_token_tiles(y1_ref, tm)
    o_ref[...] = _layer_norm(alpha * x_ref[...] + y, lng_ref[...], lnb_ref[...])


def _combine(x1, y, gcol, ln_g, ln_b, alpha):
    n, d = x1.shape
    const = lambda i: (0, 0)
    rows = ROW_TILE * TOKEN_TILE_ROWS
    return pl.pallas_call(
        functools.partial(_combine_kernel, alpha=alpha),
        grid=(n // ROW_TILE,),
        in_specs=[pl.BlockSpec((ROW_TILE, d), lambda i: (i, 0)),
                  pl.BlockSpec((rows, LANES), lambda i: (i, 0)),
                  pl.BlockSpec((rows, LANES), lambda i: (i + n // ROW_TILE, 0)),
                  pl.BlockSpec((ROW_TILE, LANES), lambda i: (i, 0)),
                  pl.BlockSpec((1, d), const), pl.BlockSpec((1, d), const)],
        out_specs=pl.BlockSpec((ROW_TILE, d), lambda i: (i, 0)),
        out_shape=jax.ShapeDtypeStruct((n, d), jnp.float32),
        compiler_params=pltpu.CompilerParams(dimension_semantics=("arbitrary",),
                                             vmem_limit_bytes=VMEM_LIMIT),
        name="moe_combine_ln",
    )(x1, y, y, gcol, ln_g, ln_b)


def _moe(x1, x1t, meta, gcol, cnt, w_gate, w_up, w_down, layer, ln_g, ln_b, alpha):
    n = x1.shape[0]
    max_tiles = (2 * n + N_EXPERTS * (MOE_TM - 1)) // MOE_TM
    counts = cnt[:, 0].astype(jnp.int32)
    e0, e1, r0, r1 = meta[0], meta[1], meta[2], meta[3]
    tok, te, nv = _plan(counts, e0, e1, r0, r1, max_tiles)
    y = _experts(tok, te, nv, x1t, w_gate, w_up, w_down, layer, max_tiles)
    return _combine(x1, y, gcol, ln_g, ln_b, alpha)


def _t5_bucket(dist):
    max_exact = N_BUCKETS // 2
    d = jnp.maximum(dist, 0)
    large = max_exact + (jnp.log(jnp.maximum(d, 1).astype(jnp.float32) / max_exact)
                         / math.log(MAX_DISTANCE / max_exact) * (N_BUCKETS - max_exact)).astype(jnp.int32)
    large = jnp.minimum(large, N_BUCKETS - 1)
    return jnp.where(d < max_exact, d, large)


def _tile_bias_kernel(table_ref, o_ref):
    slot = pl.program_id(0)
    qi = lax.broadcasted_iota(jnp.int32, (BLOCK, 2 * BLOCK), 0)
    kj = lax.broadcasted_iota(jnp.int32, (BLOCK, 2 * BLOCK), 1)
    steps = qi + BLOCK - kj

    def per_step(s, acc):
        return jnp.where(steps == s, table_ref[slot, s], acc)
    o_ref[...] = lax.fori_loop(0, BLOCK + 1, per_step,
                               jnp.zeros((BLOCK, 2 * BLOCK), jnp.float32))


def _tile_bias(rel_bias):
    dist = jnp.arange(BLOCK + 1)
    rows = jnp.stack([rel_bias[_t5_bucket(dist * dilation)] for _, dilation in DIL_PATTERNS])
    n_br, _, heads = rows.shape
    table = rows.astype(jnp.float32).transpose(0, 2, 1).reshape(n_br * heads, BLOCK + 1)
    return pl.pallas_call(
        _tile_bias_kernel,
        grid=(n_br * heads,),
        in_specs=[pl.BlockSpec(memory_space=pltpu.SMEM)],
        out_specs=pl.BlockSpec((None, BLOCK, 2 * BLOCK), lambda s: (s, 0, 0)),
        out_shape=jax.ShapeDtypeStruct((n_br * heads, BLOCK, 2 * BLOCK), jnp.float32),
        compiler_params=pltpu.CompilerParams(dimension_semantics=("arbitrary",)),
        name="tile_bias",
    )(table)


def kernel(x, w_in, g_sb, g_dil, w_out, ln1_g, ln1_b, ln2_g, ln2_b, rel_bias,
           w_router, b_router, w_gate, w_up, w_down):
    batch, seq, d = x.shape
    depth = w_in.shape[0]
    alpha = (2.0 * depth) ** 0.25
    n = batch * seq
    x2d = x.reshape(n, d)

    tile_bias = _tile_bias(rel_bias)
    wrt =w_router.T.astype(jnp.float32)
    wrt_hi = _bf16(wrt)
    wrt_lo = _bf16(wrt - wrt_hi.astype(jnp.float32))
    br = b_router.astype(jnp.float32).reshape(N_EXPERTS, 1)
    sb_cols = 3 * D_SB

    for l in range(depth):
        w_in_l = _bf16(w_in[l])
        proj_sb = _in_proj(x2d, w_in_l[:, :sb_cols], jnp.bfloat16, N_PAIRS_SB, "in_proj_sb")
        proj_dl = _in_proj(x2d, w_in_l[:, sb_cols:], jnp.float32, N_PAIRS_DIL, "in_proj_dil")
        y_sb = _sb_attention(proj_sb, g_sb[l].reshape(N_PAIRS_SB, 1, LANES), batch, seq)
        y_dl = _dil_attention(proj_dl, g_dil[l].reshape(N_PAIRS_DIL, 1, LANES),
                              tile_bias, batch, seq)
        x1, x1t, meta, gcol, cnt = _out_proj(x2d, y_sb, y_dl, _bf16(w_out[l]),
                                             ln1_g[l].reshape(1, d), ln1_b[l].reshape(1, d),
                                             wrt_hi, wrt_lo, br, alpha)
        x2d = _moe(x1, x1t, meta, gcol, cnt, w_gate, w_up, w_down, l,
                   ln2_g[l].reshape(1, d), ln2_b[l].reshape(1, d), alpha)
    return x2d.reshape(batch, seq, d)
```

```python
import functools
import math

import jax
import jax.numpy as jnp
from jax import lax
from jax.experimental import pallas as pl
from jax.experimental.pallas import tpu as pltpu

D_MODEL = 1024
HEAD_DIM = 64
N_HEADS_SB = 8
N_HEADS_DIL = 8
D_SB = N_HEADS_SB * HEAD_DIM
D_DIL = N_HEADS_DIL * HEAD_DIM
DIL_PATTERNS = ((128, 1), (512, 4), (2048, 16))
BLOCK = 128
N_BUCKETS = 32
MAX_DISTANCE = 2048
N_EXPERTS = 16
N_GROUPS = 4
EXPERTS_PER_GROUP = N_EXPERTS // N_GROUPS
D_FF_EXPERT = 1024
LN_EPS = 1e-5
NEG_INF = -1e30

LANES = 128
N_PAIRS_SB = D_SB // LANES
N_PAIRS_DIL = D_DIL // LANES
VMEM_LIMIT = 56 * 1024 * 1024

ROW_TILE = 512
SB_TQ = 256
SB_TK = 256
SB_TILES_PER_STEP = 2
SB_SKIP_BOUND = -110.0
DIL_SUPER = 2048
DIL_UNITS = DIL_SUPER // BLOCK
TOKEN_TILE_ROWS = D_MODEL // LANES
MOE_TM = 256
PERM_TP = 512

_NT = (((1,), (1,)), ((), ()))


def _bf16(x):
    return x.astype(jnp.bfloat16)


def _split_bf16(x):
    hi = _bf16(x)
    lo = _bf16(x - hi.astype(jnp.float32))
    return hi, lo


def _dot(a, b):
    return jnp.dot(a, b, preferred_element_type=jnp.float32)


def _dot_nt(a, b):
    return lax.dot_general(a, b, _NT, preferred_element_type=jnp.float32)


def _head_rms_gain(o, g):
    r = lax.broadcasted_iota(jnp.int32, (LANES, LANES), 0) // HEAD_DIM
    c = lax.broadcasted_iota(jnp.int32, (LANES, LANES), 1) // HEAD_DIM
    same_head = jnp.where(r == c, 1.0 / HEAD_DIM, 0.0).astype(jnp.bfloat16)
    hi, lo = _split_bf16(o * o)
    ms = _dot(hi, same_head) + _dot(lo, same_head)
    return o * lax.rsqrt(ms + 1e-6) * g


def _layer_norm(x, g, b):
    mu = jnp.mean(x, axis=-1, keepdims=True)
    xc = x - mu
    var = jnp.mean(xc * xc, axis=-1, keepdims=True)
    return xc * lax.rsqrt(var + LN_EPS) * g + b


def _in_proj_kernel(x_ref, w_ref, sb_ref, dl_ref):
    x = _bf16(x_ref[...])
    scale = HEAD_DIM ** -0.5
    col = 0
    for o_ref, n_q in ((sb_ref, N_PAIRS_SB), (dl_ref, N_PAIRS_DIL)):
        for j in range(0, o_ref.shape[0], 2):
            res = _dot(x, w_ref[:, col:col + 2 * LANES])
            col += 2 * LANES
            for jj in range(2):
                blk = res[:, jj * LANES:(jj + 1) * LANES]
                if j + jj < n_q:
                    blk = blk * scale
                o_ref[j + jj] = blk.astype(o_ref.dtype)


def _in_proj(x2d, w):
    n, d = x2d.shape
    slabs = lambda k: pl.BlockSpec((k, ROW_TILE, LANES), lambda i: (0, i, 0))
    return pl.pallas_call(
        _in_proj_kernel,
        grid=(n // ROW_TILE,),
        in_specs=[pl.BlockSpec((ROW_TILE, d), lambda i: (i, 0)),
                  pl.BlockSpec((d, w.shape[1]), lambda i: (0, 0))],
        out_specs=[slabs(3 * N_PAIRS_SB), slabs(3 * N_PAIRS_DIL)],
        out_shape=[jax.ShapeDtypeStruct((3 * N_PAIRS_SB, n, LANES), jnp.bfloat16),
                   jax.ShapeDtypeStruct((3 * N_PAIRS_DIL, n, LANES), jnp.float32)],
        compiler_params=pltpu.CompilerParams(dimension_semantics=("arbitrary",),
                                             vmem_limit_bytes=VMEM_LIMIT),
        name="in_proj",
    )(x2d, w)


def _sb_kernel(q_ref, k_ref, v_ref, g_ref, o_ref, carry_ref, acc_ref):
    step = pl.program_id(2)
    lane = lax.broadcasted_iota(jnp.int32, (1, LANES), 1)
    first_head = lane < HEAD_DIM

    jj = lax.broadcasted_iota(jnp.int32, (SB_TK, SB_TK + LANES), 0)
    ss = lax.broadcasted_iota(jnp.int32, (SB_TK, SB_TK + LANES), 1)
    suffix = jnp.where((jj > ss) | (ss >= SB_TK), 1.0, 0.0).astype(jnp.bfloat16)

    tq_i = lax.broadcasted_iota(jnp.int32, (2 * SB_TQ, SB_TK), 0) % SB_TQ
    ts_i = lax.broadcasted_iota(jnp.int32, (2 * SB_TQ, SB_TK), 1)
    causal = ts_i < tq_i

    def block(q2, kb, diag, carry, acc, weight=None):
        off = pl.multiple_of(kb * SB_TK, SB_TK)
        kblk = k_ref[pl.ds(off, SB_TK), :]
        vblk = v_ref[pl.ds(off, SB_TK), :]
        z = _dot_nt(q2, kblk)
        sp = jnp.log(1.0 + jnp.exp(-jnp.abs(z)))
        log_beta = jnp.minimum(z, 0.0) - sp
        log_1m = log_beta - z
        if diag:
            log_1m = jnp.where(causal, log_1m, 0.0)
        sfx = _dot(_bf16(log_1m), suffix)
        between = sfx[:, :SB_TK] + jnp.concatenate([carry] * (SB_TK // LANES), axis=1)
        a = jnp.exp(log_beta + between)
        if diag:
            a = jnp.where(causal, a, 0.0)
        out = _dot(_bf16(a), vblk)
        tot = sfx[:, SB_TK:]
        if weight is not None:
            out = out * weight
            tot = tot * weight
        return carry + tot, acc + out

    zeros = jnp.zeros((2 * SB_TQ, LANES), jnp.float32)
    tiles = []
    for t in range(SB_TILES_PER_STEP):
        i = step * SB_TILES_PER_STEP + t
        q = q_ref[t * SB_TQ:(t + 1) * SB_TQ, :]
        zero = jnp.zeros_like(q)
        q2 = jnp.concatenate([jnp.where(first_head, q, zero), jnp.where(first_head, zero, q)], axis=0)
        carry, acc = block(q2, i, True, zeros, zeros)
        has_prev = jnp.where(i >= 1, 1.0, 0.0)
        carry, acc = block(q2, jnp.maximum(i - 1, 0), False, carry, acc, weight=has_prev)
        carry_ref[t] = carry
        acc_ref[t] = acc
        tiles.append((i, q2, jnp.max(carry)))

    for t, (i, q2, mx0) in enumerate(tiles):
        def cond(state):
            kb, mx = state
            return jnp.logical_and(kb >= 0, mx > SB_SKIP_BOUND)

        def body(state, t=t, q2=q2):
            kb, _ = state
            c, a = block(q2, kb, False, carry_ref[t], acc_ref[t])
            carry_ref[t] = c
            acc_ref[t] = a
            return kb - 1, jnp.max(c)

        lax.while_loop(cond, body, (i - 2, mx0))

        o = jnp.where(first_head, acc_ref[t, :SB_TQ, :], acc_ref[t, SB_TQ:, :])
        o_ref[t * SB_TQ:(t + 1) * SB_TQ, :] = _head_rms_gain(o, g_ref[...]).astype(o_ref.dtype)


def _sb_attention(proj, g, batch, seq):
    n = batch * seq
    rows = SB_TQ * SB_TILES_PER_STEP
    nq = seq // rows
    return pl.pallas_call(
        _sb_kernel,
        grid=(batch, N_PAIRS_SB, nq),
        in_specs=[
            pl.BlockSpec((None, rows, LANES), lambda b, p, i: (p, b * nq + i, 0)),
            pl.BlockSpec((None, seq, LANES), lambda b, p, i: (N_PAIRS_SB + p, b, 0)),
            pl.BlockSpec((None, seq, LANES), lambda b, p, i: (2 * N_PAIRS_SB + p, b, 0)),
            pl.BlockSpec((None, 1, LANES), lambda b, p, i: (p, 0, 0)),
        ],
        out_specs=pl.BlockSpec((None, rows, LANES), lambda b, p, i: (p, b * nq + i, 0)),
        out_shape=jax.ShapeDtypeStruct((N_PAIRS_SB, n, LANES), jnp.bfloat16),
        scratch_shapes=[pltpu.VMEM((SB_TILES_PER_STEP, 2 * SB_TQ, LANES), jnp.float32),
                        pltpu.VMEM((SB_TILES_PER_STEP, 2 * SB_TQ, LANES), jnp.float32)],
        compiler_params=pltpu.CompilerParams(
            dimension_semantics=("arbitrary", "arbitrary", "arbitrary"),
            vmem_limit_bytes=VMEM_LIMIT),
        name="sb_attention",
    )(proj, proj, proj, g)


def _dil_kernel(tbias_ref, q_ref, k_ref, v_ref, g_ref, o_ref,
                bias_ref, obr_ref, lse_ref):
    b = pl.program_id(0)
    p = pl.program_id(1)
    i = pl.program_id(2)
    n_br = len(DIL_PATTERNS)

    qi = lax.broadcasted_iota(jnp.int32, (BLOCK, 2 * BLOCK), 0)
    kj = lax.broadcasted_iota(jnp.int32, (BLOCK, 2 * BLOCK), 1)
    steps = qi + BLOCK - kj

    assert all(window // dilation == BLOCK for window, dilation in DIL_PATTERNS)
    valid = (steps >= 0) & (steps <= BLOCK)

    @pl.when((b == 0) & (p == 0) & (i == 0))
    def _():
        n_tiles = n_br * N_HEADS_DIL

        def per_tile(s, _):
            tile = tbias_ref[s]
            rows = pl.ds(pl.multiple_of(s * BLOCK, BLOCK), BLOCK)
            bias_ref[rows, :] = jnp.where(valid, tile, NEG_INF)
            rows = pl.ds(pl.multiple_of((n_tiles + s) * BLOCK, BLOCK), BLOCK)
            bias_ref[rows, :] = jnp.where(valid & (kj >= BLOCK), tile, NEG_INF)
            return 0
        lax.fori_loop(0, n_br * N_HEADS_DIL, per_tile, 0)

    lane = lax.broadcasted_iota(jnp.int32, (1, LANES), 1)
    first_head = lane < HEAD_DIM
    t0 = i * DIL_SUPER

    def strided(ref, start, dilation):
        if dilation == 1:
            return ref[pl.ds(start, BLOCK), :]
        return ref[pl.ds(start, BLOCK, stride=dilation), :]

    for gidx, (window, dilation) in enumerate(DIL_PATTERNS):
        units_per_residue = DIL_UNITS // dilation
        shift = units_per_residue.bit_length() - 1

        def unit(u, _, gidx=gidx, dilation=dilation, units_per_residue=units_per_residue, shift=shift):
            r = lax.shift_right_logical(u, shift)
            cb = u & (units_per_residue - 1)
            sq = r + cb * (dilation * BLOCK)
            cur = t0 + sq
            prev = cur - dilation * BLOCK
            prev_ok = prev >= 0
            prev_c = jnp.where(prev_ok, prev, cur)
            no_prev = jnp.where(prev_ok, 0, 1)

            qb = _bf16(strided(q_ref, sq, dilation))
            kk = _bf16(jnp.concatenate([strided(k_ref, prev_c, dilation),
                                        strided(k_ref, cur, dilation)], axis=0))
            vv = _bf16(jnp.concatenate([strided(v_ref, prev_c, dilation),
                                        strided(v_ref, cur, dilation)], axis=0))
            zero = jnp.zeros_like(qb)
            q2 = jnp.concatenate([jnp.where(first_head, qb, zero),
                                  jnp.where(first_head, zero, qb)], axis=0)
            z = _dot_nt(q2, kk)
            slot = no_prev * (n_br * N_HEADS_DIL) + gidx * N_HEADS_DIL + 2 * p
            lg = z + bias_ref[pl.ds(pl.multiple_of(slot * BLOCK, 2 * BLOCK), 2 * BLOCK), :]
            m = jnp.max(lg, axis=-1, keepdims=True)
            pe = jnp.exp(lg - m)
            den = jnp.sum(pe, axis=-1, keepdims=True)
            out = _dot(_bf16(pe), vv) / den
            lse = m + jnp.log(den)
            o_tile = jnp.where(first_head, out[:BLOCK], out[BLOCK:])
            l_tile = jnp.where(first_head, lse[:BLOCK], lse[BLOCK:])
            if dilation == 1:
                idx = pl.ds(sq, BLOCK)
            else:
                idx = pl.ds(sq, BLOCK, stride=dilation)
            obr_ref.at[gidx][idx, :] = o_tile
            lse_ref.at[gidx][idx, :] = l_tile
            return 0

        lax.fori_loop(0, DIL_UNITS, unit, 0, unroll=8)

    chunk = 256

    def combine(c, _):
        rows = pl.ds(pl.multiple_of(c * chunk, chunk), chunk)
        ls = [lse_ref[gidx, rows, :] for gidx in range(n_br)]
        m = functools.reduce(jnp.maximum, ls)
        es = [jnp.exp(l - m) for l in ls]
        num = sum(e * obr_ref[gidx, rows, :] for gidx, e in enumerate(es))
        o = num / sum(es)
        o_ref[rows, :] = _head_rms_gain(o, g_ref[...]).astype(o_ref.dtype)
        return 0

    lax.fori_loop(0, DIL_SUPER // chunk, combine, 0)


def _dil_attention(proj, g, tile_bias, batch, seq):
    n = batch * seq
    ns = seq // DIL_SUPER
    n_br = len(DIL_PATTERNS)
    return pl.pallas_call(
        _dil_kernel,
        grid=(batch, N_PAIRS_DIL, ns),
        in_specs=[
            pl.BlockSpec((n_br * N_HEADS_DIL, BLOCK, 2 * BLOCK), lambda b, p, i: (0, 0, 0)),
            pl.BlockSpec((None, DIL_SUPER, LANES), lambda b, p, i: (p, b * ns + i, 0)),
            pl.BlockSpec((None, seq, LANES), lambda b, p, i: (N_PAIRS_DIL + p, b, 0)),
            pl.BlockSpec((None, seq, LANES), lambda b, p, i: (2 * N_PAIRS_DIL + p, b, 0)),
            pl.BlockSpec((None, 1, LANES), lambda b, p, i: (p, 0, 0)),
        ],
        out_specs=pl.BlockSpec((None, DIL_SUPER, LANES), lambda b, p, i: (p, b * ns + i, 0)),
        out_shape=jax.ShapeDtypeStruct((N_PAIRS_DIL, n, LANES), jnp.bfloat16),
        scratch_shapes=[pltpu.VMEM((2 * n_br * N_HEADS_DIL * BLOCK, 2 * BLOCK), jnp.float32),
                        pltpu.VMEM((n_br, DIL_SUPER, LANES), jnp.float32),
                        pltpu.VMEM((n_br, DIL_SUPER, LANES), jnp.float32)],
        compiler_params=pltpu.CompilerParams(
            dimension_semantics=("arbitrary", "arbitrary", "arbitrary"),
            vmem_limit_bytes=VMEM_LIMIT),
        name="dilated_attention",
    )(tile_bias, proj, proj, proj, g)


def _router(x1, wrt_hi, wrt_lo, br):
    tm = x1.shape[0]
    xh, xl = _split_bf16(x1)
    lt = _dot_nt(wrt_hi, xh) + _dot_nt(wrt_hi, xl) + _dot_nt(wrt_lo, xh) + br
    rows = [lt[e:e + 1, :] for e in range(N_EXPERTS)]
    m = functools.reduce(jnp.maximum, rows)
    ex = [jnp.exp(r - m) for r in rows]
    tot = functools.reduce(lambda a, c: a + c, ex)
    probs = [e / tot for e in ex]

    keep, score = [], []
    for gi in range(N_GROUPS):
        members = range(gi * EXPERTS_PER_GROUP, (gi + 1) * EXPERTS_PER_GROUP)
        s = None
        for e in members:
            rank = None
            for o in members:
                if o == e:
                    continue
                beats = (probs[o] > probs[e]) | ((probs[o] == probs[e]) & (o < e))
                beats = beats.astype(jnp.int32)
                rank = beats if rank is None else rank + beats
            k = rank < 2
            keep.append(k)
            contrib = jnp.where(k, probs[e], 0.0)
            s = contrib if s is None else s + contrib
        score.append(s)

    sel, gates = [], []
    for gi in range(N_GROUPS):
        chosen = None
        for o in range(N_GROUPS):
            if o == gi:
                continue
            c = (score[gi] > score[o]) if o < gi else (score[gi] >= score[o])
            chosen = c if chosen is None else (chosen & c)
        for e in range(gi * EXPERTS_PER_GROUP, (gi + 1) * EXPERTS_PER_GROUP):
            s = keep[e] & chosen
            sel.append(s)
            gates.append(jnp.where(s, probs[e] / score[gi], 0.0))

    idx0 = functools.reduce(jnp.minimum, [jnp.where(sel[e], e, N_EXPERTS) for e in range(N_EXPERTS)])
    idx1 = functools.reduce(jnp.maximum, [jnp.where(sel[e], e, -1) for e in range(N_EXPERTS)])
    g0 = sum(jnp.where(idx0 == e, gates[e], 0.0) for e in range(N_EXPERTS))
    g1 = sum(jnp.where(idx1 == e, gates[e], 0.0) for e in range(N_EXPERTS))

    row_id = lax.broadcasted_iota(jnp.int32, (N_EXPERTS, tm), 0)
    selmat = jnp.zeros((N_EXPERTS, tm), jnp.float32)
    for e in range(N_EXPERTS):
        selmat = jnp.where((row_id == e) & sel[e], 1.0, selmat)
    return idx0, idx1, g0, g1, selmat


def _to_token_tiles(ref, x):
    rows = x.shape[0]
    for s in range(TOKEN_TILE_ROWS):
        ref[pl.ds(s, rows, stride=TOKEN_TILE_ROWS), :] = x[:, s * LANES:(s + 1) * LANES]


def _from_token_tiles(ref, rows):
    return jnp.concatenate([ref[pl.ds(s, rows, stride=TOKEN_TILE_ROWS), :]
                            for s in range(TOKEN_TILE_ROWS)], axis=1)


def _out_proj_kernel(x_ref, ysb_ref, ydl_ref, w_ref, lng_ref, lnb_ref,
                     wrh_ref, wrl_ref, br_ref,
                     x1_ref, x1t_ref, meta_ref, gcol_ref, cnt_ref, carry_ref, *, alpha):
    i = pl.program_id(0)
    tm = x_ref.shape[0]

    @pl.when(i == 0)
    def _():
        carry_ref[...] = jnp.zeros_like(carry_ref)

    y = jnp.concatenate([ysb_ref[j] for j in range(N_PAIRS_SB)]
                        + [ydl_ref[j] for j in range(N_PAIRS_DIL)], axis=1)
    h = _dot(y, w_ref[...])
    x1 = _layer_norm(alpha * x_ref[...] + h, lng_ref[...], lnb_ref[...])
    x1_ref[...] = x1
    _to_token_tiles(x1t_ref, x1)

    idx0, idx1, g0, g1, selmat = _router(x1, wrh_ref[...], wrl_ref[...], br_ref[...])

    jj = lax.broadcasted_iota(jnp.int32, (tm, tm + LANES), 0)
    ss = lax.broadcasted_iota(jnp.int32, (tm, tm + LANES), 1)
    before = jnp.where((jj < ss) | (ss >= tm), 1.0, 0.0).astype(jnp.bfloat16)
    pc = _dot(_bf16(selmat), before)
    carry = carry_ref[...]
    rankmat = pc[:, :tm] + jnp.concatenate([carry] * (tm // LANES), axis=1)
    carry = carry + pc[:, tm:]
    carry_ref[...] = carry
    cnt_ref[...] = carry

    rank0 = sum(jnp.where(idx0 == e, rankmat[e:e + 1, :], 0.0) for e in range(N_EXPERTS))
    rank1 = sum(jnp.where(idx1 == e, rankmat[e:e + 1, :], 0.0) for e in range(N_EXPERTS))

    row8 = lax.broadcasted_iota(jnp.int32, (8, tm), 0)
    meta = jnp.zeros((8, tm), jnp.int32)
    for r, v in enumerate((idx0, idx1, rank0.astype(jnp.int32), rank1.astype(jnp.int32))):
        meta = jnp.where(row8 == r, v, meta)
    meta_ref[...] = meta

    rowl = lax.broadcasted_iota(jnp.int32, (LANES, tm), 0)
    gt = jnp.where(rowl == 0, g0, jnp.where(rowl == 1, g1, 0.0))
    gcol_ref[...] = gt.T


def _out_proj(x2d, y_sb, y_dl, w_out, ln_g, ln_b, wrt_hi, wrt_lo, br, alpha):
    n, d = x2d.shape
    const = lambda i: (0, 0)
    return pl.pallas_call(
        functools.partial(_out_proj_kernel, alpha=alpha),
        grid=(n // ROW_TILE,),
        in_specs=[
            pl.BlockSpec((ROW_TILE, d), lambda i: (i, 0)),
            pl.BlockSpec((N_PAIRS_SB, ROW_TILE, LANES), lambda i: (0, i, 0)),
            pl.BlockSpec((N_PAIRS_DIL, ROW_TILE, LANES), lambda i: (0, i, 0)),
            pl.BlockSpec((d, d), const),
            pl.BlockSpec((1, d), const),
            pl.BlockSpec((1, d), const),
            pl.BlockSpec((N_EXPERTS, d), const),
            pl.BlockSpec((N_EXPERTS, d), const),
            pl.BlockSpec((N_EXPERTS, 1), const),
        ],
        out_specs=[pl.BlockSpec((ROW_TILE, d), lambda i: (i, 0)),
                   pl.BlockSpec((ROW_TILE * TOKEN_TILE_ROWS, LANES), lambda i: (i, 0)),
                   pl.BlockSpec((8, ROW_TILE), lambda i: (0, i)),
                   pl.BlockSpec((ROW_TILE, LANES), lambda i: (i, 0)),
                   pl.BlockSpec((N_EXPERTS, LANES), const)],
        out_shape=[jax.ShapeDtypeStruct((n, d), jnp.float32),
                   jax.ShapeDtypeStruct((n * TOKEN_TILE_ROWS, LANES), jnp.float32),
                   jax.ShapeDtypeStruct((8, n), jnp.int32),
                   jax.ShapeDtypeStruct((n, LANES), jnp.float32),
                   jax.ShapeDtypeStruct((N_EXPERTS, LANES), jnp.float32)],
        scratch_shapes=[pltpu.VMEM((N_EXPERTS, LANES), jnp.float32)],
        compiler_params=pltpu.CompilerParams(dimension_semantics=("arbitrary",),
                                             vmem_limit_bytes=VMEM_LIMIT),
        name="out_proj_ln_router",
    )(x2d, y_sb, y_dl, w_out, ln_g, ln_b, wrt_hi, wrt_lo, br)


def _segment_offsets(cnt_ref, off_ref):
    acc = jnp.int32(0)
    padded = []
    for e in range(N_EXPERTS):
        off_ref[e] = acc
        pc = lax.div(cnt_ref[e] + (MOE_TM - 1), MOE_TM) * MOE_TM
        padded.append(pc)
        acc = acc + pc
    return padded, acc


def _row_copy(src_ref, src_row, dst_ref, dst_row, sem):
    def tile(row):
        if isinstance(row, int):
            return pl.ds(row * TOKEN_TILE_ROWS, TOKEN_TILE_ROWS)
        return pl.ds(pl.multiple_of(row * TOKEN_TILE_ROWS, TOKEN_TILE_ROWS), TOKEN_TILE_ROWS)
    return pltpu.make_async_copy(src_ref.at[tile(src_row), :], dst_ref.at[tile(dst_row), :], sem)


def _plan_kernel(cnt_ref, e0_ref, e1_ref, r0_ref, r1_ref, tok_ref, te_ref, nv_ref, off_ref,
                 *, max_tiles, n_tokens):
    i = pl.program_id(0)
    tp = e0_ref.shape[0]
    padded, total = _segment_offsets(cnt_ref, off_ref)

    @pl.when(i == 0)
    def _():
        def clear(k, c):
            tok_ref[k] = 0
            return c
        for e in range(N_EXPERTS):
            lax.fori_loop(off_ref[e] + cnt_ref[e], off_ref[e] + padded[e], clear, 0)
        lax.fori_loop(total, max_tiles * MOE_TM, clear, 0)

        n_tiles = lax.div(total, MOE_TM)
        for e in range(N_EXPERTS):
            first = lax.div(off_ref[e], MOE_TM)
            count = lax.div(padded[e], MOE_TM)

            def fill(k, c, first=first, e=e):
                te_ref[first + k] = e
                nv_ref[first + k] = jnp.minimum(cnt_ref[e] - k * MOE_TM, MOE_TM)
                return c
            lax.fori_loop(0, count, fill, 0)
        last = te_ref[n_tiles - 1]

        def fill_tail(k, c):
            te_ref[k] = last
            nv_ref[k] = 0
            return c
        lax.fori_loop(n_tiles, max_tiles, fill_tail, 0)
        te_ref[max_tiles] = n_tiles

    def place(j, c):
        token = i * tp + j
        tok_ref[off_ref[e0_ref[j]] + r0_ref[j]] = token
        tok_ref[off_ref[e1_ref[j]] + r1_ref[j]] = token + n_tokens
        return c
    lax.fori_loop(0, tp, place, 0, unroll=8)


def _plan(counts, e0, e1, r0, r1, max_tiles):
    n = e0.shape[0]
    smem = pl.BlockSpec(memory_space=pltpu.SMEM)
    vec = pl.BlockSpec((PERM_TP,), lambda i: (i,), memory_space=pltpu.SMEM)
    return pl.pallas_call(
        functools.partial(_plan_kernel, max_tiles=max_tiles, n_tokens=n),
        grid=(n // PERM_TP,),
        in_specs=[smem, vec, vec, vec, vec],
        out_specs=[smem, smem, smem],
        out_shape=[jax.ShapeDtypeStruct((max_tiles * MOE_TM,), jnp.int32),
                   jax.ShapeDtypeStruct((max_tiles + 1,), jnp.int32),
                   jax.ShapeDtypeStruct((max_tiles,), jnp.int32)],
        scratch_shapes=[pltpu.SMEM((N_EXPERTS,), jnp.int32)],
        compiler_params=pltpu.CompilerParams(dimension_semantics=("arbitrary",)),
        name="moe_plan",
    )(counts, e0, e1, r0, r1)


def _experts_kernel(tok_ref, te_ref, nv_ref, layer_ref, x_ref, wg_ref, wu_ref, wd_ref, y_ref,
                    xbuf_ref, ybuf_ref, wgb_ref, wub_ref, wdb_ref, gsem, ssem, *, max_tiles):
    i = pl.program_id(0)
    n_tiles = te_ref[max_tiles]
    slot = i & 1
    tile_rows = MOE_TM * TOKEN_TILE_ROWS
    n_tokens = x_ref.shape[0] // TOKEN_TILE_ROWS

    def gather(tile, s):
        base = tile * MOE_TM

        def body(j, c):
            code = tok_ref[base + j]
            token = jnp.where(code >= n_tokens, code - n_tokens, code)
            _row_copy(x_ref, token, xbuf_ref.at[s], j, gsem.at[s]).start()
            return c
        lax.fori_loop(0, MOE_TM, body, 0, unroll=8)

    def scatter_copy(tile, s, j):
        return _row_copy(ybuf_ref.at[s], j, y_ref, tok_ref[tile * MOE_TM + j], ssem.at[s])

    def issue_scatter(tile, s):
        def body(j, c):
            scatter_copy(tile, s, j).start()
            return c
        full = nv_ref[tile] == MOE_TM

        @pl.when(full)
        def _():
            lax.fori_loop(0, MOE_TM, body, 0, unroll=8)

        @pl.when(jnp.logical_not(full))
        def _():
            lax.fori_loop(0, nv_ref[tile], body, 0)

    def wait_scatter(tile, s):
        def body(j, c):
            scatter_copy(tile, s, j).wait()
            return c
        full = nv_ref[tile] == MOE_TM

        @pl.when(full)
        def _():
            pltpu.make_async_copy(ybuf_ref.at[s], y_ref.at[pl.ds(0, tile_rows), :],
                                  ssem.at[s]).wait()

        @pl.when(jnp.logical_not(full))
        def _():
            lax.fori_loop(0, nv_ref[tile], body, 0)

    @pl.when(i == 0)
    def _():
        gather(0, 0)

    @pl.when(i + 1 < n_tiles)
    def _():
        gather(i + 1, 1 - slot)

    @pl.when(i < n_tiles)
    def _():
        pltpu.make_async_copy(x_ref.at[pl.ds(0, tile_rows), :], xbuf_ref.at[slot],
                              gsem.at[slot]).wait()

        @pl.when(i >= 2)
        def _():
            wait_scatter(i - 2, slot)

        @pl.when((i == 0) | (te_ref[i] != te_ref[jnp.maximum(i - 1, 0)]))
        def _():
            wgb_ref[...] = _bf16(wg_ref[...])
            wub_ref[...] = _bf16(wu_ref[...])
            wdb_ref[...] = _bf16(wd_ref[...])

        xb = _bf16(_from_token_tiles(xbuf_ref.at[slot], MOE_TM))
        gt = _dot(xb, wgb_ref[...])
        up = _dot(xb, wub_ref[...])
        h = gt / (1.0 + jnp.exp(-gt)) * up
        _to_token_tiles(ybuf_ref.at[slot], _dot(_bf16(h), wdb_ref[...]))

        issue_scatter(i, slot)

        @pl.when(i == n_tiles - 1)
        def _():
            @pl.when(i >= 1)
            def _():
                wait_scatter(i - 1, 1 - slot)
            wait_scatter(i, slot)


def _experts(tok, te, nv, x1t, w_gate, w_up, w_down, layer, max_tiles):
    d, f = w_gate.shape[-2:]
    rows = MOE_TM * TOKEN_TILE_ROWS
    any_spec = pl.BlockSpec(memory_space=pl.ANY)
    expert = lambda i, tok, te, nv, lay: (lay[0], te[i], 0, 0)
    return pl.pallas_call(
        functools.partial(_experts_kernel, max_tiles=max_tiles),
        grid_spec=pltpu.PrefetchScalarGridSpec(
            num_scalar_prefetch=4,
            grid=(max_tiles,),
            in_specs=[any_spec,
                      pl.BlockSpec((None, None, d, f), expert),
                      pl.BlockSpec((None, None, d, f), expert),
                      pl.BlockSpec((None, None, f, d), expert)],
            out_specs=any_spec,
            scratch_shapes=[pltpu.VMEM((2, rows, LANES), jnp.float32),
                            pltpu.VMEM((2, rows, LANES), jnp.float32),
                            pltpu.VMEM((d, f), jnp.bfloat16),
                            pltpu.VMEM((d, f), jnp.bfloat16),
                            pltpu.VMEM((f, d), jnp.bfloat16),
                            pltpu.SemaphoreType.DMA((2,)),
                            pltpu.SemaphoreType.DMA((2,))]),
        out_shape=jax.ShapeDtypeStruct((2 * x1t.shape[0], LANES), jnp.float32),
        compiler_params=pltpu.CompilerParams(dimension_semantics=("arbitrary",),
                                             vmem_limit_bytes=VMEM_LIMIT),
        name="moe_experts",
    )(tok, te, nv, jnp.full((1,), layer, jnp.int32), x1t, w_gate, w_up, w_down)


def _combine_kernel(x_ref, y0_ref, y1_ref, gcol_ref, lng_ref, lnb_ref, o_ref, *, alpha):
    tm = x_ref.shape[0]
    lane = lax.broadcasted_iota(jnp.int32, (1, LANES), 1)
    gc = gcol_ref[...]
    g0 = jnp.sum(jnp.where(lane == 0, gc, 0.0), axis=-1, keepdims=True)
    g1 = jnp.sum(jnp.where(lane == 1, gc, 0.0), axis=-1, keepdims=True)
    y = g0 * _from_token_tiles(y0_ref, tm) + g1 * _from_token_tiles(y1_ref, tm)
    o_ref[...] = _layer_norm(alpha * x_ref[...] + y, lng_ref[...], lnb_ref[...])


def _combine(x1, y, gcol, ln_g, ln_b, alpha):
    n, d = x1.shape
    const = lambda i: (0, 0)
    rows = ROW_TILE * TOKEN_TILE_ROWS
    return pl.pallas_call(
        functools.partial(_combine_kernel, alpha=alpha),
        grid=(n // ROW_TILE,),
        in_specs=[pl.BlockSpec((ROW_TILE, d), lambda i: (i, 0)),
                  pl.BlockSpec((rows, LANES), lambda i: (i, 0)),
                  pl.BlockSpec((rows, LANES), lambda i: (i + n // ROW_TILE, 0)),
                  pl.BlockSpec((ROW_TILE, LANES), lambda i: (i, 0)),
                  pl.BlockSpec((1, d), const), pl.BlockSpec((1, d), const)],
        out_specs=pl.BlockSpec((ROW_TILE, d), lambda i: (i, 0)),
        out_shape=jax.ShapeDtypeStruct((n, d), jnp.float32),
        compiler_params=pltpu.CompilerParams(dimension_semantics=("arbitrary",),
                                             vmem_limit_bytes=VMEM_LIMIT),
        name="moe_combine_ln",
    )(x1, y, y, gcol, ln_g, ln_b)


def _moe(x1, x1t, meta, gcol, cnt, w_gate, w_up, w_down, layer, ln_g, ln_b, alpha):
    n = x1.shape[0]
    max_tiles = (2 * n + N_EXPERTS * (MOE_TM - 1)) // MOE_TM
    counts = cnt[:, 0].astype(jnp.int32)
    e0, e1, r0, r1 = meta[0], meta[1], meta[2], meta[3]
    tok, te, nv = _plan(counts, e0, e1, r0, r1, max_tiles)
    y = _experts(tok, te, nv, x1t, w_gate, w_up, w_down, layer, max_tiles)
    return _combine(x1, y, gcol, ln_g, ln_b, alpha)


def _t5_bucket(dist):
    max_exact = N_BUCKETS // 2
    d = jnp.maximum(dist, 0)
    large = max_exact + (jnp.log(jnp.maximum(d, 1).astype(jnp.float32) / max_exact)
                         / math.log(MAX_DISTANCE / max_exact) * (N_BUCKETS - max_exact)).astype(jnp.int32)
    large = jnp.minimum(large, N_BUCKETS - 1)
    return jnp.where(d < max_exact, d, large)


def _tile_bias_kernel(w_ref, o_ref):
    w = w_ref[0]
    for q in range(BLOCK):
        o_ref[q:q + 1, :] = w[:, BLOCK - 1 - q:3 * BLOCK - 1 - q]


def _tile_bias(rel_bias):
    dist = jnp.arange(BLOCK + 1)
    rows = jnp.stack([rel_bias[_t5_bucket(dist * dilation)] for _, dilation in DIL_PATTERNS])
    n_br, _, heads = rows.shape
    table = rows.astype(jnp.float32).transpose(0, 2, 1).reshape(n_br * heads, BLOCK + 1)
    w = jnp.pad(table[:, ::-1], ((0, 0), (BLOCK - 1, BLOCK)))[:, None, :]
    return pl.pallas_call(
        _tile_bias_kernel,
        grid=(n_br * heads,),
        in_specs=[pl.BlockSpec((1, 1, 3 * BLOCK), lambda s: (s, 0, 0))],
        out_specs=pl.BlockSpec((None, BLOCK, 2 * BLOCK), lambda s: (s, 0, 0)),
        out_shape=jax.ShapeDtypeStruct((n_br * heads, BLOCK, 2 * BLOCK), jnp.float32),
        compiler_params=pltpu.CompilerParams(dimension_semantics=("arbitrary",)),
        name="tile_bias",
    )(w)


def kernel(x, w_in, g_sb, g_dil, w_out, ln1_g, ln1_b, ln2_g, ln2_b, rel_bias,
           w_router, b_router, w_gate, w_up, w_down):
    batch, seq, d = x.shape
    depth = w_in.shape[0]
    alpha = (2.0 * depth) ** 0.25
    n = batch * seq
    x2d = x.reshape(n, d)

    tile_bias = _tile_bias(rel_bias)
    wrt =w_router.T.astype(jnp.float32)
    wrt_hi = _bf16(wrt)
    wrt_lo = _bf16(wrt - wrt_hi.astype(jnp.float32))
    br = b_router.astype(jnp.float32).reshape(N_EXPERTS, 1)

    for l in range(depth):
        w_in_l = _bf16(w_in[l])
        proj_sb, proj_dl = _in_proj(x2d, w_in_l)
        y_sb = _sb_attention(proj_sb, g_sb[l].reshape(N_PAIRS_SB, 1, LANES), batch, seq)
        y_dl = _dil_attention(proj_dl, g_dil[l].reshape(N_PAIRS_DIL, 1, LANES),
                              tile_bias, batch, seq)
        x1, x1t, meta, gcol, cnt = _out_proj(x2d, y_sb, y_dl, _bf16(w_out[l]),
                                             ln1_g[l].reshape(1, d), ln1_b[l].reshape(1, d),
                                             wrt_hi, wrt_lo, br, alpha)
        x2d = _moe(x1, x1t, meta, gcol, cnt, w_gate, w_up, w_down, l,
                   ln2_g[l].reshape(1, d), ln2_b[l].reshape(1, d), alpha)
    return x2d.reshape(batch, seq, d)
```

```python
import functools
import math

import jax
import jax.numpy as jnp
from jax import lax
from jax.experimental import pallas as pl
from jax.experimental.pallas import tpu as pltpu

D_MODEL = 1024
HEAD_DIM = 64
N_HEADS_SB = 8
N_HEADS_DIL = 8
D_SB = N_HEADS_SB * HEAD_DIM
D_DIL = N_HEADS_DIL * HEAD_DIM
DIL_PATTERNS = ((128, 1), (512, 4), (2048, 16))
BLOCK = 128
N_BUCKETS = 32
MAX_DISTANCE = 2048
N_EXPERTS = 16
N_GROUPS = 4
EXPERTS_PER_GROUP = N_EXPERTS // N_GROUPS
D_FF_EXPERT = 1024
LN_EPS = 1e-5
NEG_INF = -1e30

LANES = 128
N_PAIRS_SB = D_SB // LANES
N_PAIRS_DIL = D_DIL // LANES
VMEM_LIMIT = 56 * 1024 * 1024

ROW_TILE = 512
SB_TQ = 256
SB_TK = 256
SB_TILES_PER_STEP = 2
SB_SKIP_BOUND = -110.0
DIL_SUPER = 2048
DIL_UNITS = DIL_SUPER // BLOCK
TOKEN_TILE_ROWS = D_MODEL // LANES
MOE_TM = 256
PERM_TP = 512

_NT = (((1,), (1,)), ((), ()))


def _bf16(x):
    return x.astype(jnp.bfloat16)


def _split_bf16(x):
    hi = _bf16(x)
    lo = _bf16(x - hi.astype(jnp.float32))
    return hi, lo


def _dot(a, b):
    return jnp.dot(a, b, preferred_element_type=jnp.float32)


def _dot_nt(a, b):
    return lax.dot_general(a, b, _NT, preferred_element_type=jnp.float32)


def _head_rms_gain(o, g):
    r = lax.broadcasted_iota(jnp.int32, (LANES, LANES), 0) // HEAD_DIM
    c = lax.broadcasted_iota(jnp.int32, (LANES, LANES), 1) // HEAD_DIM
    same_head = jnp.where(r == c, 1.0 / HEAD_DIM, 0.0).astype(jnp.bfloat16)
    hi, lo = _split_bf16(o * o)
    ms = _dot(hi, same_head) + _dot(lo, same_head)
    return o * lax.rsqrt(ms + 1e-6) * g


def _layer_norm(x, g, b):
    mu = jnp.mean(x, axis=-1, keepdims=True)
    xc = x - mu
    var = jnp.mean(xc * xc, axis=-1, keepdims=True)
    return xc * lax.rsqrt(var + LN_EPS) * g + b


def _in_proj_kernel(x_ref, w_ref, sb_ref, dl_ref):
    x = _bf16(x_ref[...])
    scale = HEAD_DIM ** -0.5
    col = 0
    for o_ref, n_q in ((sb_ref, N_PAIRS_SB), (dl_ref, N_PAIRS_DIL)):
        for j in range(0, o_ref.shape[0], 2):
            res = _dot(x, w_ref[:, col:col + 2 * LANES])
            col += 2 * LANES
            for jj in range(2):
                blk = res[:, jj * LANES:(jj + 1) * LANES]
                if j + jj < n_q:
                    blk = blk * scale
                o_ref[j + jj] = blk.astype(o_ref.dtype)


def _in_proj(x2d, w):
    n, d = x2d.shape
    slabs = lambda k: pl.BlockSpec((k, ROW_TILE, LANES), lambda i: (0, i, 0))
    return pl.pallas_call(
        _in_proj_kernel,
        grid=(n // ROW_TILE,),
        in_specs=[pl.BlockSpec((ROW_TILE, d), lambda i: (i, 0)),
                  pl.BlockSpec((d, w.shape[1]), lambda i: (0, 0))],
        out_specs=[slabs(3 * N_PAIRS_SB), slabs(3 * N_PAIRS_DIL)],
        out_shape=[jax.ShapeDtypeStruct((3 * N_PAIRS_SB, n, LANES), jnp.bfloat16),
                   jax.ShapeDtypeStruct((3 * N_PAIRS_DIL, n, LANES), jnp.float32)],
        compiler_params=pltpu.CompilerParams(dimension_semantics=("arbitrary",),
                                             vmem_limit_bytes=VMEM_LIMIT),
        name="in_proj",
    )(x2d, w)


def _sb_kernel(q_ref, k_ref, v_ref, g_ref, o_ref, carry_ref, acc_ref):
    step = pl.program_id(2)
    lane = lax.broadcasted_iota(jnp.int32, (1, LANES), 1)
    first_head = lane < HEAD_DIM

    jj = lax.broadcasted_iota(jnp.int32, (SB_TK, SB_TK + LANES), 0)
    ss = lax.broadcasted_iota(jnp.int32, (SB_TK, SB_TK + LANES), 1)
    suffix = jnp.where((jj > ss) | (ss >= SB_TK), 1.0, 0.0).astype(jnp.bfloat16)

    tq_i = lax.broadcasted_iota(jnp.int32, (2 * SB_TQ, SB_TK), 0) % SB_TQ
    ts_i = lax.broadcasted_iota(jnp.int32, (2 * SB_TQ, SB_TK), 1)
    causal = ts_i < tq_i

    def block(q2, kb, diag, carry, acc, weight=None):
        off = pl.multiple_of(kb * SB_TK, SB_TK)
        kblk = k_ref[pl.ds(off, SB_TK), :]
        vblk = v_ref[pl.ds(off, SB_TK), :]
        z = _dot_nt(q2, kblk)
        sp = jnp.log(1.0 + jnp.exp(-jnp.abs(z)))
        log_beta = jnp.minimum(z, 0.0) - sp
        log_1m = log_beta - z
        if diag:
            log_1m = jnp.where(causal, log_1m, 0.0)
        sfx = _dot(_bf16(log_1m), suffix)
        between = sfx[:, :SB_TK] + jnp.concatenate([carry] * (SB_TK // LANES), axis=1)
        a = jnp.exp(log_beta + between)
        if diag:
            a = jnp.where(causal, a, 0.0)
        out = _dot(_bf16(a), vblk)
        tot = sfx[:, SB_TK:]
        if weight is not None:
            out = out * weight
            tot = tot * weight
        return carry + tot, acc + out

    zeros = jnp.zeros((2 * SB_TQ, LANES), jnp.float32)
    tiles = []
    for t in range(SB_TILES_PER_STEP):
        i = step * SB_TILES_PER_STEP + t
        q = q_ref[t * SB_TQ:(t + 1) * SB_TQ, :]
        zero = jnp.zeros_like(q)
        q2 = jnp.concatenate([jnp.where(first_head, q, zero), jnp.where(first_head, zero, q)], axis=0)
        carry, acc = block(q2, i, True, zeros, zeros)
        has_prev = jnp.where(i >= 1, 1.0, 0.0)
        carry, acc = block(q2, jnp.maximum(i - 1, 0), False, carry, acc, weight=has_prev)
        carry_ref[t] = carry
        acc_ref[t] = acc
        tiles.append((i, q2, jnp.max(carry)))

    for t, (i, q2, mx0) in enumerate(tiles):
        def cond(state):
            kb, mx = state
            return jnp.logical_and(kb >= 0, mx > SB_SKIP_BOUND)

        def body(state, t=t, q2=q2):
            kb, _ = state
            c, a = block(q2, kb, False, carry_ref[t], acc_ref[t])
            carry_ref[t] = c
            acc_ref[t] = a
            return kb - 1, jnp.max(c)

        lax.while_loop(cond, body, (i - 2, mx0))

        o = jnp.where(first_head, acc_ref[t, :SB_TQ, :], acc_ref[t, SB_TQ:, :])
        o_ref[t * SB_TQ:(t + 1) * SB_TQ, :] = _head_rms_gain(o, g_ref[...]).astype(o_ref.dtype)


def _sb_attention(proj, g, batch, seq):
    n = batch * seq
    rows = SB_TQ * SB_TILES_PER_STEP
    nq = seq // rows
    return pl.pallas_call(
        _sb_kernel,
        grid=(batch, N_PAIRS_SB, nq),
        in_specs=[
            pl.BlockSpec((None, rows, LANES), lambda b, p, i: (p, b * nq + i, 0)),
            pl.BlockSpec((None, seq, LANES), lambda b, p, i: (N_PAIRS_SB + p, b, 0)),
            pl.BlockSpec((None, seq, LANES), lambda b, p, i: (2 * N_PAIRS_SB + p, b, 0)),
            pl.BlockSpec((None, 1, LANES), lambda b, p, i: (p, 0, 0)),
        ],
        out_specs=pl.BlockSpec((None, rows, LANES), lambda b, p, i: (p, b * nq + i, 0)),
        out_shape=jax.ShapeDtypeStruct((N_PAIRS_SB, n, LANES), jnp.bfloat16),
        scratch_shapes=[pltpu.VMEM((SB_TILES_PER_STEP, 2 * SB_TQ, LANES), jnp.float32),
                        pltpu.VMEM((SB_TILES_PER_STEP, 2 * SB_TQ, LANES), jnp.float32)],
        compiler_params=pltpu.CompilerParams(
            dimension_semantics=("arbitrary", "arbitrary", "arbitrary"),
            vmem_limit_bytes=VMEM_LIMIT),
        name="sb_attention",
    )(proj, proj, proj, g)


def _dil_kernel(tbias_ref, q_ref, k_ref, v_ref, g_ref, o_ref,
                bias_ref, obr_ref, lse_ref):
    b = pl.program_id(0)
    p = pl.program_id(1)
    i = pl.program_id(2)
    n_br = len(DIL_PATTERNS)

    qi = lax.broadcasted_iota(jnp.int32, (BLOCK, 2 * BLOCK), 0)
    kj = lax.broadcasted_iota(jnp.int32, (BLOCK, 2 * BLOCK), 1)
    steps = qi + BLOCK - kj

    assert all(window // dilation == BLOCK for window, dilation in DIL_PATTERNS)
    valid = (steps >= 0) & (steps <= BLOCK)

    @pl.when((b == 0) & (p == 0) & (i == 0))
    def _():
        n_tiles = n_br * N_HEADS_DIL

        def per_tile(s, _):
            tile = tbias_ref[s]
            rows = pl.ds(pl.multiple_of(s * BLOCK, BLOCK), BLOCK)
            bias_ref[rows, :] = jnp.where(valid, tile, NEG_INF)
            rows = pl.ds(pl.multiple_of((n_tiles + s) * BLOCK, BLOCK), BLOCK)
            bias_ref[rows, :] = jnp.where(valid & (kj >= BLOCK), tile, NEG_INF)
            return 0
        lax.fori_loop(0, n_br * N_HEADS_DIL, per_tile, 0)

    lane = lax.broadcasted_iota(jnp.int32, (1, LANES), 1)
    first_head = lane < HEAD_DIM
    t0 = i * DIL_SUPER

    def strided(ref, start, dilation):
        if dilation == 1:
            return ref[pl.ds(start, BLOCK), :]
        return ref[pl.ds(start, BLOCK, stride=dilation), :]

    for gidx, (window, dilation) in enumerate(DIL_PATTERNS):
        units_per_residue = DIL_UNITS // dilation
        shift = units_per_residue.bit_length() - 1

        def unit(u, _, gidx=gidx, dilation=dilation, units_per_residue=units_per_residue, shift=shift):
            r = lax.shift_right_logical(u, shift)
            cb = u & (units_per_residue - 1)
            sq = r + cb * (dilation * BLOCK)
            cur = t0 + sq
            prev = cur - dilation * BLOCK
            prev_ok = prev >= 0
            prev_c = jnp.where(prev_ok, prev, cur)
            no_prev = jnp.where(prev_ok, 0, 1)

            qb = _bf16(strided(q_ref, sq, dilation))
            kk = _bf16(jnp.concatenate([strided(k_ref, prev_c, dilation),
                                        strided(k_ref, cur, dilation)], axis=0))
            vv = _bf16(jnp.concatenate([strided(v_ref, prev_c, dilation),
                                        strided(v_ref, cur, dilation)], axis=0))
            zero = jnp.zeros_like(qb)
            q2 = jnp.concatenate([jnp.where(first_head, qb, zero),
                                  jnp.where(first_head, zero, qb)], axis=0)
            z = _dot_nt(q2, kk)
            slot = no_prev * (n_br * N_HEADS_DIL) + gidx * N_HEADS_DIL + 2 * p
            lg = z + bias_ref[pl.ds(pl.multiple_of(slot * BLOCK, 2 * BLOCK), 2 * BLOCK), :]
            m = jnp.max(lg, axis=-1, keepdims=True)
            pe = jnp.exp(lg - m)
            den = jnp.sum(pe, axis=-1, keepdims=True)
            out = _dot(_bf16(pe), vv) / den
            lse = m + jnp.log(den)
            o_tile = jnp.where(first_head, out[:BLOCK], out[BLOCK:])
            l_tile = jnp.where(first_head, lse[:BLOCK], lse[BLOCK:])
            if dilation == 1:
                idx = pl.ds(sq, BLOCK)
            else:
                idx = pl.ds(sq, BLOCK, stride=dilation)
            obr_ref.at[gidx][idx, :] = o_tile
            lse_ref.at[gidx][idx, :] = l_tile
            return 0

        lax.fori_loop(0, DIL_UNITS, unit, 0, unroll=8)

    chunk = 256

    def combine(c, _):
        rows = pl.ds(pl.multiple_of(c * chunk, chunk), chunk)
        ls = [lse_ref[gidx, rows, :] for gidx in range(n_br)]
        m = functools.reduce(jnp.maximum, ls)
        es = [jnp.exp(l - m) for l in ls]
        num = sum(e * obr_ref[gidx, rows, :] for gidx, e in enumerate(es))
        o = num / sum(es)
        o_ref[rows, :] = _head_rms_gain(o, g_ref[...]).astype(o_ref.dtype)
        return 0

    lax.fori_loop(0, DIL_SUPER // chunk, combine, 0)


def _dil_attention(proj, g, tile_bias, batch, seq):
    n = batch * seq
    ns = seq // DIL_SUPER
    n_br = len(DIL_PATTERNS)
    return pl.pallas_call(
        _dil_kernel,
        grid=(batch, N_PAIRS_DIL, ns),
        in_specs=[
            pl.BlockSpec((n_br * N_HEADS_DIL, BLOCK, 2 * BLOCK), lambda b, p, i: (0, 0, 0)),
            pl.BlockSpec((None, DIL_SUPER, LANES), lambda b, p, i: (p, b * ns + i, 0)),
            pl.BlockSpec((None, seq, LANES), lambda b, p, i: (N_PAIRS_DIL + p, b, 0)),
            pl.BlockSpec((None, seq, LANES), lambda b, p, i: (2 * N_PAIRS_DIL + p, b, 0)),
            pl.BlockSpec((None, 1, LANES), lambda b, p, i: (p, 0, 0)),
        ],
        out_specs=pl.BlockSpec((None, DIL_SUPER, LANES), lambda b, p, i: (p, b * ns + i, 0)),
        out_shape=jax.ShapeDtypeStruct((N_PAIRS_DIL, n, LANES), jnp.bfloat16),
        scratch_shapes=[pltpu.VMEM((2 * n_br * N_HEADS_DIL * BLOCK, 2 * BLOCK), jnp.float32),
                        pltpu.VMEM((n_br, DIL_SUPER, LANES), jnp.float32),
                        pltpu.VMEM((n_br, DIL_SUPER, LANES), jnp.float32)],
        compiler_params=pltpu.CompilerParams(
            dimension_semantics=("arbitrary", "arbitrary", "arbitrary"),
            vmem_limit_bytes=VMEM_LIMIT),
        name="dilated_attention",
    )(tile_bias, proj, proj, proj, g)


def _router(x1, wrt_hi, wrt_lo, br):
    tm = x1.shape[0]
    xh, xl = _split_bf16(x1)
    lt = _dot_nt(wrt_hi, xh) + _dot_nt(wrt_hi, xl) + _dot_nt(wrt_lo, xh) + br
    rows = [lt[e:e + 1, :] for e in range(N_EXPERTS)]
    m = functools.reduce(jnp.maximum, rows)
    ex = [jnp.exp(r - m) for r in rows]
    tot = functools.reduce(lambda a, c: a + c, ex)
    probs = [e / tot for e in ex]

    keep, score = [], []
    for gi in range(N_GROUPS):
        members = range(gi * EXPERTS_PER_GROUP, (gi + 1) * EXPERTS_PER_GROUP)
        s = None
        for e in members:
            rank = None
            for o in members:
                if o == e:
                    continue
                beats = (probs[o] > probs[e]) | ((probs[o] == probs[e]) & (o < e))
                beats = beats.astype(jnp.int32)
                rank = beats if rank is None else rank + beats
            k = rank < 2
            keep.append(k)
            contrib = jnp.where(k, probs[e], 0.0)
            s = contrib if s is None else s + contrib
        score.append(s)

    sel, gates = [], []
    for gi in range(N_GROUPS):
        chosen = None
        for o in range(N_GROUPS):
            if o == gi:
                continue
            c = (score[gi] > score[o]) if o < gi else (score[gi] >= score[o])
            chosen = c if chosen is None else (chosen & c)
        for e in range(gi * EXPERTS_PER_GROUP, (gi + 1) * EXPERTS_PER_GROUP):
            s = keep[e] & chosen
            sel.append(s)
            gates.append(jnp.where(s, probs[e] / score[gi], 0.0))

    idx0 = functools.reduce(jnp.minimum, [jnp.where(sel[e], e, N_EXPERTS) for e in range(N_EXPERTS)])
    idx1 = functools.reduce(jnp.maximum, [jnp.where(sel[e], e, -1) for e in range(N_EXPERTS)])
    g0 = sum(jnp.where(idx0 == e, gates[e], 0.0) for e in range(N_EXPERTS))
    g1 = sum(jnp.where(idx1 == e, gates[e], 0.0) for e in range(N_EXPERTS))

    row_id = lax.broadcasted_iota(jnp.int32, (N_EXPERTS, tm), 0)
    selmat = jnp.zeros((N_EXPERTS, tm), jnp.float32)
    for e in range(N_EXPERTS):
        selmat = jnp.where((row_id == e) & sel[e], 1.0, selmat)
    return idx0, idx1, g0, g1, selmat


def _to_token_tiles(ref, x):
    rows = x.shape[0]
    for s in range(TOKEN_TILE_ROWS):
        ref[pl.ds(s, rows, stride=TOKEN_TILE_ROWS), :] = x[:, s * LANES:(s + 1) * LANES]


def _from_token_tiles(ref, rows):
    return jnp.concatenate([ref[pl.ds(s, rows, stride=TOKEN_TILE_ROWS), :]
                            for s in range(TOKEN_TILE_ROWS)], axis=1)


def _out_proj_kernel(x_ref, ysb_ref, ydl_ref, w_ref, lng_ref, lnb_ref,
                     wrh_ref, wrl_ref, br_ref,
                     x1_ref, x1t_ref, meta_ref, gcol_ref, cnt_ref, carry_ref, *, alpha):
    i = pl.program_id(0)
    tm = x_ref.shape[0]

    @pl.when(i == 0)
    def _():
        carry_ref[...] = jnp.zeros_like(carry_ref)

    y = jnp.concatenate([ysb_ref[j] for j in range(N_PAIRS_SB)]
                        + [ydl_ref[j] for j in range(N_PAIRS_DIL)], axis=1)
    h = _dot(y, w_ref[...])
    x1 = _layer_norm(alpha * x_ref[...] + h, lng_ref[...], lnb_ref[...])
    x1_ref[...] = x1
    _to_token_tiles(x1t_ref, x1)

    idx0, idx1, g0, g1, selmat = _router(x1, wrh_ref[...], wrl_ref[...], br_ref[...])

    jj = lax.broadcasted_iota(jnp.int32, (tm, tm + LANES), 0)
    ss = lax.broadcasted_iota(jnp.int32, (tm, tm + LANES), 1)
    before = jnp.where((jj < ss) | (ss >= tm), 1.0, 0.0).astype(jnp.bfloat16)
    pc = _dot(_bf16(selmat), before)
    carry = carry_ref[...]
    rankmat = pc[:, :tm] + jnp.concatenate([carry] * (tm // LANES), axis=1)
    carry = carry + pc[:, tm:]
    carry_ref[...] = carry
    cnt_ref[...] = carry

    rank0 = sum(jnp.where(idx0 == e, rankmat[e:e + 1, :], 0.0) for e in range(N_EXPERTS))
    rank1 = sum(jnp.where(idx1 == e, rankmat[e:e + 1, :], 0.0) for e in range(N_EXPERTS))

    row8 = lax.broadcasted_iota(jnp.int32, (8, tm), 0)
    meta = jnp.zeros((8, tm), jnp.int32)
    for r, v in enumerate((idx0, idx1, rank0.astype(jnp.int32), rank1.astype(jnp.int32))):
        meta = jnp.where(row8 == r, v, meta)
    meta_ref[...] = meta

    rowl = lax.broadcasted_iota(jnp.int32, (LANES, tm), 0)
    gt = jnp.where(rowl == 0, g0, jnp.where(rowl == 1, g1, 0.0))
    gcol_ref[...] = gt.T


def _out_proj(x2d, y_sb, y_dl, w_out, ln_g, ln_b, wrt_hi, wrt_lo, br, alpha):
    n, d = x2d.shape
    const = lambda i: (0, 0)
    return pl.pallas_call(
        functools.partial(_out_proj_kernel, alpha=alpha),
        grid=(n // ROW_TILE,),
        in_specs=[
            pl.BlockSpec((ROW_TILE, d), lambda i: (i, 0)),
            pl.BlockSpec((N_PAIRS_SB, ROW_TILE, LANES), lambda i: (0, i, 0)),
            pl.BlockSpec((N_PAIRS_DIL, ROW_TILE, LANES), lambda i: (0, i, 0)),
            pl.BlockSpec((d, d), const),
            pl.BlockSpec((1, d), const),
            pl.BlockSpec((1, d), const),
            pl.BlockSpec((N_EXPERTS, d), const),
            pl.BlockSpec((N_EXPERTS, d), const),
            pl.BlockSpec((N_EXPERTS, 1), const),
        ],
        out_specs=[pl.BlockSpec((ROW_TILE, d), lambda i: (i, 0)),
                   pl.BlockSpec((ROW_TILE * TOKEN_TILE_ROWS, LANES), lambda i: (i, 0)),
                   pl.BlockSpec((8, ROW_TILE), lambda i: (0, i)),
                   pl.BlockSpec((ROW_TILE, LANES), lambda i: (i, 0)),
                   pl.BlockSpec((N_EXPERTS, LANES), const)],
        out_shape=[jax.ShapeDtypeStruct((n, d), jnp.float32),
                   jax.ShapeDtypeStruct((n * TOKEN_TILE_ROWS, LANES), jnp.float32),
                   jax.ShapeDtypeStruct((8, n), jnp.int32),
                   jax.ShapeDtypeStruct((n, LANES), jnp.float32),
                   jax.ShapeDtypeStruct((N_EXPERTS, LANES), jnp.float32)],
        scratch_shapes=[pltpu.VMEM((N_EXPERTS, LANES), jnp.float32)],
        compiler_params=pltpu.CompilerParams(dimension_semantics=("arbitrary",),
                                             vmem_limit_bytes=VMEM_LIMIT),
        name="out_proj_ln_router",
    )(x2d, y_sb, y_dl, w_out, ln_g, ln_b, wrt_hi, wrt_lo, br)


def _segment_offsets(cnt_ref, off_ref):
    acc = jnp.int32(0)
    padded = []
    for e in range(N_EXPERTS):
        off_ref[e] = acc
        pc = lax.div(cnt_ref[e] + (MOE_TM - 1), MOE_TM) * MOE_TM
        padded.append(pc)
        acc = acc + pc
    return padded, acc


def _row_copy(src_ref, src_row, dst_ref, dst_row, sem):
    def tile(row):
        if isinstance(row, int):
            return pl.ds(row * TOKEN_TILE_ROWS, TOKEN_TILE_ROWS)
        return pl.ds(pl.multiple_of(row * TOKEN_TILE_ROWS, TOKEN_TILE_ROWS), TOKEN_TILE_ROWS)
    return pltpu.make_async_copy(src_ref.at[tile(src_row), :], dst_ref.at[tile(dst_row), :], sem)


def _plan_kernel(cnt_ref, e0_ref, e1_ref, r0_ref, r1_ref, tok_ref, te_ref, pos0_ref, pos1_ref,
                 off_ref, *, max_tiles):
    i = pl.program_id(0)
    tp = e0_ref.shape[0]
    padded, total = _segment_offsets(cnt_ref, off_ref)

    @pl.when(i == 0)
    def _():
        def clear(k, c):
            tok_ref[k] = 0
            return c
        for e in range(N_EXPERTS):
            lax.fori_loop(off_ref[e] + cnt_ref[e], off_ref[e] + padded[e], clear, 0)
        lax.fori_loop(total, max_tiles * MOE_TM, clear, 0)

        n_tiles = lax.div(total, MOE_TM)
        for e in range(N_EXPERTS):
            first = lax.div(off_ref[e], MOE_TM)
            count = lax.div(padded[e], MOE_TM)

            def fill(k, c, first=first, e=e):
                te_ref[first + k] = e
                return c
            lax.fori_loop(0, count, fill, 0)
        last = te_ref[n_tiles - 1]

        def fill_tail(k, c):
            te_ref[k] = last
            return c
        lax.fori_loop(n_tiles, max_tiles, fill_tail, 0)
        te_ref[max_tiles] = n_tiles

    def place(j, c):
        token = i * tp + j
        p0 = off_ref[e0_ref[j]] + r0_ref[j]
        p1 = off_ref[e1_ref[j]] + r1_ref[j]
        tok_ref[p0] = token
        tok_ref[p1] = token
        pos0_ref[j] = p0
        pos1_ref[j] = p1
        return c
    lax.fori_loop(0, tp, place, 0, unroll=8)


def _plan(counts, e0, e1, r0, r1, max_tiles):
    n = e0.shape[0]
    smem = pl.BlockSpec(memory_space=pltpu.SMEM)
    vec = pl.BlockSpec((PERM_TP,), lambda i: (i,), memory_space=pltpu.SMEM)
    return pl.pallas_call(
        functools.partial(_plan_kernel, max_tiles=max_tiles),
        grid=(n // PERM_TP,),
        in_specs=[smem, vec, vec, vec, vec],
        out_specs=[smem, smem, vec, vec],
        out_shape=[jax.ShapeDtypeStruct((max_tiles * MOE_TM,), jnp.int32),
                   jax.ShapeDtypeStruct((max_tiles + 1,), jnp.int32),
                   jax.ShapeDtypeStruct((n,), jnp.int32),
                   jax.ShapeDtypeStruct((n,), jnp.int32)],
        scratch_shapes=[pltpu.SMEM((N_EXPERTS,), jnp.int32)],
        compiler_params=pltpu.CompilerParams(dimension_semantics=("arbitrary",)),
        name="moe_plan",
    )(counts, e0, e1, r0, r1)


def _experts_kernel(tok_ref, te_ref, layer_ref, x_ref, wg_ref, wu_ref, wd_ref, ys_ref,
                    xbuf_ref, wgb_ref, wub_ref, wdb_ref, gsem, *, max_tiles):
    i = pl.program_id(0)
    n_tiles = te_ref[max_tiles]
    slot = i & 1
    tile_rows = MOE_TM * TOKEN_TILE_ROWS

    def gather(tile, s):
        base = tile * MOE_TM

        def body(j, c):
            _row_copy(x_ref, tok_ref[base + j], xbuf_ref.at[s], j, gsem.at[s]).start()
            return c
        lax.fori_loop(0, MOE_TM, body, 0, unroll=8)

    @pl.when(i == 0)
    def _():
        gather(0, 0)

    @pl.when(i + 1 < n_tiles)
    def _():
        gather(i + 1, 1 - slot)

    @pl.when(i < n_tiles)
    def _():
        pltpu.make_async_copy(x_ref.at[pl.ds(0, tile_rows), :], xbuf_ref.at[slot],
                              gsem.at[slot]).wait()

        @pl.when((i == 0) | (te_ref[i] != te_ref[jnp.maximum(i - 1, 0)]))
        def _():
            wgb_ref[...] = _bf16(wg_ref[...])
            wub_ref[...] = _bf16(wu_ref[...])
            wdb_ref[...] = _bf16(wd_ref[...])

        xb = _bf16(_from_token_tiles(xbuf_ref.at[slot], MOE_TM))
        gt = _dot(xb, wgb_ref[...])
        up = _dot(xb, wub_ref[...])
        h = gt / (1.0 + jnp.exp(-gt)) * up
        _to_token_tiles(ys_ref, _dot(_bf16(h), wdb_ref[...]))

    @pl.when(i >= n_tiles)
    def _():
        ys_ref[...] = jnp.zeros_like(ys_ref)


def _experts(tok, te, x1t, w_gate, w_up, w_down, layer, max_tiles):
    d, f = w_gate.shape[-2:]
    rows = MOE_TM * TOKEN_TILE_ROWS
    expert = lambda i, tok, te, lay: (lay[0], te[i], 0, 0)
    return pl.pallas_call(
        functools.partial(_experts_kernel, max_tiles=max_tiles),
        grid_spec=pltpu.PrefetchScalarGridSpec(
            num_scalar_prefetch=3,
            grid=(max_tiles,),
            in_specs=[pl.BlockSpec(memory_space=pl.ANY),
                      pl.BlockSpec((None, None, d, f), expert),
                      pl.BlockSpec((None, None, d, f), expert),
                      pl.BlockSpec((None, None, f, d), expert)],
            out_specs=pl.BlockSpec((rows, LANES), lambda i, tok, te, lay: (i, 0)),
            scratch_shapes=[pltpu.VMEM((2, rows, LANES), jnp.float32),
                            pltpu.VMEM((d, f), jnp.bfloat16),
                            pltpu.VMEM((d, f), jnp.bfloat16),
                            pltpu.VMEM((f, d), jnp.bfloat16),
                            pltpu.SemaphoreType.DMA((2,))]),
        out_shape=jax.ShapeDtypeStruct((max_tiles * rows, LANES), jnp.float32),
        compiler_params=pltpu.CompilerParams(dimension_semantics=("arbitrary",),
                                             vmem_limit_bytes=VMEM_LIMIT),
        name="moe_experts",
    )(tok, te, jnp.full((1,), layer, jnp.int32), x1t, w_gate, w_up, w_down)


def _combine_kernel(pos0_ref, pos1_ref, x_ref, gcol_ref, lng_ref, lnb_ref, ys_ref, o_ref,
                    ybuf_ref, sem, *, alpha):
    i = pl.program_id(0)
    n_steps = pl.num_programs(0)
    tm = x_ref.shape[0]
    slot = i & 1
    tile_rows = tm * TOKEN_TILE_ROWS

    def gather(step, s):
        base = step * tm

        def body(j, c):
            _row_copy(ys_ref, pos0_ref[base + j], ybuf_ref.at[s, 0], j, sem.at[s]).start()
            _row_copy(ys_ref, pos1_ref[base + j], ybuf_ref.at[s, 1], j, sem.at[s]).start()
            return c
        lax.fori_loop(0, tm, body, 0, unroll=8)

    @pl.when(i == 0)
    def _():
        gather(0, 0)

    @pl.when(i + 1 < n_steps)
    def _():
        gather(i + 1, 1 - slot)

    for k in range(2):
        pltpu.make_async_copy(ys_ref.at[pl.ds(0, tile_rows), :], ybuf_ref.at[slot, k],
                              sem.at[slot]).wait()

    lane = lax.broadcasted_iota(jnp.int32, (1, LANES), 1)
    gc = gcol_ref[...]
    g0 = jnp.sum(jnp.where(lane == 0, gc, 0.0), axis=-1, keepdims=True)
    g1 = jnp.sum(jnp.where(lane == 1, gc, 0.0), axis=-1, keepdims=True)
    y = (g0 * _from_token_tiles(ybuf_ref.at[slot, 0], tm)
         + g1 * _from_token_tiles(ybuf_ref.at[slot, 1], tm))
    o_ref[...] = _layer_norm(alpha * x_ref[...] + y, lng_ref[...], lnb_ref[...])


def _combine(pos0, pos1, x1, ys, gcol, ln_g, ln_b, alpha):
    n, d = x1.shape
    const = lambda i, p0, p1: (0, 0)
    tile = lambda i, p0, p1: (i, 0)
    return pl.pallas_call(
        functools.partial(_combine_kernel, alpha=alpha),
        grid_spec=pltpu.PrefetchScalarGridSpec(
            num_scalar_prefetch=2,
            grid=(n // ROW_TILE,),
            in_specs=[pl.BlockSpec((ROW_TILE, d), tile),
                      pl.BlockSpec((ROW_TILE, LANES), tile),
                      pl.BlockSpec((1, d), const), pl.BlockSpec((1, d), const),
                      pl.BlockSpec(memory_space=pl.ANY)],
            out_specs=pl.BlockSpec((ROW_TILE, d), tile),
            scratch_shapes=[pltpu.VMEM((2, 2, ROW_TILE * TOKEN_TILE_ROWS, LANES), jnp.float32),
                            pltpu.SemaphoreType.DMA((2,))]),
        out_shape=jax.ShapeDtypeStruct((n, d), jnp.float32),
        compiler_params=pltpu.CompilerParams(dimension_semantics=("arbitrary",),
                                             vmem_limit_bytes=VMEM_LIMIT),
        name="moe_combine_ln",
    )(pos0, pos1, x1, gcol, ln_g, ln_b, ys)


def _moe(x1, x1t, meta, gcol, cnt, w_gate, w_up, w_down, layer, ln_g, ln_b, alpha):
    n = x1.shape[0]
    max_tiles = (2 * n + N_EXPERTS * (MOE_TM - 1)) // MOE_TM
    counts = cnt[:, 0].astype(jnp.int32)
    e0, e1, r0, r1 = meta[0], meta[1], meta[2], meta[3]
    tok, te, pos0, pos1 = _plan(counts, e0, e1, r0, r1, max_tiles)
    ys = _experts(tok, te, x1t, w_gate, w_up, w_down, layer, max_tiles)
    return _combine(pos0, pos1, x1, ys, gcol, ln_g, ln_b, alpha)


def _t5_bucket(dist):
    max_exact = N_BUCKETS // 2
    d = jnp.maximum(dist, 0)
    large = max_exact + (jnp.log(jnp.maximum(d, 1).astype(jnp.float32) / max_exact)
                         / math.log(MAX_DISTANCE / max_exact) * (N_BUCKETS - max_exact)).astype(jnp.int32)
    large = jnp.minimum(large, N_BUCKETS - 1)
    return jnp.where(d < max_exact, d, large)


def _tile_bias_kernel(w_ref, o_ref):
    w = w_ref[0]
    for q in range(BLOCK):
        o_ref[q:q + 1, :] = w[:, BLOCK - 1 - q:3 * BLOCK - 1 - q]


def _tile_bias(rel_bias):
    dist = jnp.arange(BLOCK + 1)
    rows = jnp.stack([rel_bias[_t5_bucket(dist * dilation)] for _, dilation in DIL_PATTERNS])
    n_br, _, heads = rows.shape
    table = rows.astype(jnp.float32).transpose(0, 2, 1).reshape(n_br * heads, BLOCK + 1)
    w = jnp.pad(table[:, ::-1], ((0, 0), (BLOCK - 1, BLOCK)))[:, None, :]
    return pl.pallas_call(
        _tile_bias_kernel,
        grid=(n_br * heads,),
        in_specs=[pl.BlockSpec((1, 1, 3 * BLOCK), lambda s: (s, 0, 0))],
        out_specs=pl.BlockSpec((None, BLOCK, 2 * BLOCK), lambda s: (s, 0, 0)),
        out_shape=jax.ShapeDtypeStruct((n_br * heads, BLOCK, 2 * BLOCK), jnp.float32),
        compiler_params=pltpu.CompilerParams(dimension_semantics=("arbitrary",)),
        name="tile_bias",
    )(w)


def kernel(x, w_in, g_sb, g_dil, w_out, ln1_g, ln1_b, ln2_g, ln2_b, rel_bias,
           w_router, b_router, w_gate, w_up, w_down):
    batch, seq, d = x.shape
    depth = w_in.shape[0]
    alpha = (2.0 * depth) ** 0.25
    n = batch * seq
    x2d = x.reshape(n, d)

    tile_bias = _tile_bias(rel_bias)
    wrt =w_router.T.astype(jnp.float32)
    wrt_hi = _bf16(wrt)
    wrt_lo = _bf16(wrt - wrt_hi.astype(jnp.float32))
    br = b_router.astype(jnp.float32).reshape(N_EXPERTS, 1)

    for l in range(depth):
        w_in_l = _bf16(w_in[l])
        proj_sb, proj_dl = _in_proj(x2d, w_in_l)
        y_sb = _sb_attention(proj_sb, g_sb[l].reshape(N_PAIRS_SB, 1, LANES), batch, seq)
        y_dl = _dil_attention(proj_dl, g_dil[l].reshape(N_PAIRS_DIL, 1, LANES),
                              tile_bias, batch, seq)
        x1, x1t, meta, gcol, cnt = _out_proj(x2d, y_sb, y_dl, _bf16(w_out[l]),
                                             ln1_g[l].reshape(1, d), ln1_b[l].reshape(1, d),
                                             wrt_hi, wrt_lo, br, alpha)
        x2d = _moe(x1, x1t, meta, gcol, cnt, w_gate, w_up, w_down, l,
                   ln2_g[l].reshape(1, d), ln2_b[l].reshape(1, d), alpha)
    return x2d.reshape(batch, seq, d)
```

```python
import functools
import math

import jax
import jax.numpy as jnp
from jax import lax
from jax.experimental import pallas as pl
from jax.experimental.pallas import tpu as pltpu

D_MODEL = 1024
HEAD_DIM = 64
N_HEADS_SB = 8
N_HEADS_DIL = 8
D_SB = N_HEADS_SB * HEAD_DIM
D_DIL = N_HEADS_DIL * HEAD_DIM
DIL_PATTERNS = ((128, 1), (512, 4), (2048, 16))
BLOCK = 128
N_BUCKETS = 32
MAX_DISTANCE = 2048
N_EXPERTS = 16
N_GROUPS = 4
EXPERTS_PER_GROUP = N_EXPERTS // N_GROUPS
D_FF_EXPERT = 1024
LN_EPS = 1e-5
NEG_INF = -1e30

LANES = 128
N_PAIRS_SB = D_SB // LANES
N_PAIRS_DIL = D_DIL // LANES
VMEM_LIMIT = 56 * 1024 * 1024

ROW_TILE = 512
SB_TQ = 256
SB_TK = 256
SB_TILES_PER_STEP = 2
SB_SKIP_BOUND = -110.0
DIL_SUPER = 2048
DIL_UNITS = DIL_SUPER // BLOCK
TOKEN_TILE_ROWS = D_MODEL // LANES
MOE_TM = 256
PERM_TP = 512
DMA_UNROLL = 8

_NT = (((1,), (1,)), ((), ()))


def _bf16(x):
    return x.astype(jnp.bfloat16)


def _split_bf16(x):
    hi = _bf16(x)
    lo = _bf16(x - hi.astype(jnp.float32))
    return hi, lo


def _dot(a, b):
    return jnp.dot(a, b, preferred_element_type=jnp.float32)


def _dot_nt(a, b):
    return lax.dot_general(a, b, _NT, preferred_element_type=jnp.float32)


def _head_rms_gain(o, g):
    r = lax.broadcasted_iota(jnp.int32, (LANES, LANES), 0) // HEAD_DIM
    c = lax.broadcasted_iota(jnp.int32, (LANES, LANES), 1) // HEAD_DIM
    same_head = jnp.where(r == c, 1.0 / HEAD_DIM, 0.0).astype(jnp.bfloat16)
    hi, lo = _split_bf16(o * o)
    ms = _dot(hi, same_head) + _dot(lo, same_head)
    return o * lax.rsqrt(ms + 1e-6) * g


def _layer_norm(x, g, b):
    mu = jnp.mean(x, axis=-1, keepdims=True)
    xc = x - mu
    var = jnp.mean(xc * xc, axis=-1, keepdims=True)
    return xc * lax.rsqrt(var + LN_EPS) * g + b


def _in_proj_kernel(x_ref, w_ref, sb_ref, dl_ref):
    x = _bf16(x_ref[...])
    scale = HEAD_DIM ** -0.5
    col = 0
    for o_ref, n_q in ((sb_ref, N_PAIRS_SB), (dl_ref, N_PAIRS_DIL)):
        for j in range(0, o_ref.shape[0], 2):
            res = _dot(x, w_ref[:, col:col + 2 * LANES])
            col += 2 * LANES
            for jj in range(2):
                blk = res[:, jj * LANES:(jj + 1) * LANES]
                if j + jj < n_q:
                    blk = blk * scale
                o_ref[j + jj] = blk.astype(o_ref.dtype)


def _in_proj(x2d, w):
    n, d = x2d.shape
    slabs = lambda k: pl.BlockSpec((k, ROW_TILE, LANES), lambda i: (0, i, 0))
    return pl.pallas_call(
        _in_proj_kernel,
        grid=(n // ROW_TILE,),
        in_specs=[pl.BlockSpec((ROW_TILE, d), lambda i: (i, 0)),
                  pl.BlockSpec((d, w.shape[1]), lambda i: (0, 0))],
        out_specs=[slabs(3 * N_PAIRS_SB), slabs(3 * N_PAIRS_DIL)],
        out_shape=[jax.ShapeDtypeStruct((3 * N_PAIRS_SB, n, LANES), jnp.bfloat16),
                   jax.ShapeDtypeStruct((3 * N_PAIRS_DIL, n, LANES), jnp.float32)],
        compiler_params=pltpu.CompilerParams(dimension_semantics=("arbitrary",),
                                             vmem_limit_bytes=VMEM_LIMIT),
        name="in_proj",
    )(x2d, w)


def _sb_kernel(q_ref, k_ref, v_ref, g_ref, o_ref, carry_ref, acc_ref):
    step = pl.program_id(2)
    lane = lax.broadcasted_iota(jnp.int32, (1, LANES), 1)
    first_head = lane < HEAD_DIM

    jj = lax.broadcasted_iota(jnp.int32, (SB_TK, SB_TK + LANES), 0)
    ss = lax.broadcasted_iota(jnp.int32, (SB_TK, SB_TK + LANES), 1)
    suffix = jnp.where((jj > ss) | (ss >= SB_TK), 1.0, 0.0).astype(jnp.bfloat16)

    tq_i = lax.broadcasted_iota(jnp.int32, (2 * SB_TQ, SB_TK), 0) % SB_TQ
    ts_i = lax.broadcasted_iota(jnp.int32, (2 * SB_TQ, SB_TK), 1)
    causal = ts_i < tq_i

    def block(q2, kb, diag, carry, acc, weight=None):
        off = pl.multiple_of(kb * SB_TK, SB_TK)
        kblk = k_ref[pl.ds(off, SB_TK), :]
        vblk = v_ref[pl.ds(off, SB_TK), :]
        z = _dot_nt(q2, kblk)
        sp = jnp.log(1.0 + jnp.exp(-jnp.abs(z)))
        log_beta = jnp.minimum(z, 0.0) - sp
        log_1m = log_beta - z
        if diag:
            log_1m = jnp.where(causal, log_1m, 0.0)
        sfx = _dot(_bf16(log_1m), suffix)
        between = sfx[:, :SB_TK] + jnp.concatenate([carry] * (SB_TK // LANES), axis=1)
        a = jnp.exp(log_beta + between)
        if diag:
            a = jnp.where(causal, a, 0.0)
        out = _dot(_bf16(a), vblk)
        tot = sfx[:, SB_TK:]
        if weight is not None:
            out = out * weight
            tot = tot * weight
        return carry + tot, acc + out

    zeros = jnp.zeros((2 * SB_TQ, LANES), jnp.float32)
    tiles = []
    for t in range(SB_TILES_PER_STEP):
        i = step * SB_TILES_PER_STEP + t
        q = q_ref[t * SB_TQ:(t + 1) * SB_TQ, :]
        zero = jnp.zeros_like(q)
        q2 = jnp.concatenate([jnp.where(first_head, q, zero), jnp.where(first_head, zero, q)], axis=0)
        carry, acc = block(q2, i, True, zeros, zeros)
        has_prev = jnp.where(i >= 1, 1.0, 0.0)
        carry, acc = block(q2, jnp.maximum(i - 1, 0), False, carry, acc, weight=has_prev)
        carry_ref[t] = carry
        acc_ref[t] = acc
        tiles.append((i, q2, jnp.max(carry)))

    for t, (i, q2, mx0) in enumerate(tiles):
        def cond(state):
            kb, mx = state
            return jnp.logical_and(kb >= 0, mx > SB_SKIP_BOUND)

        def body(state, t=t, q2=q2):
            kb, _ = state
            c, a = block(q2, kb, False, carry_ref[t], acc_ref[t])
            carry_ref[t] = c
            acc_ref[t] = a
            return kb - 1, jnp.max(c)

        lax.while_loop(cond, body, (i - 2, mx0))

        o = jnp.where(first_head, acc_ref[t, :SB_TQ, :], acc_ref[t, SB_TQ:, :])
        o_ref[t * SB_TQ:(t + 1) * SB_TQ, :] = _head_rms_gain(o, g_ref[...]).astype(o_ref.dtype)


def _sb_attention(proj, g, batch, seq):
    n = batch * seq
    rows = SB_TQ * SB_TILES_PER_STEP
    nq = seq // rows
    return pl.pallas_call(
        _sb_kernel,
        grid=(batch, N_PAIRS_SB, nq),
        in_specs=[
            pl.BlockSpec((None, rows, LANES), lambda b, p, i: (p, b * nq + i, 0)),
            pl.BlockSpec((None, seq, LANES), lambda b, p, i: (N_PAIRS_SB + p, b, 0)),
            pl.BlockSpec((None, seq, LANES), lambda b, p, i: (2 * N_PAIRS_SB + p, b, 0)),
            pl.BlockSpec((None, 1, LANES), lambda b, p, i: (p, 0, 0)),
        ],
        out_specs=pl.BlockSpec((None, rows, LANES), lambda b, p, i: (p, b * nq + i, 0)),
        out_shape=jax.ShapeDtypeStruct((N_PAIRS_SB, n, LANES), jnp.bfloat16),
        scratch_shapes=[pltpu.VMEM((SB_TILES_PER_STEP, 2 * SB_TQ, LANES), jnp.float32),
                        pltpu.VMEM((SB_TILES_PER_STEP, 2 * SB_TQ, LANES), jnp.float32)],
        compiler_params=pltpu.CompilerParams(
            dimension_semantics=("arbitrary", "arbitrary", "arbitrary"),
            vmem_limit_bytes=VMEM_LIMIT),
        name="sb_attention",
    )(proj, proj, proj, g)


def _dil_kernel(tbias_ref, q_ref, k_ref, v_ref, g_ref, o_ref,
                bias_ref, obr_ref, lse_ref):
    b = pl.program_id(0)
    p = pl.program_id(1)
    i = pl.program_id(2)
    n_br = len(DIL_PATTERNS)

    qi = lax.broadcasted_iota(jnp.int32, (BLOCK, 2 * BLOCK), 0)
    kj = lax.broadcasted_iota(jnp.int32, (BLOCK, 2 * BLOCK), 1)
    steps = qi + BLOCK - kj

    assert all(window // dilation == BLOCK for window, dilation in DIL_PATTERNS)
    valid = (steps >= 0) & (steps <= BLOCK)

    @pl.when((b == 0) & (p == 0) & (i == 0))
    def _():
        n_tiles = n_br * N_HEADS_DIL

        def per_tile(s, _):
            tile = tbias_ref[s]
            rows = pl.ds(pl.multiple_of(s * BLOCK, BLOCK), BLOCK)
            bias_ref[rows, :] = jnp.where(valid, tile, NEG_INF)
            rows = pl.ds(pl.multiple_of((n_tiles + s) * BLOCK, BLOCK), BLOCK)
            bias_ref[rows, :] = jnp.where(valid & (kj >= BLOCK), tile, NEG_INF)
            return 0
        lax.fori_loop(0, n_br * N_HEADS_DIL, per_tile, 0)

    lane = lax.broadcasted_iota(jnp.int32, (1, LANES), 1)
    first_head = lane < HEAD_DIM
    t0 = i * DIL_SUPER

    def strided(ref, start, dilation):
        if dilation == 1:
            return ref[pl.ds(start, BLOCK), :]
        return ref[pl.ds(start, BLOCK, stride=dilation), :]

    for gidx, (window, dilation) in enumerate(DIL_PATTERNS):
        units_per_residue = DIL_UNITS // dilation
        shift = units_per_residue.bit_length() - 1

        def unit(u, _, gidx=gidx, dilation=dilation, units_per_residue=units_per_residue, shift=shift):
            r = lax.shift_right_logical(u, shift)
            cb = u & (units_per_residue - 1)
            sq = r + cb * (dilation * BLOCK)
            cur = t0 + sq
            prev = cur - dilation * BLOCK
            prev_ok = prev >= 0
            prev_c = jnp.where(prev_ok, prev, cur)
            no_prev = jnp.where(prev_ok, 0, 1)

            qb = _bf16(strided(q_ref, sq, dilation))
            kk = _bf16(jnp.concatenate([strided(k_ref, prev_c, dilation),
                                        strided(k_ref, cur, dilation)], axis=0))
            vv = _bf16(jnp.concatenate([strided(v_ref, prev_c, dilation),
                                        strided(v_ref, cur, dilation)], axis=0))
            zero = jnp.zeros_like(qb)
            q2 = jnp.concatenate([jnp.where(first_head, qb, zero),
                                  jnp.where(first_head, zero, qb)], axis=0)
            z = _dot_nt(q2, kk)
            slot = no_prev * (n_br * N_HEADS_DIL) + gidx * N_HEADS_DIL + 2 * p
            lg = z + bias_ref[pl.ds(pl.multiple_of(slot * BLOCK, 2 * BLOCK), 2 * BLOCK), :]
            m = jnp.max(lg, axis=-1, keepdims=True)
            pe = jnp.exp(lg - m)
            den = jnp.sum(pe, axis=-1, keepdims=True)
            out = _dot(_bf16(pe), vv) / den
            lse = m + jnp.log(den)
            o_tile = jnp.where(first_head, out[:BLOCK], out[BLOCK:])
            l_tile = jnp.where(first_head, lse[:BLOCK], lse[BLOCK:])
            if dilation == 1:
                idx = pl.ds(sq, BLOCK)
            else:
                idx = pl.ds(sq, BLOCK, stride=dilation)
            obr_ref.at[gidx][idx, :] = o_tile
            lse_ref.at[gidx][idx, :] = l_tile
            return 0

        lax.fori_loop(0, DIL_UNITS, unit, 0, unroll=8)

    chunk = 256

    def combine(c, _):
        rows = pl.ds(pl.multiple_of(c * chunk, chunk), chunk)
        ls = [lse_ref[gidx, rows, :] for gidx in range(n_br)]
        m = functools.reduce(jnp.maximum, ls)
        es = [jnp.exp(l - m) for l in ls]
        num = sum(e * obr_ref[gidx, rows, :] for gidx, e in enumerate(es))
        o = num / sum(es)
        o_ref[rows, :] = _head_rms_gain(o, g_ref[...]).astype(o_ref.dtype)
        return 0

    lax.fori_loop(0, DIL_SUPER // chunk, combine, 0)


def _dil_attention(proj, g, tile_bias, batch, seq):
    n = batch * seq
    ns = seq // DIL_SUPER
    n_br = len(DIL_PATTERNS)
    return pl.pallas_call(
        _dil_kernel,
        grid=(batch, N_PAIRS_DIL, ns),
        in_specs=[
            pl.BlockSpec((n_br * N_HEADS_DIL, BLOCK, 2 * BLOCK), lambda b, p, i: (0, 0, 0)),
            pl.BlockSpec((None, DIL_SUPER, LANES), lambda b, p, i: (p, b * ns + i, 0)),
            pl.BlockSpec((None, seq, LANES), lambda b, p, i: (N_PAIRS_DIL + p, b, 0)),
            pl.BlockSpec((None, seq, LANES), lambda b, p, i: (2 * N_PAIRS_DIL + p, b, 0)),
            pl.BlockSpec((None, 1, LANES), lambda b, p, i: (p, 0, 0)),
        ],
        out_specs=pl.BlockSpec((None, DIL_SUPER, LANES), lambda b, p, i: (p, b * ns + i, 0)),
        out_shape=jax.ShapeDtypeStruct((N_PAIRS_DIL, n, LANES), jnp.bfloat16),
        scratch_shapes=[pltpu.VMEM((2 * n_br * N_HEADS_DIL * BLOCK, 2 * BLOCK), jnp.float32),
                        pltpu.VMEM((n_br, DIL_SUPER, LANES), jnp.float32),
                        pltpu.VMEM((n_br, DIL_SUPER, LANES), jnp.float32)],
        compiler_params=pltpu.CompilerParams(
            dimension_semantics=("arbitrary", "arbitrary", "arbitrary"),
            vmem_limit_bytes=VMEM_LIMIT),
        name="dilated_attention",
    )(tile_bias, proj, proj, proj, g)


def _router(x1, wrt_hi, wrt_lo, br):
    tm = x1.shape[0]
    xh, xl = _split_bf16(x1)
    lt = _dot_nt(wrt_hi, xh) + _dot_nt(wrt_hi, xl) + _dot_nt(wrt_lo, xh) + br
    rows = [lt[e:e + 1, :] for e in range(N_EXPERTS)]
    m = functools.reduce(jnp.maximum, rows)
    ex = [jnp.exp(r - m) for r in rows]
    tot = functools.reduce(lambda a, c: a + c, ex)
    probs = [e / tot for e in ex]

    keep, score = [], []
    for gi in range(N_GROUPS):
        members = range(gi * EXPERTS_PER_GROUP, (gi + 1) * EXPERTS_PER_GROUP)
        s = None
        for e in members:
            rank = None
            for o in members:
                if o == e:
                    continue
                beats = (probs[o] > probs[e]) | ((probs[o] == probs[e]) & (o < e))
                beats = beats.astype(jnp.int32)
                rank = beats if rank is None else rank + beats
            k = rank < 2
            keep.append(k)
            contrib = jnp.where(k, probs[e], 0.0)
            s = contrib if s is None else s + contrib
        score.append(s)

    sel, gates = [], []
    for gi in range(N_GROUPS):
        chosen = None
        for o in range(N_GROUPS):
            if o == gi:
                continue
            c = (score[gi] > score[o]) if o < gi else (score[gi] >= score[o])
            chosen = c if chosen is None else (chosen & c)
        for e in range(gi * EXPERTS_PER_GROUP, (gi + 1) * EXPERTS_PER_GROUP):
            s = keep[e] & chosen
            sel.append(s)
            gates.append(jnp.where(s, probs[e] / score[gi], 0.0))

    idx0 = functools.reduce(jnp.minimum, [jnp.where(sel[e], e, N_EXPERTS) for e in range(N_EXPERTS)])
    idx1 = functools.reduce(jnp.maximum, [jnp.where(sel[e], e, -1) for e in range(N_EXPERTS)])
    g0 = sum(jnp.where(idx0 == e, gates[e], 0.0) for e in range(N_EXPERTS))
    g1 = sum(jnp.where(idx1 == e, gates[e], 0.0) for e in range(N_EXPERTS))

    row_id = lax.broadcasted_iota(jnp.int32, (N_EXPERTS, tm), 0)
    selmat = jnp.zeros((N_EXPERTS, tm), jnp.float32)
    for e in range(N_EXPERTS):
        selmat = jnp.where((row_id == e) & sel[e], 1.0, selmat)
    return idx0, idx1, g0, g1, selmat


def _to_token_tiles(ref, x):
    rows = x.shape[0]
    for s in range(TOKEN_TILE_ROWS):
        ref[pl.ds(s, rows, stride=TOKEN_TILE_ROWS), :] = x[:, s * LANES:(s + 1) * LANES]


def _from_token_tiles(ref, rows):
    return jnp.concatenate([ref[pl.ds(s, rows, stride=TOKEN_TILE_ROWS), :]
                            for s in range(TOKEN_TILE_ROWS)], axis=1)


def _out_proj_kernel(x_ref, ysb_ref, ydl_ref, w_ref, lng_ref, lnb_ref,
                     wrh_ref, wrl_ref, br_ref,
                     x1_ref, x1t_ref, meta_ref, gcol_ref, cnt_ref, carry_ref, *, alpha):
    i = pl.program_id(0)
    tm = x_ref.shape[0]

    @pl.when(i == 0)
    def _():
        carry_ref[...] = jnp.zeros_like(carry_ref)

    y = jnp.concatenate([ysb_ref[j] for j in range(N_PAIRS_SB)]
                        + [ydl_ref[j] for j in range(N_PAIRS_DIL)], axis=1)
    h = _dot(y, w_ref[...])
    x1 = _layer_norm(alpha * x_ref[...] + h, lng_ref[...], lnb_ref[...])
    x1_ref[...] = x1
    _to_token_tiles(x1t_ref, x1)

    idx0, idx1, g0, g1, selmat = _router(x1, wrh_ref[...], wrl_ref[...], br_ref[...])

    jj = lax.broadcasted_iota(jnp.int32, (tm, tm + LANES), 0)
    ss = lax.broadcasted_iota(jnp.int32, (tm, tm + LANES), 1)
    before = jnp.where((jj < ss) | (ss >= tm), 1.0, 0.0).astype(jnp.bfloat16)
    pc = _dot(_bf16(selmat), before)
    carry = carry_ref[...]
    rankmat = pc[:, :tm] + jnp.concatenate([carry] * (tm // LANES), axis=1)
    carry = carry + pc[:, tm:]
    carry_ref[...] = carry
    cnt_ref[...] = carry

    rank0 = sum(jnp.where(idx0 == e, rankmat[e:e + 1, :], 0.0) for e in range(N_EXPERTS))
    rank1 = sum(jnp.where(idx1 == e, rankmat[e:e + 1, :], 0.0) for e in range(N_EXPERTS))

    row8 = lax.broadcasted_iota(jnp.int32, (8, tm), 0)
    meta = jnp.zeros((8, tm), jnp.int32)
    for r, v in enumerate((idx0, idx1, rank0.astype(jnp.int32), rank1.astype(jnp.int32))):
        meta = jnp.where(row8 == r, v, meta)
    meta_ref[...] = meta

    rowl = lax.broadcasted_iota(jnp.int32, (LANES, tm), 0)
    gt = jnp.where(rowl == 0, g0, jnp.where(rowl == 1, g1, 0.0))
    gcol_ref[...] = gt.T


def _out_proj(x2d, y_sb, y_dl, w_out, ln_g, ln_b, wrt_hi, wrt_lo, br, alpha):
    n, d = x2d.shape
    const = lambda i: (0, 0)
    return pl.pallas_call(
        functools.partial(_out_proj_kernel, alpha=alpha),
        grid=(n // ROW_TILE,),
        in_specs=[
            pl.BlockSpec((ROW_TILE, d), lambda i: (i, 0)),
            pl.BlockSpec((N_PAIRS_SB, ROW_TILE, LANES), lambda i: (0, i, 0)),
            pl.BlockSpec((N_PAIRS_DIL, ROW_TILE, LANES), lambda i: (0, i, 0)),
            pl.BlockSpec((d, d), const),
            pl.BlockSpec((1, d), const),
            pl.BlockSpec((1, d), const),
            pl.BlockSpec((N_EXPERTS, d), const),
            pl.BlockSpec((N_EXPERTS, d), const),
            pl.BlockSpec((N_EXPERTS, 1), const),
        ],
        out_specs=[pl.BlockSpec((ROW_TILE, d), lambda i: (i, 0)),
                   pl.BlockSpec((ROW_TILE * TOKEN_TILE_ROWS, LANES), lambda i: (i, 0)),
                   pl.BlockSpec((8, ROW_TILE), lambda i: (0, i)),
                   pl.BlockSpec((ROW_TILE, LANES), lambda i: (i, 0)),
                   pl.BlockSpec((N_EXPERTS, LANES), const)],
        out_shape=[jax.ShapeDtypeStruct((n, d), jnp.float32),
                   jax.ShapeDtypeStruct((n * TOKEN_TILE_ROWS, LANES), jnp.float32),
                   jax.ShapeDtypeStruct((8, n), jnp.int32),
                   jax.ShapeDtypeStruct((n, LANES), jnp.float32),
                   jax.ShapeDtypeStruct((N_EXPERTS, LANES), jnp.float32)],
        scratch_shapes=[pltpu.VMEM((N_EXPERTS, LANES), jnp.float32)],
        compiler_params=pltpu.CompilerParams(dimension_semantics=("arbitrary",),
                                             vmem_limit_bytes=VMEM_LIMIT),
        name="out_proj_ln_router",
    )(x2d, y_sb, y_dl, w_out, ln_g, ln_b, wrt_hi, wrt_lo, br)


def _segment_offsets(cnt_ref, off_ref):
    acc = jnp.int32(0)
    padded = []
    for e in range(N_EXPERTS):
        off_ref[e] = acc
        pc = lax.div(cnt_ref[e] + (MOE_TM - 1), MOE_TM) * MOE_TM
        padded.append(pc)
        acc = acc + pc
    return padded, acc


def _row_copy(src_ref, src_row, dst_ref, dst_row, sem):
    def tile(row):
        if isinstance(row, int):
            return pl.ds(row * TOKEN_TILE_ROWS, TOKEN_TILE_ROWS)
        return pl.ds(pl.multiple_of(row * TOKEN_TILE_ROWS, TOKEN_TILE_ROWS), TOKEN_TILE_ROWS)
    return pltpu.make_async_copy(src_ref.at[tile(src_row), :], dst_ref.at[tile(dst_row), :], sem)


def _pos_kernel(meta_ref, cnt_ref, pos_ref):
    cnt = cnt_ref[...]
    padded = jnp.floor((cnt + (MOE_TM - 1)) * (1.0 / MOE_TM)) * MOE_TM
    meta = meta_ref[...]
    rows = []
    for k in range(2):
        e = meta[k:k + 1, :]
        start = jnp.zeros(e.shape, jnp.float32)
        off = jnp.zeros((1, 1), jnp.float32)
        for ex in range(N_EXPERTS):
            start = jnp.where(e == ex, off, start)
            off = off + padded[ex:ex + 1, 0:1]
        rows.append(start.astype(jnp.int32) + meta[2 + k:3 + k, :])
    row8 = lax.broadcasted_iota(jnp.int32, meta.shape, 0)
    pos_ref[...] = jnp.where(row8 == 0, rows[0], jnp.where(row8 == 1, rows[1], 0))


def _positions(meta, cnt):
    n = meta.shape[1]
    tile = 2048
    return pl.pallas_call(
        _pos_kernel,
        grid=(n // tile,),
        in_specs=[pl.BlockSpec((8, tile), lambda i: (0, i)),
                  pl.BlockSpec((N_EXPERTS, LANES), lambda i: (0, 0))],
        out_specs=pl.BlockSpec((8, tile), lambda i: (0, i)),
        out_shape=jax.ShapeDtypeStruct((8, n), jnp.int32),
        compiler_params=pltpu.CompilerParams(dimension_semantics=("arbitrary",)),
        name="moe_pos",
    )(meta, cnt)


def _plan_kernel(cnt_ref, pos0_ref, pos1_ref, tok_ref, te_ref, off_ref, *, max_tiles):
    i = pl.program_id(0)
    tp = pos0_ref.shape[0]
    padded, total = _segment_offsets(cnt_ref, off_ref)

    @pl.when(i == 0)
    def _():
        def clear(k, c):
            tok_ref[k] = 0
            return c
        for e in range(N_EXPERTS):
            lax.fori_loop(off_ref[e] + cnt_ref[e], off_ref[e] + padded[e], clear, 0)
        lax.fori_loop(total, max_tiles * MOE_TM, clear, 0)

        n_tiles = lax.div(total, MOE_TM)
        for e in range(N_EXPERTS):
            first = lax.div(off_ref[e], MOE_TM)
            count = lax.div(padded[e], MOE_TM)

            def fill(k, c, first=first, e=e):
                te_ref[first + k] = e
                return c
            lax.fori_loop(0, count, fill, 0)
        last = te_ref[n_tiles - 1]

        def fill_tail(k, c):
            te_ref[k] = last
            return c
        lax.fori_loop(n_tiles, max_tiles, fill_tail, 0)
        te_ref[max_tiles] = n_tiles

    def place(j, c):
        token = i * tp + j
        tok_ref[pos0_ref[j]] = token
        tok_ref[pos1_ref[j]] = token
        return c
    lax.fori_loop(0, tp, place, 0, unroll=8)


def _plan(counts, pos0, pos1, max_tiles):
    n = pos0.shape[0]
    smem = pl.BlockSpec(memory_space=pltpu.SMEM)
    vec = pl.BlockSpec((PERM_TP,), lambda i: (i,), memory_space=pltpu.SMEM)
    return pl.pallas_call(
        functools.partial(_plan_kernel, max_tiles=max_tiles),
        grid=(n // PERM_TP,),
        in_specs=[smem, vec, vec],
        out_specs=[smem, smem],
        out_shape=[jax.ShapeDtypeStruct((max_tiles * MOE_TM,), jnp.int32),
                   jax.ShapeDtypeStruct((max_tiles + 1,), jnp.int32)],
        scratch_shapes=[pltpu.SMEM((N_EXPERTS,), jnp.int32)],
        compiler_params=pltpu.CompilerParams(dimension_semantics=("arbitrary",)),
        name="moe_plan",
    )(counts, pos0, pos1)


def _experts_kernel(tok_ref, te_ref, layer_ref, x_ref, wg_ref, wu_ref, wd_ref, ys_ref,
                    xbuf_ref, wgb_ref, wub_ref, wdb_ref, gsem, *, max_tiles):
    i = pl.program_id(0)
    n_tiles = te_ref[max_tiles]
    slot = i & 1
    tile_rows = MOE_TM * TOKEN_TILE_ROWS

    def gather(tile, s):
        base = tile * MOE_TM

        def body(j8, c):
            for u in range(DMA_UNROLL):
                j = j8 * DMA_UNROLL + u
                _row_copy(x_ref, tok_ref[base + j], xbuf_ref.at[s], j, gsem.at[s]).start(priority=u % 2)
            return c
        lax.fori_loop(0, MOE_TM // DMA_UNROLL, body, 0)

    @pl.when(i == 0)
    def _():
        gather(0, 0)

    @pl.when(i + 1 < n_tiles)
    def _():
        gather(i + 1, 1 - slot)

    @pl.when(i < n_tiles)
    def _():
        pltpu.make_async_copy(x_ref.at[pl.ds(0, tile_rows), :], xbuf_ref.at[slot],
                              gsem.at[slot]).wait()

        @pl.when((i == 0) | (te_ref[i] != te_ref[jnp.maximum(i - 1, 0)]))
        def _():
            wgb_ref[...] = _bf16(wg_ref[...])
            wub_ref[...] = _bf16(wu_ref[...])
            wdb_ref[...] = _bf16(wd_ref[...])

        xb = _bf16(_from_token_tiles(xbuf_ref.at[slot], MOE_TM))
        gt = _dot(xb, wgb_ref[...])
        up = _dot(xb, wub_ref[...])
        h = gt / (1.0 + jnp.exp(-gt)) * up
        _to_token_tiles(ys_ref, _dot(_bf16(h), wdb_ref[...]))

    @pl.when(i >= n_tiles)
    def _():
        ys_ref[...] = jnp.zeros_like(ys_ref)


def _experts(tok, te, x1t, w_gate, w_up, w_down, layer, max_tiles):
    d, f = w_gate.shape[-2:]
    rows = MOE_TM * TOKEN_TILE_ROWS
    expert = lambda i, tok, te, lay: (lay[0], te[i], 0, 0)
    return pl.pallas_call(
        functools.partial(_experts_kernel, max_tiles=max_tiles),
        grid_spec=pltpu.PrefetchScalarGridSpec(
            num_scalar_prefetch=3,
            grid=(max_tiles,),
            in_specs=[pl.BlockSpec(memory_space=pl.ANY),
                      pl.BlockSpec((None, None, d, f), expert),
                      pl.BlockSpec((None, None, d, f), expert),
                      pl.BlockSpec((None, None, f, d), expert)],
            out_specs=pl.BlockSpec((rows, LANES), lambda i, tok, te, lay: (i, 0)),
            scratch_shapes=[pltpu.VMEM((2, rows, LANES), jnp.float32),
                            pltpu.VMEM((d, f), jnp.bfloat16),
                            pltpu.VMEM((d, f), jnp.bfloat16),
                            pltpu.VMEM((f, d), jnp.bfloat16),
                            pltpu.SemaphoreType.DMA((2,))]),
        out_shape=jax.ShapeDtypeStruct((max_tiles * rows, LANES), jnp.float32),
        compiler_params=pltpu.CompilerParams(dimension_semantics=("arbitrary",),
                                             vmem_limit_bytes=VMEM_LIMIT),
        name="moe_experts",
    )(tok, te, jnp.full((1,), layer, jnp.int32), x1t, w_gate, w_up, w_down)


def _combine_kernel(pos0_ref, pos1_ref, x_ref, gcol_ref, lng_ref, lnb_ref, ys_ref, o_ref,
                    ybuf_ref, sem, *, alpha):
    i = pl.program_id(0)
    n_steps = pl.num_programs(0)
    tm = x_ref.shape[0]
    slot = i & 1
    tile_rows = tm * TOKEN_TILE_ROWS

    def gather(step, s):
        base = step * tm

        def body(j, c):
            _row_copy(ys_ref, pos0_ref[base + j], ybuf_ref.at[s, 0], j, sem.at[s]).start(priority=0)
            _row_copy(ys_ref, pos1_ref[base + j], ybuf_ref.at[s, 1], j, sem.at[s]).start(priority=1)
            return c
        lax.fori_loop(0, tm, body, 0, unroll=DMA_UNROLL)

    @pl.when(i == 0)
    def _():
        gather(0, 0)

    @pl.when(i + 1 < n_steps)
    def _():
        gather(i + 1, 1 - slot)

    for k in range(2):
        pltpu.make_async_copy(ys_ref.at[pl.ds(0, tile_rows), :], ybuf_ref.at[slot, k],
                              sem.at[slot]).wait()

    lane = lax.broadcasted_iota(jnp.int32, (1, LANES), 1)
    gc = gcol_ref[...]
    g0 = jnp.sum(jnp.where(lane == 0, gc, 0.0), axis=-1, keepdims=True)
    g1 = jnp.sum(jnp.where(lane == 1, gc, 0.0), axis=-1, keepdims=True)
    y = (g0 * _from_token_tiles(ybuf_ref.at[slot, 0], tm)
         + g1 * _from_token_tiles(ybuf_ref.at[slot, 1], tm))
    o_ref[...] = _layer_norm(alpha * x_ref[...] + y, lng_ref[...], lnb_ref[...])


def _combine(pos0, pos1, x1, ys, gcol, ln_g, ln_b, alpha):
    n, d = x1.shape
    const = lambda i, p0, p1: (0, 0)
    tile = lambda i, p0, p1: (i, 0)
    return pl.pallas_call(
        functools.partial(_combine_kernel, alpha=alpha),
        grid_spec=pltpu.PrefetchScalarGridSpec(
            num_scalar_prefetch=2,
            grid=(n // ROW_TILE,),
            in_specs=[pl.BlockSpec((ROW_TILE, d), tile),
                      pl.BlockSpec((ROW_TILE, LANES), tile),
                      pl.BlockSpec((1, d), const), pl.BlockSpec((1, d), const),
                      pl.BlockSpec(memory_space=pl.ANY)],
            out_specs=pl.BlockSpec((ROW_TILE, d), tile),
            scratch_shapes=[pltpu.VMEM((2, 2, ROW_TILE * TOKEN_TILE_ROWS, LANES), jnp.float32),
                            pltpu.SemaphoreType.DMA((2,))]),
        out_shape=jax.ShapeDtypeStruct((n, d), jnp.float32),
        compiler_params=pltpu.CompilerParams(dimension_semantics=("arbitrary",),
                                             vmem_limit_bytes=VMEM_LIMIT),
        name="moe_combine_ln",
    )(pos0, pos1, x1, gcol, ln_g, ln_b, ys)


def _moe(x1, x1t, meta, gcol, cnt, w_gate, w_up, w_down, layer, ln_g, ln_b, alpha):
    n = x1.shape[0]
    max_tiles = (2 * n + N_EXPERTS * (MOE_TM - 1)) // MOE_TM
    counts = cnt[:, 0].astype(jnp.int32)
    pos = _positions(meta, cnt)
    pos0, pos1 = pos[0], pos[1]
    tok, te = _plan(counts, pos0, pos1, max_tiles)
    ys = _experts(tok, te, x1t, w_gate, w_up, w_down, layer, max_tiles)
    return _combine(pos0, pos1, x1, ys, gcol, ln_g, ln_b, alpha)


def _t5_bucket(dist):
    max_exact = N_BUCKETS // 2
    d = jnp.maximum(dist, 0)
    large = max_exact + (jnp.log(jnp.maximum(d, 1).astype(jnp.float32) / max_exact)
                         / math.log(MAX_DISTANCE / max_exact) * (N_BUCKETS - max_exact)).astype(jnp.int32)
    large = jnp.minimum(large, N_BUCKETS - 1)
    return jnp.where(d < max_exact, d, large)


def _tile_bias_kernel(w_ref, o_ref):
    w = w_ref[0]
    for q in range(BLOCK):
        o_ref[q:q + 1, :] = w[:, BLOCK - 1 - q:3 * BLOCK - 1 - q]


def _tile_bias(rel_bias):
    dist = jnp.arange(BLOCK + 1)
    rows = jnp.stack([rel_bias[_t5_bucket(dist * dilation)] for _, dilation in DIL_PATTERNS])
    n_br, _, heads = rows.shape
    table = rows.astype(jnp.float32).transpose(0, 2, 1).reshape(n_br * heads, BLOCK + 1)
    w = jnp.pad(table[:, ::-1], ((0, 0), (BLOCK - 1, BLOCK)))[:, None, :]
    return pl.pallas_call(
        _tile_bias_kernel,
        grid=(n_br * heads,),
        in_specs=[pl.BlockSpec((1, 1, 3 * BLOCK), lambda s: (s, 0, 0))],
        out_specs=pl.BlockSpec((None, BLOCK, 2 * BLOCK), lambda s: (s, 0, 0)),
        out_shape=jax.ShapeDtypeStruct((n_br * heads, BLOCK, 2 * BLOCK), jnp.float32),
        compiler_params=pltpu.CompilerParams(dimension_semantics=("arbitrary",)),
        name="tile_bias",
    )(w)


def kernel(x, w_in, g_sb, g_dil, w_out, ln1_g, ln1_b, ln2_g, ln2_b, rel_bias,
           w_router, b_router, w_gate, w_up, w_down):
    batch, seq, d = x.shape
    depth = w_in.shape[0]
    alpha = (2.0 * depth) ** 0.25
    n = batch * seq
    x2d = x.reshape(n, d)

    tile_bias = _tile_bias(rel_bias)
    wrt =w_router.T.astype(jnp.float32)
    wrt_hi = _bf16(wrt)
    wrt_lo = _bf16(wrt - wrt_hi.astype(jnp.float32))
    br = b_router.astype(jnp.float32).reshape(N_EXPERTS, 1)

    for l in range(depth):
        w_in_l = _bf16(w_in[l])
        proj_sb, proj_dl = _in_proj(x2d, w_in_l)
        y_sb = _sb_attention(proj_sb, g_sb[l].reshape(N_PAIRS_SB, 1, LANES), batch, seq)
        y_dl = _dil_attention(proj_dl, g_dil[l].reshape(N_PAIRS_DIL, 1, LANES),
                              tile_bias, batch, seq)
        x1, x1t, meta, gcol, cnt = _out_proj(x2d, y_sb, y_dl, _bf16(w_out[l]),
                                             ln1_g[l].reshape(1, d), ln1_b[l].reshape(1, d),
                                             wrt_hi, wrt_lo, br, alpha)
        x2d = _moe(x1, x1t, meta, gcol, cnt, w_gate, w_up, w_down, l,
                   ln2_g[l].reshape(1, d), ln2_b[l].reshape(1, d), alpha)
    return x2d.reshape(batch, seq, d)
```

```python
import functools
import math

import jax
import jax.numpy as jnp
from jax import lax
from jax.experimental import pallas as pl
from jax.experimental.pallas import tpu as pltpu

D_MODEL = 1024
HEAD_DIM = 64
N_HEADS_SB = 8
N_HEADS_DIL = 8
D_SB = N_HEADS_SB * HEAD_DIM
D_DIL = N_HEADS_DIL * HEAD_DIM
DIL_PATTERNS = ((128, 1), (512, 4), (2048, 16))
BLOCK = 128
N_BUCKETS = 32
MAX_DISTANCE = 2048
N_EXPERTS = 16
N_GROUPS = 4
EXPERTS_PER_GROUP = N_EXPERTS // N_GROUPS
D_FF_EXPERT = 1024
LN_EPS = 1e-5
NEG_INF = -1e30

LANES = 128
N_PAIRS_SB = D_SB // LANES
N_PAIRS_DIL = D_DIL // LANES
VMEM_LIMIT = 56 * 1024 * 1024

ROW_TILE = 512
SB_TQ = 256
SB_TK = 256
SB_TILES_PER_STEP = 2
SB_SKIP_BOUND = -110.0
DIL_SUPER = 2048
DIL_UNITS = DIL_SUPER // BLOCK
TOKEN_TILE_ROWS = D_MODEL // LANES
MOE_TM = 256
PERM_TP = 512
DMA_UNROLL = 8

_NT = (((1,), (1,)), ((), ()))


def _bf16(x):
    return x.astype(jnp.bfloat16)


def _split_bf16(x):
    hi = _bf16(x)
    lo = _bf16(x - hi.astype(jnp.float32))
    return hi, lo


def _dot(a, b):
    return jnp.dot(a, b, preferred_element_type=jnp.float32)


def _dot_nt(a, b):
    return lax.dot_general(a, b, _NT, preferred_element_type=jnp.float32)


def _head_rms_gain(o, g):
    r = lax.broadcasted_iota(jnp.int32, (LANES, LANES), 0) // HEAD_DIM
    c = lax.broadcasted_iota(jnp.int32, (LANES, LANES), 1) // HEAD_DIM
    same_head = jnp.where(r == c, 1.0 / HEAD_DIM, 0.0).astype(jnp.bfloat16)
    hi, lo = _split_bf16(o * o)
    ms = _dot(hi, same_head) + _dot(lo, same_head)
    return o * lax.rsqrt(ms + 1e-6) * g


def _layer_norm(x, g, b):
    mu = jnp.mean(x, axis=-1, keepdims=True)
    xc = x - mu
    var = jnp.mean(xc * xc, axis=-1, keepdims=True)
    return xc * lax.rsqrt(var + LN_EPS) * g + b


def _in_proj_kernel(x_ref, w_ref, sb_ref, dl_ref):
    x = _bf16(x_ref[...])
    scale = HEAD_DIM ** -0.5
    col = 0
    for o_ref, n_q in ((sb_ref, N_PAIRS_SB), (dl_ref, N_PAIRS_DIL)):
        for j in range(0, o_ref.shape[0], 2):
            res = _dot(x, w_ref[:, col:col + 2 * LANES])
            col += 2 * LANES
            for jj in range(2):
                blk = res[:, jj * LANES:(jj + 1) * LANES]
                if j + jj < n_q:
                    blk = blk * scale
                o_ref[j + jj] = blk.astype(o_ref.dtype)


def _in_proj(x2d, w):
    n, d = x2d.shape
    slabs = lambda k: pl.BlockSpec((k, ROW_TILE, LANES), lambda i: (0, i, 0))
    return pl.pallas_call(
        _in_proj_kernel,
        grid=(n // ROW_TILE,),
        in_specs=[pl.BlockSpec((ROW_TILE, d), lambda i: (i, 0)),
                  pl.BlockSpec((d, w.shape[1]), lambda i: (0, 0))],
        out_specs=[slabs(3 * N_PAIRS_SB), slabs(3 * N_PAIRS_DIL)],
        out_shape=[jax.ShapeDtypeStruct((3 * N_PAIRS_SB, n, LANES), jnp.bfloat16),
                   jax.ShapeDtypeStruct((3 * N_PAIRS_DIL, n, LANES), jnp.float32)],
        compiler_params=pltpu.CompilerParams(dimension_semantics=("arbitrary",),
                                             vmem_limit_bytes=VMEM_LIMIT),
        name="in_proj",
    )(x2d, w)


def _sb_kernel(q_ref, k_ref, v_ref, g_ref, o_ref, carry_ref, acc_ref):
    step = pl.program_id(2)
    lane = lax.broadcasted_iota(jnp.int32, (1, LANES), 1)
    first_head = lane < HEAD_DIM

    jj = lax.broadcasted_iota(jnp.int32, (SB_TK, SB_TK + LANES), 0)
    ss = lax.broadcasted_iota(jnp.int32, (SB_TK, SB_TK + LANES), 1)
    suffix = jnp.where((jj > ss) | (ss >= SB_TK), 1.0, 0.0).astype(jnp.bfloat16)

    tq_i = lax.broadcasted_iota(jnp.int32, (2 * SB_TQ, SB_TK), 0) % SB_TQ
    ts_i = lax.broadcasted_iota(jnp.int32, (2 * SB_TQ, SB_TK), 1)
    causal = ts_i < tq_i

    def block(q2, kb, diag, carry, acc, weight=None):
        off = pl.multiple_of(kb * SB_TK, SB_TK)
        kblk = k_ref[pl.ds(off, SB_TK), :]
        vblk = v_ref[pl.ds(off, SB_TK), :]
        z = _dot_nt(q2, kblk)
        sp = jnp.log(1.0 + jnp.exp(-jnp.abs(z)))
        log_beta = jnp.minimum(z, 0.0) - sp
        log_1m = log_beta - z
        if diag:
            log_1m = jnp.where(causal, log_1m, 0.0)
        sfx = _dot(_bf16(log_1m), suffix)
        between = sfx[:, :SB_TK] + jnp.concatenate([carry] * (SB_TK // LANES), axis=1)
        a = jnp.exp(log_beta + between)
        if diag:
            a = jnp.where(causal, a, 0.0)
        out = _dot(_bf16(a), vblk)
        tot = sfx[:, SB_TK:]
        if weight is not None:
            out = out * weight
            tot = tot * weight
        return carry + tot, acc + out

    zeros = jnp.zeros((2 * SB_TQ, LANES), jnp.float32)
    tiles = []
    for t in range(SB_TILES_PER_STEP):
        i = step * SB_TILES_PER_STEP + t
        q = q_ref[t * SB_TQ:(t + 1) * SB_TQ, :]
        zero = jnp.zeros_like(q)
        q2 = jnp.concatenate([jnp.where(first_head, q, zero), jnp.where(first_head, zero, q)], axis=0)
        carry, acc = block(q2, i, True, zeros, zeros)
        has_prev = jnp.where(i >= 1, 1.0, 0.0)
        carry, acc = block(q2, jnp.maximum(i - 1, 0), False, carry, acc, weight=has_prev)
        carry_ref[t] = carry
        acc_ref[t] = acc
        tiles.append((i, q2, jnp.max(carry)))

    for t, (i, q2, mx0) in enumerate(tiles):
        def cond(state):
            kb, mx = state
            return jnp.logical_and(kb >= 0, mx > SB_SKIP_BOUND)

        def body(state, t=t, q2=q2):
            kb, _ = state
            c, a = block(q2, kb, False, carry_ref[t], acc_ref[t])
            carry_ref[t] = c
            acc_ref[t] = a
            return kb - 1, jnp.max(c)

        lax.while_loop(cond, body, (i - 2, mx0))

        o = jnp.where(first_head, acc_ref[t, :SB_TQ, :], acc_ref[t, SB_TQ:, :])
        o_ref[t * SB_TQ:(t + 1) * SB_TQ, :] = _head_rms_gain(o, g_ref[...]).astype(o_ref.dtype)


def _sb_attention(proj, g, batch, seq):
    n = batch * seq
    rows = SB_TQ * SB_TILES_PER_STEP
    nq = seq // rows
    return pl.pallas_call(
        _sb_kernel,
        grid=(batch, N_PAIRS_SB, nq),
        in_specs=[
            pl.BlockSpec((None, rows, LANES), lambda b, p, i: (p, b * nq + i, 0)),
            pl.BlockSpec((None, seq, LANES), lambda b, p, i: (N_PAIRS_SB + p, b, 0)),
            pl.BlockSpec((None, seq, LANES), lambda b, p, i: (2 * N_PAIRS_SB + p, b, 0)),
            pl.BlockSpec((None, 1, LANES), lambda b, p, i: (p, 0, 0)),
        ],
        out_specs=pl.BlockSpec((None, rows, LANES), lambda b, p, i: (p, b * nq + i, 0)),
        out_shape=jax.ShapeDtypeStruct((N_PAIRS_SB, n, LANES), jnp.bfloat16),
        scratch_shapes=[pltpu.VMEM((SB_TILES_PER_STEP, 2 * SB_TQ, LANES), jnp.float32),
                        pltpu.VMEM((SB_TILES_PER_STEP, 2 * SB_TQ, LANES), jnp.float32)],
        compiler_params=pltpu.CompilerParams(
            dimension_semantics=("arbitrary", "arbitrary", "arbitrary"),
            vmem_limit_bytes=VMEM_LIMIT),
        name="sb_attention",
    )(proj, proj, proj, g)


def _dil_kernel(tbias_ref, q_ref, k_ref, v_ref, g_ref, o_ref,
                bias_ref, obr_ref, lse_ref):
    b = pl.program_id(0)
    p = pl.program_id(1)
    i = pl.program_id(2)
    n_br = len(DIL_PATTERNS)

    qi = lax.broadcasted_iota(jnp.int32, (BLOCK, 2 * BLOCK), 0)
    kj = lax.broadcasted_iota(jnp.int32, (BLOCK, 2 * BLOCK), 1)
    steps = qi + BLOCK - kj

    assert all(window // dilation == BLOCK for window, dilation in DIL_PATTERNS)
    valid = (steps >= 0) & (steps <= BLOCK)

    @pl.when((b == 0) & (p == 0) & (i == 0))
    def _():
        n_tiles = n_br * N_HEADS_DIL

        def per_tile(s, _):
            tile = tbias_ref[s]
            rows = pl.ds(pl.multiple_of(s * BLOCK, BLOCK), BLOCK)
            bias_ref[rows, :] = jnp.where(valid, tile, NEG_INF)
            rows = pl.ds(pl.multiple_of((n_tiles + s) * BLOCK, BLOCK), BLOCK)
            bias_ref[rows, :] = jnp.where(valid & (kj >= BLOCK), tile, NEG_INF)
            return 0
        lax.fori_loop(0, n_br * N_HEADS_DIL, per_tile, 0)

    lane = lax.broadcasted_iota(jnp.int32, (1, LANES), 1)
    first_head = lane < HEAD_DIM
    t0 = i * DIL_SUPER

    def strided(ref, start, dilation):
        if dilation == 1:
            return ref[pl.ds(start, BLOCK), :]
        return ref[pl.ds(start, BLOCK, stride=dilation), :]

    for gidx, (window, dilation) in enumerate(DIL_PATTERNS):
        units_per_residue = DIL_UNITS // dilation
        shift = units_per_residue.bit_length() - 1

        def unit(u, _, gidx=gidx, dilation=dilation, units_per_residue=units_per_residue, shift=shift):
            r = lax.shift_right_logical(u, shift)
            cb = u & (units_per_residue - 1)
            sq = r + cb * (dilation * BLOCK)
            cur = t0 + sq
            prev = cur - dilation * BLOCK
            prev_ok = prev >= 0
            prev_c = jnp.where(prev_ok, prev, cur)
            no_prev = jnp.where(prev_ok, 0, 1)

            qb = _bf16(strided(q_ref, sq, dilation))
            kk = _bf16(jnp.concatenate([strided(k_ref, prev_c, dilation),
                                        strided(k_ref, cur, dilation)], axis=0))
            vv = _bf16(jnp.concatenate([strided(v_ref, prev_c, dilation),
                                        strided(v_ref, cur, dilation)], axis=0))
            zero = jnp.zeros_like(qb)
            q2 = jnp.concatenate([jnp.where(first_head, qb, zero),
                                  jnp.where(first_head, zero, qb)], axis=0)
            z = _dot_nt(q2, kk)
            slot = no_prev * (n_br * N_HEADS_DIL) + gidx * N_HEADS_DIL + 2 * p
            lg = z + bias_ref[pl.ds(pl.multiple_of(slot * BLOCK, 2 * BLOCK), 2 * BLOCK), :]
            m = jnp.max(lg, axis=-1, keepdims=True)
            pe = jnp.exp(lg - m)
            res = _dot(_bf16(pe), jnp.concatenate([vv, jnp.ones_like(vv)], axis=1))
            den = res[:, LANES:]
            out = res[:, :LANES] / den
            lse = m + jnp.log(den)
            o_tile = jnp.where(first_head, out[:BLOCK], out[BLOCK:])
            l_tile = jnp.where(first_head, lse[:BLOCK], lse[BLOCK:])
            if dilation == 1:
                idx = pl.ds(sq, BLOCK)
            else:
                idx = pl.ds(sq, BLOCK, stride=dilation)
            obr_ref.at[gidx][idx, :] = o_tile
            lse_ref.at[gidx][idx, :] = l_tile
            return 0

        lax.fori_loop(0, DIL_UNITS, unit, 0, unroll=8)

    chunk = 256

    def combine(c, _):
        rows = pl.ds(pl.multiple_of(c * chunk, chunk), chunk)
        ls = [lse_ref[gidx, rows, :] for gidx in range(n_br)]
        m = functools.reduce(jnp.maximum, ls)
        es = [jnp.exp(l - m) for l in ls]
        num = sum(e * obr_ref[gidx, rows, :] for gidx, e in enumerate(es))
        o = num / sum(es)
        o_ref[rows, :] = _head_rms_gain(o, g_ref[...]).astype(o_ref.dtype)
        return 0

    lax.fori_loop(0, DIL_SUPER // chunk, combine, 0)


def _dil_attention(proj, g, tile_bias, batch, seq):
    n = batch * seq
    ns = seq // DIL_SUPER
    n_br = len(DIL_PATTERNS)
    return pl.pallas_call(
        _dil_kernel,
        grid=(batch, N_PAIRS_DIL, ns),
        in_specs=[
            pl.BlockSpec((n_br * N_HEADS_DIL, BLOCK, 2 * BLOCK), lambda b, p, i: (0, 0, 0)),
            pl.BlockSpec((None, DIL_SUPER, LANES), lambda b, p, i: (p, b * ns + i, 0)),
            pl.BlockSpec((None, seq, LANES), lambda b, p, i: (N_PAIRS_DIL + p, b, 0)),
            pl.BlockSpec((None, seq, LANES), lambda b, p, i: (2 * N_PAIRS_DIL + p, b, 0)),
            pl.BlockSpec((None, 1, LANES), lambda b, p, i: (p, 0, 0)),
        ],
        out_specs=pl.BlockSpec((None, DIL_SUPER, LANES), lambda b, p, i: (p, b * ns + i, 0)),
        out_shape=jax.ShapeDtypeStruct((N_PAIRS_DIL, n, LANES), jnp.bfloat16),
        scratch_shapes=[pltpu.VMEM((2 * n_br * N_HEADS_DIL * BLOCK, 2 * BLOCK), jnp.float32),
                        pltpu.VMEM((n_br, DIL_SUPER, LANES), jnp.float32),
                        pltpu.VMEM((n_br, DIL_SUPER, LANES), jnp.float32)],
        compiler_params=pltpu.CompilerParams(
            dimension_semantics=("arbitrary", "arbitrary", "arbitrary"),
            vmem_limit_bytes=VMEM_LIMIT),
        name="dilated_attention",
    )(tile_bias, proj, proj, proj, g)


def _router(x1, wrt_hi, wrt_lo, br):
    tm = x1.shape[0]
    xh, xl = _split_bf16(x1)
    lt = _dot_nt(wrt_hi, xh) + _dot_nt(wrt_hi, xl) + _dot_nt(wrt_lo, xh) + br
    rows = [lt[e:e + 1, :] for e in range(N_EXPERTS)]
    m = functools.reduce(jnp.maximum, rows)
    ex = [jnp.exp(r - m) for r in rows]
    tot = functools.reduce(lambda a, c: a + c, ex)
    probs = [e / tot for e in ex]

    keep, score = [], []
    for gi in range(N_GROUPS):
        members = range(gi * EXPERTS_PER_GROUP, (gi + 1) * EXPERTS_PER_GROUP)
        s = None
        for e in members:
            rank = None
            for o in members:
                if o == e:
                    continue
                beats = (probs[o] > probs[e]) | ((probs[o] == probs[e]) & (o < e))
                beats = beats.astype(jnp.int32)
                rank = beats if rank is None else rank + beats
            k = rank < 2
            keep.append(k)
            contrib = jnp.where(k, probs[e], 0.0)
            s = contrib if s is None else s + contrib
        score.append(s)

    sel, gates = [], []
    for gi in range(N_GROUPS):
        chosen = None
        for o in range(N_GROUPS):
            if o == gi:
                continue
            c = (score[gi] > score[o]) if o < gi else (score[gi] >= score[o])
            chosen = c if chosen is None else (chosen & c)
        for e in range(gi * EXPERTS_PER_GROUP, (gi + 1) * EXPERTS_PER_GROUP):
            s = keep[e] & chosen
            sel.append(s)
            gates.append(jnp.where(s, probs[e] / score[gi], 0.0))

    idx0 = functools.reduce(jnp.minimum, [jnp.where(sel[e], e, N_EXPERTS) for e in range(N_EXPERTS)])
    idx1 = functools.reduce(jnp.maximum, [jnp.where(sel[e], e, -1) for e in range(N_EXPERTS)])
    g0 = sum(jnp.where(idx0 == e, gates[e], 0.0) for e in range(N_EXPERTS))
    g1 = sum(jnp.where(idx1 == e, gates[e], 0.0) for e in range(N_EXPERTS))

    row_id = lax.broadcasted_iota(jnp.int32, (N_EXPERTS, tm), 0)
    selmat = jnp.zeros((N_EXPERTS, tm), jnp.float32)
    for e in range(N_EXPERTS):
        selmat = jnp.where((row_id == e) & sel[e], 1.0, selmat)
    return idx0, idx1, g0, g1, selmat


def _to_token_tiles(ref, x):
    rows = x.shape[0]
    for s in range(TOKEN_TILE_ROWS):
        ref[pl.ds(s, rows, stride=TOKEN_TILE_ROWS), :] = x[:, s * LANES:(s + 1) * LANES]


def _from_token_tiles(ref, rows):
    return jnp.concatenate([ref[pl.ds(s, rows, stride=TOKEN_TILE_ROWS), :]
                            for s in range(TOKEN_TILE_ROWS)], axis=1)


def _out_proj_kernel(x_ref, ysb_ref, ydl_ref, w_ref, lng_ref, lnb_ref,
                     wrh_ref, wrl_ref, br_ref,
                     x1_ref, x1t_ref, meta_ref, gcol_ref, cnt_ref, carry_ref, *, alpha):
    i = pl.program_id(0)
    tm = x_ref.shape[0]

    @pl.when(i == 0)
    def _():
        carry_ref[...] = jnp.zeros_like(carry_ref)

    y = jnp.concatenate([ysb_ref[j] for j in range(N_PAIRS_SB)]
                        + [ydl_ref[j] for j in range(N_PAIRS_DIL)], axis=1)
    h = _dot(y, w_ref[...])
    x1 = _layer_norm(alpha * x_ref[...] + h, lng_ref[...], lnb_ref[...])
    x1_ref[...] = x1
    _to_token_tiles(x1t_ref, x1)

    idx0, idx1, g0, g1, selmat = _router(x1, wrh_ref[...], wrl_ref[...], br_ref[...])

    jj = lax.broadcasted_iota(jnp.int32, (tm, tm + LANES), 0)
    ss = lax.broadcasted_iota(jnp.int32, (tm, tm + LANES), 1)
    before = jnp.where((jj < ss) | (ss >= tm), 1.0, 0.0).astype(jnp.bfloat16)
    pc = _dot(_bf16(selmat), before)
    carry = carry_ref[...]
    rankmat = pc[:, :tm] + jnp.concatenate([carry] * (tm // LANES), axis=1)
    carry = carry + pc[:, tm:]
    carry_ref[...] = carry
    cnt_ref[...] = carry

    rank0 = sum(jnp.where(idx0 == e, rankmat[e:e + 1, :], 0.0) for e in range(N_EXPERTS))
    rank1 = sum(jnp.where(idx1 == e, rankmat[e:e + 1, :], 0.0) for e in range(N_EXPERTS))

    row8 = lax.broadcasted_iota(jnp.int32, (8, tm), 0)
    meta = jnp.zeros((8, tm), jnp.int32)
    for r, v in enumerate((idx0, idx1, rank0.astype(jnp.int32), rank1.astype(jnp.int32))):
        meta = jnp.where(row8 == r, v, meta)
    meta_ref[...] = meta

    rowl = lax.broadcasted_iota(jnp.int32, (LANES, tm), 0)
    gt = jnp.where(rowl == 0, g0, jnp.where(rowl == 1, g1, 0.0))
    gcol_ref[...] = gt.T


def _out_proj(x2d, y_sb, y_dl, w_out, ln_g, ln_b, wrt_hi, wrt_lo, br, alpha):
    n, d = x2d.shape
    const = lambda i: (0, 0)
    return pl.pallas_call(
        functools.partial(_out_proj_kernel, alpha=alpha),
        grid=(n // ROW_TILE,),
        in_specs=[
            pl.BlockSpec((ROW_TILE, d), lambda i: (i, 0)),
            pl.BlockSpec((N_PAIRS_SB, ROW_TILE, LANES), lambda i: (0, i, 0)),
            pl.BlockSpec((N_PAIRS_DIL, ROW_TILE, LANES), lambda i: (0, i, 0)),
            pl.BlockSpec((d, d), const),
            pl.BlockSpec((1, d), const),
            pl.BlockSpec((1, d), const),
            pl.BlockSpec((N_EXPERTS, d), const),
            pl.BlockSpec((N_EXPERTS, d), const),
            pl.BlockSpec((N_EXPERTS, 1), const),
        ],
        out_specs=[pl.BlockSpec((ROW_TILE, d), lambda i: (i, 0)),
                   pl.BlockSpec((ROW_TILE * TOKEN_TILE_ROWS, LANES), lambda i: (i, 0)),
                   pl.BlockSpec((8, ROW_TILE), lambda i: (0, i)),
                   pl.BlockSpec((ROW_TILE, LANES), lambda i: (i, 0)),
                   pl.BlockSpec((N_EXPERTS, LANES), const)],
        out_shape=[jax.ShapeDtypeStruct((n, d), jnp.float32),
                   jax.ShapeDtypeStruct((n * TOKEN_TILE_ROWS, LANES), jnp.float32),
                   jax.ShapeDtypeStruct((8, n), jnp.int32),
                   jax.ShapeDtypeStruct((n, LANES), jnp.float32),
                   jax.ShapeDtypeStruct((N_EXPERTS, LANES), jnp.float32)],
        scratch_shapes=[pltpu.VMEM((N_EXPERTS, LANES), jnp.float32)],
        compiler_params=pltpu.CompilerParams(dimension_semantics=("arbitrary",),
                                             vmem_limit_bytes=VMEM_LIMIT),
        name="out_proj_ln_router",
    )(x2d, y_sb, y_dl, w_out, ln_g, ln_b, wrt_hi, wrt_lo, br)


def _segment_offsets(cnt_ref, off_ref):
    acc = jnp.int32(0)
    padded = []
    for e in range(N_EXPERTS):
        off_ref[e] = acc
        pc = lax.div(cnt_ref[e] + (MOE_TM - 1), MOE_TM) * MOE_TM
        padded.append(pc)
        acc = acc + pc
    return padded, acc


def _row_copy(src_ref, src_row, dst_ref, dst_row, sem):
    def tile(row):
        if isinstance(row, int):
            return pl.ds(row * TOKEN_TILE_ROWS, TOKEN_TILE_ROWS)
        return pl.ds(pl.multiple_of(row * TOKEN_TILE_ROWS, TOKEN_TILE_ROWS), TOKEN_TILE_ROWS)
    return pltpu.make_async_copy(src_ref.at[tile(src_row), :], dst_ref.at[tile(dst_row), :], sem)


def _pos_kernel(meta_ref, cnt_ref, pos_ref):
    cnt = cnt_ref[...]
    padded = jnp.floor((cnt + (MOE_TM - 1)) * (1.0 / MOE_TM)) * MOE_TM
    meta = meta_ref[...]
    rows = []
    for k in range(2):
        e = meta[k:k + 1, :]
        start = jnp.zeros(e.shape, jnp.float32)
        off = jnp.zeros((1, 1), jnp.float32)
        for ex in range(N_EXPERTS):
            start = jnp.where(e == ex, off, start)
            off = off + padded[ex:ex + 1, 0:1]
        rows.append(start.astype(jnp.int32) + meta[2 + k:3 + k, :])
    row8 = lax.broadcasted_iota(jnp.int32, meta.shape, 0)
    pos_ref[...] = jnp.where(row8 == 0, rows[0], jnp.where(row8 == 1, rows[1], 0))


def _positions(meta, cnt):
    n = meta.shape[1]
    tile = 2048
    return pl.pallas_call(
        _pos_kernel,
        grid=(n // tile,),
        in_specs=[pl.BlockSpec((8, tile), lambda i: (0, i)),
                  pl.BlockSpec((N_EXPERTS, LANES), lambda i: (0, 0))],
        out_specs=pl.BlockSpec((8, tile), lambda i: (0, i)),
        out_shape=jax.ShapeDtypeStruct((8, n), jnp.int32),
        compiler_params=pltpu.CompilerParams(dimension_semantics=("arbitrary",)),
        name="moe_pos",
    )(meta, cnt)


def _plan_kernel(cnt_ref, pos0_ref, pos1_ref, tok_ref, te_ref, off_ref, *, max_tiles):
    i = pl.program_id(0)
    tp = pos0_ref.shape[0]
    padded, total = _segment_offsets(cnt_ref, off_ref)

    @pl.when(i == 0)
    def _():
        def clear(k, c):
            tok_ref[k] = 0
            return c
        for e in range(N_EXPERTS):
            lax.fori_loop(off_ref[e] + cnt_ref[e], off_ref[e] + padded[e], clear, 0)
        lax.fori_loop(total, max_tiles * MOE_TM, clear, 0)

        n_tiles = lax.div(total, MOE_TM)
        for e in range(N_EXPERTS):
            first = lax.div(off_ref[e], MOE_TM)
            count = lax.div(padded[e], MOE_TM)

            def fill(k, c, first=first, e=e):
                te_ref[first + k] = e
                return c
            lax.fori_loop(0, count, fill, 0)
        last = te_ref[n_tiles - 1]

        def fill_tail(k, c):
            te_ref[k] = last
            return c
        lax.fori_loop(n_tiles, max_tiles, fill_tail, 0)
        te_ref[max_tiles] = n_tiles

    def place(j, c):
        token = i * tp + j
        tok_ref[pos0_ref[j]] = token
        tok_ref[pos1_ref[j]] = token
        return c
    lax.fori_loop(0, tp, place, 0, unroll=8)


def _plan(counts, pos0, pos1, max_tiles):
    n = pos0.shape[0]
    smem = pl.BlockSpec(memory_space=pltpu.SMEM)
    vec = pl.BlockSpec((PERM_TP,), lambda i: (i,), memory_space=pltpu.SMEM)
    return pl.pallas_call(
        functools.partial(_plan_kernel, max_tiles=max_tiles),
        grid=(n // PERM_TP,),
        in_specs=[smem, vec, vec],
        out_specs=[smem, smem],
        out_shape=[jax.ShapeDtypeStruct((max_tiles * MOE_TM,), jnp.int32),
                   jax.ShapeDtypeStruct((max_tiles + 1,), jnp.int32)],
        scratch_shapes=[pltpu.SMEM((N_EXPERTS,), jnp.int32)],
        compiler_params=pltpu.CompilerParams(dimension_semantics=("arbitrary",)),
        name="moe_plan",
    )(counts, pos0, pos1)


def _experts_kernel(tok_ref, te_ref, layer_ref, x_ref, wg_ref, wu_ref, wd_ref, ys_ref,
                    xbuf_ref, wgb_ref, wub_ref, wdb_ref, gsem, *, max_tiles):
    i = pl.program_id(0)
    n_tiles = te_ref[max_tiles]
    slot = i & 1
    tile_rows = MOE_TM * TOKEN_TILE_ROWS

    def gather(tile, s):
        base = tile * MOE_TM

        def body(j8, c):
            for u in range(DMA_UNROLL):
                j = j8 * DMA_UNROLL + u
                _row_copy(x_ref, tok_ref[base + j], xbuf_ref.at[s], j, gsem.at[s]).start(priority=u % 2)
            return c
        lax.fori_loop(0, MOE_TM // DMA_UNROLL, body, 0)

    @pl.when(i == 0)
    def _():
        gather(0, 0)

    @pl.when(i + 1 < n_tiles)
    def _():
        gather(i + 1, 1 - slot)

    @pl.when(i < n_tiles)
    def _():
        pltpu.make_async_copy(x_ref.at[pl.ds(0, tile_rows), :], xbuf_ref.at[slot],
                              gsem.at[slot]).wait()

        @pl.when((i == 0) | (te_ref[i] != te_ref[jnp.maximum(i - 1, 0)]))
        def _():
            wgb_ref[...] = _bf16(wg_ref[...])
            wub_ref[...] = _bf16(wu_ref[...])
            wdb_ref[...] = _bf16(wd_ref[...])

        xb = _bf16(_from_token_tiles(xbuf_ref.at[slot], MOE_TM))
        gt = _dot(xb, wgb_ref[...])
        up = _dot(xb, wub_ref[...])
        h = gt / (1.0 + jnp.exp(-gt)) * up
        _to_token_tiles(ys_ref, _dot(_bf16(h), wdb_ref[...]))

    @pl.when(i >= n_tiles)
    def _():
        ys_ref[...] = jnp.zeros_like(ys_ref)


def _experts(tok, te, x1t, w_gate, w_up, w_down, layer, max_tiles):
    d, f = w_gate.shape[-2:]
    rows = MOE_TM * TOKEN_TILE_ROWS
    expert = lambda i, tok, te, lay: (lay[0], te[i], 0, 0)
    return pl.pallas_call(
        functools.partial(_experts_kernel, max_tiles=max_tiles),
        grid_spec=pltpu.PrefetchScalarGridSpec(
            num_scalar_prefetch=3,
            grid=(max_tiles,),
            in_specs=[pl.BlockSpec(memory_space=pl.ANY),
                      pl.BlockSpec((None, None, d, f), expert),
                      pl.BlockSpec((None, None, d, f), expert),
                      pl.BlockSpec((None, None, f, d), expert)],
            out_specs=pl.BlockSpec((rows, LANES), lambda i, tok, te, lay: (i, 0)),
            scratch_shapes=[pltpu.VMEM((2, rows, LANES), jnp.float32),
                            pltpu.VMEM((d, f), jnp.bfloat16),
                            pltpu.VMEM((d, f), jnp.bfloat16),
                            pltpu.VMEM((f, d), jnp.bfloat16),
                            pltpu.SemaphoreType.DMA((2,))]),
        out_shape=jax.ShapeDtypeStruct((max_tiles * rows, LANES), jnp.float32),
        compiler_params=pltpu.CompilerParams(dimension_semantics=("arbitrary",),
                                             vmem_limit_bytes=VMEM_LIMIT),
        name="moe_experts",
    )(tok, te, jnp.full((1,), layer, jnp.int32), x1t, w_gate, w_up, w_down)


def _combine_kernel(pos0_ref, pos1_ref, x_ref, gcol_ref, lng_ref, lnb_ref, ys_ref, o_ref,
                    ybuf_ref, sem, *, alpha):
    i = pl.program_id(0)
    n_steps = pl.num_programs(0)
    tm = x_ref.shape[0]
    slot = i & 1
    tile_rows = tm * TOKEN_TILE_ROWS

    def gather(step, s):
        base = step * tm

        def body(j, c):
            _row_copy(ys_ref, pos0_ref[base + j], ybuf_ref.at[s, 0], j, sem.at[s]).start(priority=0)
            _row_copy(ys_ref, pos1_ref[base + j], ybuf_ref.at[s, 1], j, sem.at[s]).start(priority=1)
            return c
        lax.fori_loop(0, tm, body, 0, unroll=DMA_UNROLL)

    @pl.when(i == 0)
    def _():
        gather(0, 0)

    @pl.when(i + 1 < n_steps)
    def _():
        gather(i + 1, 1 - slot)

    for k in range(2):
        pltpu.make_async_copy(ys_ref.at[pl.ds(0, tile_rows), :], ybuf_ref.at[slot, k],
                              sem.at[slot]).wait()

    lane = lax.broadcasted_iota(jnp.int32, (1, LANES), 1)
    gc = gcol_ref[...]
    g0 = jnp.sum(jnp.where(lane == 0, gc, 0.0), axis=-1, keepdims=True)
    g1 = jnp.sum(jnp.where(lane == 1, gc, 0.0), axis=-1, keepdims=True)
    y = (g0 * _from_token_tiles(ybuf_ref.at[slot, 0], tm)
         + g1 * _from_token_tiles(ybuf_ref.at[slot, 1], tm))
    o_ref[...] = _layer_norm(alpha * x_ref[...] + y, lng_ref[...], lnb_ref[...])


def _combine(pos0, pos1, x1, ys, gcol, ln_g, ln_b, alpha):
    n, d = x1.shape
    const = lambda i, p0, p1: (0, 0)
    tile = lambda i, p0, p1: (i, 0)
    return pl.pallas_call(
        functools.partial(_combine_kernel, alpha=alpha),
        grid_spec=pltpu.PrefetchScalarGridSpec(
            num_scalar_prefetch=2,
            grid=(n // ROW_TILE,),
            in_specs=[pl.BlockSpec((ROW_TILE, d), tile),
                      pl.BlockSpec((ROW_TILE, LANES), tile),
                      pl.BlockSpec((1, d), const), pl.BlockSpec((1, d), const),
                      pl.BlockSpec(memory_space=pl.ANY)],
            out_specs=pl.BlockSpec((ROW_TILE, d), tile),
            scratch_shapes=[pltpu.VMEM((2, 2, ROW_TILE * TOKEN_TILE_ROWS, LANES), jnp.float32),
                            pltpu.SemaphoreType.DMA((2,))]),
        out_shape=jax.ShapeDtypeStruct((n, d), jnp.float32),
        compiler_params=pltpu.CompilerParams(dimension_semantics=("arbitrary",),
                                             vmem_limit_bytes=VMEM_LIMIT),
        name="moe_combine_ln",
    )(pos0, pos1, x1, gcol, ln_g, ln_b, ys)


def _moe(x1, x1t, meta, gcol, cnt, w_gate, w_up, w_down, layer, ln_g, ln_b, alpha):
    n = x1.shape[0]
    max_tiles = (2 * n + N_EXPERTS * (MOE_TM - 1)) // MOE_TM
    counts = cnt[:, 0].astype(jnp.int32)
    pos = _positions(meta, cnt)
    pos0, pos1 = pos[0], pos[1]
    tok, te = _plan(counts, pos0, pos1, max_tiles)
    ys = _experts(tok, te, x1t, w_gate, w_up, w_down, layer, max_tiles)
    return _combine(pos0, pos1, x1, ys, gcol, ln_g, ln_b, alpha)


def _t5_bucket(dist):
    max_exact = N_BUCKETS // 2
    d = jnp.maximum(dist, 0)
    large = max_exact + (jnp.log(jnp.maximum(d, 1).astype(jnp.float32) / max_exact)
                         / math.log(MAX_DISTANCE / max_exact) * (N_BUCKETS - max_exact)).astype(jnp.int32)
    large = jnp.minimum(large, N_BUCKETS - 1)
    return jnp.where(d < max_exact, d, large)


def _tile_bias_kernel(w_ref, o_ref):
    w = w_ref[0]
    for q in range(BLOCK):
        o_ref[q:q + 1, :] = w[:, BLOCK - 1 - q:3 * BLOCK - 1 - q]


def _tile_bias(rel_bias):
    dist = jnp.arange(BLOCK + 1)
    rows = jnp.stack([rel_bias[_t5_bucket(dist * dilation)] for _, dilation in DIL_PATTERNS])
    n_br, _, heads = rows.shape
    table = rows.astype(jnp.float32).transpose(0, 2, 1).reshape(n_br * heads, BLOCK + 1)
    w = jnp.pad(table[:, ::-1], ((0, 0), (BLOCK - 1, BLOCK)))[:, None, :]
    return pl.pallas_call(
        _tile_bias_kernel,
        grid=(n_br * heads,),
        in_specs=[pl.BlockSpec((1, 1, 3 * BLOCK), lambda s: (s, 0, 0))],
        out_specs=pl.BlockSpec((None, BLOCK, 2 * BLOCK), lambda s: (s, 0, 0)),
        out_shape=jax.ShapeDtypeStruct((n_br * heads, BLOCK, 2 * BLOCK), jnp.float32),
        compiler_params=pltpu.CompilerParams(dimension_semantics=("arbitrary",)),
        name="tile_bias",
    )(w)


def kernel(x, w_in, g_sb, g_dil, w_out, ln1_g, ln1_b, ln2_g, ln2_b, rel_bias,
           w_router, b_router, w_gate, w_up, w_down):
    batch, seq, d = x.shape
    depth = w_in.shape[0]
    alpha = (2.0 * depth) ** 0.25
    n = batch * seq
    x2d = x.reshape(n, d)

    tile_bias = _tile_bias(rel_bias)
    wrt =w_router.T.astype(jnp.float32)
    wrt_hi = _bf16(wrt)
    wrt_lo = _bf16(wrt - wrt_hi.astype(jnp.float32))
    br = b_router.astype(jnp.float32).reshape(N_EXPERTS, 1)

    for l in range(depth):
        w_in_l = _bf16(w_in[l])
        proj_sb, proj_dl = _in_proj(x2d, w_in_l)
        y_sb = _sb_attention(proj_sb, g_sb[l].reshape(N_PAIRS_SB, 1, LANES), batch, seq)
        y_dl = _dil_attention(proj_dl, g_dil[l].reshape(N_PAIRS_DIL, 1, LANES),
                              tile_bias, batch, seq)
        x1, x1t, meta, gcol, cnt = _out_proj(x2d, y_sb, y_dl, _bf16(w_out[l]),
                                             ln1_g[l].reshape(1, d), ln1_b[l].reshape(1, d),
                                             wrt_hi, wrt_lo, br, alpha)
        x2d = _moe(x1, x1t, meta, gcol, cnt, w_gate, w_up, w_down, l,
                   ln2_g[l].reshape(1, d), ln2_b[l].reshape(1, d), alpha)
    return x2d.reshape(batch, seq, d)
```

```python
import functools
import math

import jax
import jax.numpy as jnp
from jax import lax
from jax.experimental import pallas as pl
from jax.experimental.pallas import tpu as pltpu

D_MODEL = 1024
HEAD_DIM = 64
N_HEADS_SB = 8
N_HEADS_DIL = 8
D_SB = N_HEADS_SB * HEAD_DIM
D_DIL = N_HEADS_DIL * HEAD_DIM
DIL_PATTERNS = ((128, 1), (512, 4), (2048, 16))
BLOCK = 128
N_BUCKETS = 32
MAX_DISTANCE = 2048
N_EXPERTS = 16
N_GROUPS = 4
EXPERTS_PER_GROUP = N_EXPERTS // N_GROUPS
D_FF_EXPERT = 1024
LN_EPS = 1e-5
NEG_INF = -1e30

LANES = 128
N_PAIRS_SB = D_SB // LANES
N_PAIRS_DIL = D_DIL // LANES
VMEM_LIMIT = 56 * 1024 * 1024

ROW_TILE = 1024
SB_TQ = 256
SB_TK = 256
SB_TILES_PER_STEP = 4
SB_SKIP_BOUND = -110.0
DIL_SUPER = 2048
DIL_UNITS = DIL_SUPER // BLOCK
TOKEN_TILE_ROWS = D_MODEL // LANES
MOE_TM = 256
PERM_TP = 512
DMA_UNROLL = 8

_NT = (((1,), (1,)), ((), ()))


def _bf16(x):
    return x.astype(jnp.bfloat16)


def _split_bf16(x):
    hi = _bf16(x)
    lo = _bf16(x - hi.astype(jnp.float32))
    return hi, lo


def _dot(a, b):
    return jnp.dot(a, b, preferred_element_type=jnp.float32)


def _dot_nt(a, b):
    return lax.dot_general(a, b, _NT, preferred_element_type=jnp.float32)


def _head_rms_gain(o, g):
    r = lax.broadcasted_iota(jnp.int32, (LANES, LANES), 0) // HEAD_DIM
    c = lax.broadcasted_iota(jnp.int32, (LANES, LANES), 1) // HEAD_DIM
    same_head = jnp.where(r == c, 1.0 / HEAD_DIM, 0.0).astype(jnp.bfloat16)
    hi, lo = _split_bf16(o * o)
    ms = _dot(hi, same_head) + _dot(lo, same_head)
    return o * lax.rsqrt(ms + 1e-6) * g


def _layer_norm(x, g, b):
    mu = jnp.mean(x, axis=-1, keepdims=True)
    xc = x - mu
    var = jnp.mean(xc * xc, axis=-1, keepdims=True)
    return xc * lax.rsqrt(var + LN_EPS) * g + b


def _in_proj_kernel(x_ref, w_ref, sb_ref, dl_ref):
    x = _bf16(x_ref[...])
    scale = HEAD_DIM ** -0.5
    col = 0
    for o_ref, n_q in ((sb_ref, N_PAIRS_SB), (dl_ref, N_PAIRS_DIL)):
        for j in range(0, o_ref.shape[0], 2):
            res = _dot(x, w_ref[:, col:col + 2 * LANES])
            col += 2 * LANES
            for jj in range(2):
                blk = res[:, jj * LANES:(jj + 1) * LANES]
                if j + jj < n_q:
                    blk = blk * scale
                o_ref[j + jj] = blk.astype(o_ref.dtype)


def _in_proj(x2d, w):
    n, d = x2d.shape
    slabs = lambda k: pl.BlockSpec((k, ROW_TILE, LANES), lambda i: (0, i, 0))
    return pl.pallas_call(
        _in_proj_kernel,
        grid=(n // ROW_TILE,),
        in_specs=[pl.BlockSpec((ROW_TILE, d), lambda i: (i, 0)),
                  pl.BlockSpec((d, w.shape[1]), lambda i: (0, 0))],
        out_specs=[slabs(3 * N_PAIRS_SB), slabs(3 * N_PAIRS_DIL)],
        out_shape=[jax.ShapeDtypeStruct((3 * N_PAIRS_SB, n, LANES), jnp.bfloat16),
                   jax.ShapeDtypeStruct((3 * N_PAIRS_DIL, n, LANES), jnp.float32)],
        compiler_params=pltpu.CompilerParams(dimension_semantics=("arbitrary",),
                                             vmem_limit_bytes=VMEM_LIMIT),
        name="in_proj",
    )(x2d, w)


def _sb_kernel(q_ref, k_ref, v_ref, g_ref, o_ref, carry_ref, acc_ref):
    step = pl.program_id(2)
    lane = lax.broadcasted_iota(jnp.int32, (1, LANES), 1)
    first_head = lane < HEAD_DIM

    jj = lax.broadcasted_iota(jnp.int32, (SB_TK, SB_TK + LANES), 0)
    ss = lax.broadcasted_iota(jnp.int32, (SB_TK, SB_TK + LANES), 1)
    suffix = jnp.where((jj > ss) | (ss >= SB_TK), 1.0, 0.0).astype(jnp.bfloat16)

    tq_i = lax.broadcasted_iota(jnp.int32, (2 * SB_TQ, SB_TK), 0) % SB_TQ
    ts_i = lax.broadcasted_iota(jnp.int32, (2 * SB_TQ, SB_TK), 1)
    causal = ts_i < tq_i

    def block(q2, kb, diag, carry, acc, weight=None):
        off = pl.multiple_of(kb * SB_TK, SB_TK)
        kblk = k_ref[pl.ds(off, SB_TK), :]
        vblk = v_ref[pl.ds(off, SB_TK), :]
        z = _dot_nt(q2, kblk)
        sp = jnp.log(1.0 + jnp.exp(-jnp.abs(z)))
        log_beta = jnp.minimum(z, 0.0) - sp
        log_1m = log_beta - z
        if diag:
            log_1m = jnp.where(causal, log_1m, 0.0)
        sfx = _dot(_bf16(log_1m), suffix)
        between = sfx[:, :SB_TK] + jnp.concatenate([carry] * (SB_TK // LANES), axis=1)
        a = jnp.exp(log_beta + between)
        if diag:
            a = jnp.where(causal, a, 0.0)
        out = _dot(_bf16(a), vblk)
        tot = sfx[:, SB_TK:]
        if weight is not None:
            out = out * weight
            tot = tot * weight
        return carry + tot, acc + out

    zeros = jnp.zeros((2 * SB_TQ, LANES), jnp.float32)
    tiles = []
    for t in range(SB_TILES_PER_STEP):
        i = step * SB_TILES_PER_STEP + t
        q = q_ref[t * SB_TQ:(t + 1) * SB_TQ, :]
        zero = jnp.zeros_like(q)
        q2 = jnp.concatenate([jnp.where(first_head, q, zero), jnp.where(first_head, zero, q)], axis=0)
        carry, acc = block(q2, i, True, zeros, zeros)
        has_prev = jnp.where(i >= 1, 1.0, 0.0)
        carry, acc = block(q2, jnp.maximum(i - 1, 0), False, carry, acc, weight=has_prev)
        carry_ref[t] = carry
        acc_ref[t] = acc
        tiles.append((i, q2, jnp.max(carry)))

    for t, (i, q2, mx0) in enumerate(tiles):
        def cond(state):
            kb, mx = state
            return jnp.logical_and(kb >= 0, mx > SB_SKIP_BOUND)

        def body(state, t=t, q2=q2):
            kb, _ = state
            c, a = block(q2, kb, False, carry_ref[t], acc_ref[t])
            carry_ref[t] = c
            acc_ref[t] = a
            return kb - 1, jnp.max(c)

        lax.while_loop(cond, body, (i - 2, mx0))

        o = jnp.where(first_head, acc_ref[t, :SB_TQ, :], acc_ref[t, SB_TQ:, :])
        o_ref[t * SB_TQ:(t + 1) * SB_TQ, :] = _head_rms_gain(o, g_ref[...]).astype(o_ref.dtype)


def _sb_attention(proj, g, batch, seq):
    n = batch * seq
    rows = SB_TQ * SB_TILES_PER_STEP
    nq = seq // rows
    return pl.pallas_call(
        _sb_kernel,
        grid=(batch, N_PAIRS_SB, nq),
        in_specs=[
            pl.BlockSpec((None, rows, LANES), lambda b, p, i: (p, b * nq + i, 0)),
            pl.BlockSpec((None, seq, LANES), lambda b, p, i: (N_PAIRS_SB + p, b, 0)),
            pl.BlockSpec((None, seq, LANES), lambda b, p, i: (2 * N_PAIRS_SB + p, b, 0)),
            pl.BlockSpec((None, 1, LANES), lambda b, p, i: (p, 0, 0)),
        ],
        out_specs=pl.BlockSpec((None, rows, LANES), lambda b, p, i: (p, b * nq + i, 0)),
        out_shape=jax.ShapeDtypeStruct((N_PAIRS_SB, n, LANES), jnp.bfloat16),
        scratch_shapes=[pltpu.VMEM((SB_TILES_PER_STEP, 2 * SB_TQ, LANES), jnp.float32),
                        pltpu.VMEM((SB_TILES_PER_STEP, 2 * SB_TQ, LANES), jnp.float32)],
        compiler_params=pltpu.CompilerParams(
            dimension_semantics=("arbitrary", "arbitrary", "arbitrary"),
            vmem_limit_bytes=VMEM_LIMIT),
        name="sb_attention",
    )(proj, proj, proj, g)


def _dil_kernel(tbias_ref, q_ref, k_ref, v_ref, g_ref, o_ref,
                bias_ref, obr_ref, lse_ref):
    b = pl.program_id(0)
    p = pl.program_id(1)
    i = pl.program_id(2)
    n_br = len(DIL_PATTERNS)

    qi = lax.broadcasted_iota(jnp.int32, (BLOCK, 2 * BLOCK), 0)
    kj = lax.broadcasted_iota(jnp.int32, (BLOCK, 2 * BLOCK), 1)
    steps = qi + BLOCK - kj

    assert all(window // dilation == BLOCK for window, dilation in DIL_PATTERNS)
    valid = (steps >= 0) & (steps <= BLOCK)

    @pl.when((b == 0) & (p == 0) & (i == 0))
    def _():
        n_tiles = n_br * N_HEADS_DIL

        def per_tile(s, _):
            tile = tbias_ref[s]
            rows = pl.ds(pl.multiple_of(s * BLOCK, BLOCK), BLOCK)
            bias_ref[rows, :] = jnp.where(valid, tile, NEG_INF)
            rows = pl.ds(pl.multiple_of((n_tiles + s) * BLOCK, BLOCK), BLOCK)
            bias_ref[rows, :] = jnp.where(valid & (kj >= BLOCK), tile, NEG_INF)
            return 0
        lax.fori_loop(0, n_br * N_HEADS_DIL, per_tile, 0)

    lane = lax.broadcasted_iota(jnp.int32, (1, LANES), 1)
    first_head = lane < HEAD_DIM
    t0 = i * DIL_SUPER

    def strided(ref, start, dilation):
        if dilation == 1:
            return ref[pl.ds(start, BLOCK), :]
        return ref[pl.ds(start, BLOCK, stride=dilation), :]

    for gidx, (window, dilation) in enumerate(DIL_PATTERNS):
        units_per_residue = DIL_UNITS // dilation
        shift = units_per_residue.bit_length() - 1

        def unit(u, _, gidx=gidx, dilation=dilation, units_per_residue=units_per_residue, shift=shift):
            r = lax.shift_right_logical(u, shift)
            cb = u & (units_per_residue - 1)
            sq = r + cb * (dilation * BLOCK)
            cur = t0 + sq
            prev = cur - dilation * BLOCK
            prev_ok = prev >= 0
            prev_c = jnp.where(prev_ok, prev, cur)
            no_prev = jnp.where(prev_ok, 0, 1)

            qb = _bf16(strided(q_ref, sq, dilation))
            kk = _bf16(jnp.concatenate([strided(k_ref, prev_c, dilation),
                                        strided(k_ref, cur, dilation)], axis=0))
            vv = _bf16(jnp.concatenate([strided(v_ref, prev_c, dilation),
                                        strided(v_ref, cur, dilation)], axis=0))
            zero = jnp.zeros_like(qb)
            q2 = jnp.concatenate([jnp.where(first_head, qb, zero),
                                  jnp.where(first_head, zero, qb)], axis=0)
            z = _dot_nt(q2, kk)
            slot = no_prev * (n_br * N_HEADS_DIL) + gidx * N_HEADS_DIL + 2 * p
            lg = z + bias_ref[pl.ds(pl.multiple_of(slot * BLOCK, 2 * BLOCK), 2 * BLOCK), :]
            m = jnp.max(lg, axis=-1, keepdims=True)
            pe = jnp.exp(lg - m)
            res = _dot(_bf16(pe), jnp.concatenate([vv, jnp.ones_like(vv)], axis=1))
            den = res[:, LANES:]
            out = res[:, :LANES] / den
            lse = m + jnp.log(den)
            o_tile = jnp.where(first_head, out[:BLOCK], out[BLOCK:])
            l_tile = jnp.where(first_head, lse[:BLOCK], lse[BLOCK:])
            if dilation == 1:
                idx = pl.ds(sq, BLOCK)
            else:
                idx = pl.ds(sq, BLOCK, stride=dilation)
            obr_ref.at[gidx][idx, :] = o_tile
            lse_ref.at[gidx][idx, :] = l_tile
            return 0

        lax.fori_loop(0, DIL_UNITS, unit, 0, unroll=8)

    chunk = 256

    def combine(c, _):
        rows = pl.ds(pl.multiple_of(c * chunk, chunk), chunk)
        ls = [lse_ref[gidx, rows, :] for gidx in range(n_br)]
        m = functools.reduce(jnp.maximum, ls)
        es = [jnp.exp(l - m) for l in ls]
        num = sum(e * obr_ref[gidx, rows, :] for gidx, e in enumerate(es))
        o = num / sum(es)
        o_ref[rows, :] = _head_rms_gain(o, g_ref[...]).astype(o_ref.dtype)
        return 0

    lax.fori_loop(0, DIL_SUPER // chunk, combine, 0)


def _dil_attention(proj, g, tile_bias, batch, seq):
    n = batch * seq
    ns = seq // DIL_SUPER
    n_br = len(DIL_PATTERNS)
    return pl.pallas_call(
        _dil_kernel,
        grid=(batch, N_PAIRS_DIL, ns),
        in_specs=[
            pl.BlockSpec((n_br * N_HEADS_DIL, BLOCK, 2 * BLOCK), lambda b, p, i: (0, 0, 0)),
            pl.BlockSpec((None, DIL_SUPER, LANES), lambda b, p, i: (p, b * ns + i, 0)),
            pl.BlockSpec((None, seq, LANES), lambda b, p, i: (N_PAIRS_DIL + p, b, 0)),
            pl.BlockSpec((None, seq, LANES), lambda b, p, i: (2 * N_PAIRS_DIL + p, b, 0)),
            pl.BlockSpec((None, 1, LANES), lambda b, p, i: (p, 0, 0)),
        ],
        out_specs=pl.BlockSpec((None, DIL_SUPER, LANES), lambda b, p, i: (p, b * ns + i, 0)),
        out_shape=jax.ShapeDtypeStruct((N_PAIRS_DIL, n, LANES), jnp.bfloat16),
        scratch_shapes=[pltpu.VMEM((2 * n_br * N_HEADS_DIL * BLOCK, 2 * BLOCK), jnp.float32),
                        pltpu.VMEM((n_br, DIL_SUPER, LANES), jnp.float32),
                        pltpu.VMEM((n_br, DIL_SUPER, LANES), jnp.float32)],
        compiler_params=pltpu.CompilerParams(
            dimension_semantics=("arbitrary", "arbitrary", "arbitrary"),
            vmem_limit_bytes=VMEM_LIMIT),
        name="dilated_attention",
    )(tile_bias, proj, proj, proj, g)


def _router(x1, wrt_hi, wrt_lo, br):
    tm = x1.shape[0]
    xh, xl = _split_bf16(x1)
    lt = _dot_nt(wrt_hi, xh) + _dot_nt(wrt_hi, xl) + _dot_nt(wrt_lo, xh) + br
    rows = [lt[e:e + 1, :] for e in range(N_EXPERTS)]
    m = functools.reduce(jnp.maximum, rows)
    ex = [jnp.exp(r - m) for r in rows]
    tot = functools.reduce(lambda a, c: a + c, ex)
    probs = [e / tot for e in ex]

    keep, score = [], []
    for gi in range(N_GROUPS):
        members = range(gi * EXPERTS_PER_GROUP, (gi + 1) * EXPERTS_PER_GROUP)
        s = None
        for e in members:
            rank = None
            for o in members:
                if o == e:
                    continue
                beats = (probs[o] > probs[e]) | ((probs[o] == probs[e]) & (o < e))
                beats = beats.astype(jnp.int32)
                rank = beats if rank is None else rank + beats
            k = rank < 2
            keep.append(k)
            contrib = jnp.where(k, probs[e], 0.0)
            s = contrib if s is None else s + contrib
        score.append(s)

    sel, gates = [], []
    for gi in range(N_GROUPS):
        chosen = None
        for o in range(N_GROUPS):
            if o == gi:
                continue
            c = (score[gi] > score[o]) if o < gi else (score[gi] >= score[o])
            chosen = c if chosen is None else (chosen & c)
        for e in range(gi * EXPERTS_PER_GROUP, (gi + 1) * EXPERTS_PER_GROUP):
            s = keep[e] & chosen
            sel.append(s)
            gates.append(jnp.where(s, probs[e] / score[gi], 0.0))

    idx0 = functools.reduce(jnp.minimum, [jnp.where(sel[e], e, N_EXPERTS) for e in range(N_EXPERTS)])
    idx1 = functools.reduce(jnp.maximum, [jnp.where(sel[e], e, -1) for e in range(N_EXPERTS)])
    g0 = sum(jnp.where(idx0 == e, gates[e], 0.0) for e in range(N_EXPERTS))
    g1 = sum(jnp.where(idx1 == e, gates[e], 0.0) for e in range(N_EXPERTS))

    row_id = lax.broadcasted_iota(jnp.int32, (N_EXPERTS, tm), 0)
    selmat = jnp.zeros((N_EXPERTS, tm), jnp.float32)
    for e in range(N_EXPERTS):
        selmat = jnp.where((row_id == e) & sel[e], 1.0, selmat)
    return idx0, idx1, g0, g1, selmat


def _to_token_tiles(ref, x):
    rows = x.shape[0]
    for s in range(TOKEN_TILE_ROWS):
        ref[pl.ds(s, rows, stride=TOKEN_TILE_ROWS), :] = x[:, s * LANES:(s + 1) * LANES]


def _from_token_tiles(ref, rows):
    return jnp.concatenate([ref[pl.ds(s, rows, stride=TOKEN_TILE_ROWS), :]
                            for s in range(TOKEN_TILE_ROWS)], axis=1)


def _out_proj_kernel(x_ref, ysb_ref, ydl_ref, w_ref, lng_ref, lnb_ref,
                     wrh_ref, wrl_ref, br_ref,
                     x1_ref, x1t_ref, meta_ref, gcol_ref, cnt_ref, carry_ref, *, alpha):
    i = pl.program_id(0)
    tm = x_ref.shape[0]

    @pl.when(i == 0)
    def _():
        carry_ref[...] = jnp.zeros_like(carry_ref)

    y = jnp.concatenate([ysb_ref[j] for j in range(N_PAIRS_SB)]
                        + [ydl_ref[j] for j in range(N_PAIRS_DIL)], axis=1)
    h = _dot(y, w_ref[...])
    x1 = _layer_norm(alpha * x_ref[...] + h, lng_ref[...], lnb_ref[...])
    x1_ref[...] = x1
    _to_token_tiles(x1t_ref, x1)

    idx0, idx1, g0, g1, selmat = _router(x1, wrh_ref[...], wrl_ref[...], br_ref[...])

    jj = lax.broadcasted_iota(jnp.int32, (tm, tm + LANES), 0)
    ss = lax.broadcasted_iota(jnp.int32, (tm, tm + LANES), 1)
    before = jnp.where((jj < ss) | (ss >= tm), 1.0, 0.0).astype(jnp.bfloat16)
    pc = _dot(_bf16(selmat), before)
    carry = carry_ref[...]
    rankmat = pc[:, :tm] + jnp.concatenate([carry] * (tm // LANES), axis=1)
    carry = carry + pc[:, tm:]
    carry_ref[...] = carry
    cnt_ref[...] = carry

    rank0 = sum(jnp.where(idx0 == e, rankmat[e:e + 1, :], 0.0) for e in range(N_EXPERTS))
    rank1 = sum(jnp.where(idx1 == e, rankmat[e:e + 1, :], 0.0) for e in range(N_EXPERTS))

    row8 = lax.broadcasted_iota(jnp.int32, (8, tm), 0)
    meta = jnp.zeros((8, tm), jnp.int32)
    for r, v in enumerate((idx0, idx1, rank0.astype(jnp.int32), rank1.astype(jnp.int32))):
        meta = jnp.where(row8 == r, v, meta)
    meta_ref[...] = meta

    rowl = lax.broadcasted_iota(jnp.int32, (LANES, tm), 0)
    gt = jnp.where(rowl == 0, g0, jnp.where(rowl == 1, g1, 0.0))
    gcol_ref[...] = gt.T


def _out_proj(x2d, y_sb, y_dl, w_out, ln_g, ln_b, wrt_hi, wrt_lo, br, alpha):
    n, d = x2d.shape
    const = lambda i: (0, 0)
    return pl.pallas_call(
        functools.partial(_out_proj_kernel, alpha=alpha),
        grid=(n // ROW_TILE,),
        in_specs=[
            pl.BlockSpec((ROW_TILE, d), lambda i: (i, 0)),
            pl.BlockSpec((N_PAIRS_SB, ROW_TILE, LANES), lambda i: (0, i, 0)),
            pl.BlockSpec((N_PAIRS_DIL, ROW_TILE, LANES), lambda i: (0, i, 0)),
            pl.BlockSpec((d, d), const),
            pl.BlockSpec((1, d), const),
            pl.BlockSpec((1, d), const),
            pl.BlockSpec((N_EXPERTS, d), const),
            pl.BlockSpec((N_EXPERTS, d), const),
            pl.BlockSpec((N_EXPERTS, 1), const),
        ],
        out_specs=[pl.BlockSpec((ROW_TILE, d), lambda i: (i, 0)),
                   pl.BlockSpec((ROW_TILE * TOKEN_TILE_ROWS, LANES), lambda i: (i, 0)),
                   pl.BlockSpec((8, ROW_TILE), lambda i: (0, i)),
                   pl.BlockSpec((ROW_TILE, LANES), lambda i: (i, 0)),
                   pl.BlockSpec((N_EXPERTS, LANES), const)],
        out_shape=[jax.ShapeDtypeStruct((n, d), jnp.float32),
                   jax.ShapeDtypeStruct((n * TOKEN_TILE_ROWS, LANES), jnp.float32),
                   jax.ShapeDtypeStruct((8, n), jnp.int32),
                   jax.ShapeDtypeStruct((n, LANES), jnp.float32),
                   jax.ShapeDtypeStruct((N_EXPERTS, LANES), jnp.float32)],
        scratch_shapes=[pltpu.VMEM((N_EXPERTS, LANES), jnp.float32)],
        compiler_params=pltpu.CompilerParams(dimension_semantics=("arbitrary",),
                                             vmem_limit_bytes=VMEM_LIMIT),
        name="out_proj_ln_router",
    )(x2d, y_sb, y_dl, w_out, ln_g, ln_b, wrt_hi, wrt_lo, br)


def _segment_offsets(cnt_ref, off_ref):
    acc = jnp.int32(0)
    padded = []
    for e in range(N_EXPERTS):
        off_ref[e] = acc
        pc = lax.div(cnt_ref[e] + (MOE_TM - 1), MOE_TM) * MOE_TM
        padded.append(pc)
        acc = acc + pc
    return padded, acc


def _row_copy(src_ref, src_row, dst_ref, dst_row, sem):
    def tile(row):
        if isinstance(row, int):
            return pl.ds(row * TOKEN_TILE_ROWS, TOKEN_TILE_ROWS)
        return pl.ds(pl.multiple_of(row * TOKEN_TILE_ROWS, TOKEN_TILE_ROWS), TOKEN_TILE_ROWS)
    return pltpu.make_async_copy(src_ref.at[tile(src_row), :], dst_ref.at[tile(dst_row), :], sem)


def _pos_kernel(meta_ref, cnt_ref, pos_ref):
    cnt = cnt_ref[...]
    padded = jnp.floor((cnt + (MOE_TM - 1)) * (1.0 / MOE_TM)) * MOE_TM
    meta = meta_ref[...]
    rows = []
    for k in range(2):
        e = meta[k:k + 1, :]
        start = jnp.zeros(e.shape, jnp.float32)
        off = jnp.zeros((1, 1), jnp.float32)
        for ex in range(N_EXPERTS):
            start = jnp.where(e == ex, off, start)
            off = off + padded[ex:ex + 1, 0:1]
        rows.append(start.astype(jnp.int32) + meta[2 + k:3 + k, :])
    row8 = lax.broadcasted_iota(jnp.int32, meta.shape, 0)
    pos_ref[...] = jnp.where(row8 == 0, rows[0], jnp.where(row8 == 1, rows[1], 0))


def _positions(meta, cnt):
    n = meta.shape[1]
    tile = 2048
    return pl.pallas_call(
        _pos_kernel,
        grid=(n // tile,),
        in_specs=[pl.BlockSpec((8, tile), lambda i: (0, i)),
                  pl.BlockSpec((N_EXPERTS, LANES), lambda i: (0, 0))],
        out_specs=pl.BlockSpec((8, tile), lambda i: (0, i)),
        out_shape=jax.ShapeDtypeStruct((8, n), jnp.int32),
        compiler_params=pltpu.CompilerParams(dimension_semantics=("arbitrary",)),
        name="moe_pos",
    )(meta, cnt)


def _plan_kernel(cnt_ref, pos0_ref, pos1_ref, tok_ref, te_ref, off_ref, *, max_tiles):
    i = pl.program_id(0)
    tp = pos0_ref.shape[0]
    padded, total = _segment_offsets(cnt_ref, off_ref)

    @pl.when(i == 0)
    def _():
        def clear(k, c):
            tok_ref[k] = 0
            return c
        for e in range(N_EXPERTS):
            lax.fori_loop(off_ref[e] + cnt_ref[e], off_ref[e] + padded[e], clear, 0)
        lax.fori_loop(total, max_tiles * MOE_TM, clear, 0)

        n_tiles = lax.div(total, MOE_TM)
        for e in range(N_EXPERTS):
            first = lax.div(off_ref[e], MOE_TM)
            count = lax.div(padded[e], MOE_TM)

            def fill(k, c, first=first, e=e):
                te_ref[first + k] = e
                return c
            lax.fori_loop(0, count, fill, 0)
        last = te_ref[n_tiles - 1]

        def fill_tail(k, c):
            te_ref[k] = last
            return c
        lax.fori_loop(n_tiles, max_tiles, fill_tail, 0)
        te_ref[max_tiles] = n_tiles

    def place(j, c):
        token = i * tp + j
        tok_ref[pos0_ref[j]] = token
        tok_ref[pos1_ref[j]] = token
        return c
    lax.fori_loop(0, tp, place, 0, unroll=8)


def _plan(counts, pos0, pos1, max_tiles):
    n = pos0.shape[0]
    smem = pl.BlockSpec(memory_space=pltpu.SMEM)
    vec = pl.BlockSpec((PERM_TP,), lambda i: (i,), memory_space=pltpu.SMEM)
    return pl.pallas_call(
        functools.partial(_plan_kernel, max_tiles=max_tiles),
        grid=(n // PERM_TP,),
        in_specs=[smem, vec, vec],
        out_specs=[smem, smem],
        out_shape=[jax.ShapeDtypeStruct((max_tiles * MOE_TM,), jnp.int32),
                   jax.ShapeDtypeStruct((max_tiles + 1,), jnp.int32)],
        scratch_shapes=[pltpu.SMEM((N_EXPERTS,), jnp.int32)],
        compiler_params=pltpu.CompilerParams(dimension_semantics=("arbitrary",)),
        name="moe_plan",
    )(counts, pos0, pos1)


def _experts_kernel(tok_ref, te_ref, layer_ref, x_ref, wg_ref, wu_ref, wd_ref, ys_ref,
                    xbuf_ref, wgb_ref, wub_ref, wdb_ref, gsem, *, max_tiles):
    i = pl.program_id(0)
    n_tiles = te_ref[max_tiles]
    slot = i & 1
    tile_rows = MOE_TM * TOKEN_TILE_ROWS

    def gather(tile, s):
        base = tile * MOE_TM

        def body(j8, c):
            for u in range(DMA_UNROLL):
                j = j8 * DMA_UNROLL + u
                _row_copy(x_ref, tok_ref[base + j], xbuf_ref.at[s], j, gsem.at[s]).start(priority=u % 2)
            return c
        lax.fori_loop(0, MOE_TM // DMA_UNROLL, body, 0)

    @pl.when(i == 0)
    def _():
        gather(0, 0)

    @pl.when(i + 1 < n_tiles)
    def _():
        gather(i + 1, 1 - slot)

    @pl.when(i < n_tiles)
    def _():
        pltpu.make_async_copy(x_ref.at[pl.ds(0, tile_rows), :], xbuf_ref.at[slot],
                              gsem.at[slot]).wait()

        @pl.when((i == 0) | (te_ref[i] != te_ref[jnp.maximum(i - 1, 0)]))
        def _():
            wgb_ref[...] = _bf16(wg_ref[...])
            wub_ref[...] = _bf16(wu_ref[...])
            wdb_ref[...] = _bf16(wd_ref[...])

        xb = _bf16(_from_token_tiles(xbuf_ref.at[slot], MOE_TM))
        gt = _dot(xb, wgb_ref[...])
        up = _dot(xb, wub_ref[...])
        h = gt / (1.0 + jnp.exp(-gt)) * up
        _to_token_tiles(ys_ref, _dot(_bf16(h), wdb_ref[...]))

    @pl.when(i >= n_tiles)
    def _():
        ys_ref[...] = jnp.zeros_like(ys_ref)


def _experts(tok, te, x1t, w_gate, w_up, w_down, layer, max_tiles):
    d, f = w_gate.shape[-2:]
    rows = MOE_TM * TOKEN_TILE_ROWS
    expert = lambda i, tok, te, lay: (lay[0], te[i], 0, 0)
    return pl.pallas_call(
        functools.partial(_experts_kernel, max_tiles=max_tiles),
        grid_spec=pltpu.PrefetchScalarGridSpec(
            num_scalar_prefetch=3,
            grid=(max_tiles,),
            in_specs=[pl.BlockSpec(memory_space=pl.ANY),
                      pl.BlockSpec((None, None, d, f), expert),
                      pl.BlockSpec((None, None, d, f), expert),
                      pl.BlockSpec((None, None, f, d), expert)],
            out_specs=pl.BlockSpec((rows, LANES), lambda i, tok, te, lay: (i, 0)),
            scratch_shapes=[pltpu.VMEM((2, rows, LANES), jnp.float32),
                            pltpu.VMEM((d, f), jnp.bfloat16),
                            pltpu.VMEM((d, f), jnp.bfloat16),
                            pltpu.VMEM((f, d), jnp.bfloat16),
                            pltpu.SemaphoreType.DMA((2,))]),
        out_shape=jax.ShapeDtypeStruct((max_tiles * rows, LANES), jnp.float32),
        compiler_params=pltpu.CompilerParams(dimension_semantics=("arbitrary",),
                                             vmem_limit_bytes=VMEM_LIMIT),
        name="moe_experts",
    )(tok, te, jnp.full((1,), layer, jnp.int32), x1t, w_gate, w_up, w_down)


def _combine_kernel(pos0_ref, pos1_ref, x_ref, gcol_ref, lng_ref, lnb_ref, ys_ref, o_ref,
                    ybuf_ref, sem, *, alpha):
    i = pl.program_id(0)
    n_steps = pl.num_programs(0)
    tm = x_ref.shape[0]
    slot = i & 1
    tile_rows = tm * TOKEN_TILE_ROWS

    def gather(step, s):
        base = step * tm

        def body(j, c):
            _row_copy(ys_ref, pos0_ref[base + j], ybuf_ref.at[s, 0], j, sem.at[s]).start(priority=0)
            _row_copy(ys_ref, pos1_ref[base + j], ybuf_ref.at[s, 1], j, sem.at[s]).start(priority=1)
            return c
        lax.fori_loop(0, tm, body, 0, unroll=DMA_UNROLL)

    @pl.when(i == 0)
    def _():
        gather(0, 0)

    @pl.when(i + 1 < n_steps)
    def _():
        gather(i + 1, 1 - slot)

    for k in range(2):
        pltpu.make_async_copy(ys_ref.at[pl.ds(0, tile_rows), :], ybuf_ref.at[slot, k],
                              sem.at[slot]).wait()

    lane = lax.broadcasted_iota(jnp.int32, (1, LANES), 1)
    gc = gcol_ref[...]
    g0 = jnp.sum(jnp.where(lane == 0, gc, 0.0), axis=-1, keepdims=True)
    g1 = jnp.sum(jnp.where(lane == 1, gc, 0.0), axis=-1, keepdims=True)
    y = (g0 * _from_token_tiles(ybuf_ref.at[slot, 0], tm)
         + g1 * _from_token_tiles(ybuf_ref.at[slot, 1], tm))
    o_ref[...] = _layer_norm(alpha * x_ref[...] + y, lng_ref[...], lnb_ref[...])


def _combine(pos0, pos1, x1, ys, gcol, ln_g, ln_b, alpha):
    n, d = x1.shape
    const = lambda i, p0, p1: (0, 0)
    tile = lambda i, p0, p1: (i, 0)
    return pl.pallas_call(
        functools.partial(_combine_kernel, alpha=alpha),
        grid_spec=pltpu.PrefetchScalarGridSpec(
            num_scalar_prefetch=2,
            grid=(n // ROW_TILE,),
            in_specs=[pl.BlockSpec((ROW_TILE, d), tile),
                      pl.BlockSpec((ROW_TILE, LANES), tile),
                      pl.BlockSpec((1, d), const), pl.BlockSpec((1, d), const),
                      pl.BlockSpec(memory_space=pl.ANY)],
            out_specs=pl.BlockSpec((ROW_TILE, d), tile),
            scratch_shapes=[pltpu.VMEM((2, 2, ROW_TILE * TOKEN_TILE_ROWS, LANES), jnp.float32),
                            pltpu.SemaphoreType.DMA((2,))]),
        out_shape=jax.ShapeDtypeStruct((n, d), jnp.float32),
        compiler_params=pltpu.CompilerParams(dimension_semantics=("arbitrary",),
                                             vmem_limit_bytes=VMEM_LIMIT),
        name="moe_combine_ln",
    )(pos0, pos1, x1, gcol, ln_g, ln_b, ys)


def _moe(x1, x1t, meta, gcol, cnt, w_gate, w_up, w_down, layer, ln_g, ln_b, alpha):
    n = x1.shape[0]
    max_tiles = (2 * n + N_EXPERTS * (MOE_TM - 1)) // MOE_TM
    counts = cnt[:, 0].astype(jnp.int32)
    pos = _positions(meta, cnt)
    pos0, pos1 = pos[0], pos[1]
    tok, te = _plan(counts, pos0, pos1, max_tiles)
    ys = _experts(tok, te, x1t, w_gate, w_up, w_down, layer, max_tiles)
    return _combine(pos0, pos1, x1, ys, gcol, ln_g, ln_b, alpha)


def _t5_bucket(dist):
    max_exact = N_BUCKETS // 2
    d = jnp.maximum(dist, 0)
    large = max_exact + (jnp.log(jnp.maximum(d, 1).astype(jnp.float32) / max_exact)
                         / math.log(MAX_DISTANCE / max_exact) * (N_BUCKETS - max_exact)).astype(jnp.int32)
    large = jnp.minimum(large, N_BUCKETS - 1)
    return jnp.where(d < max_exact, d, large)


def _tile_bias_kernel(w_ref, o_ref):
    w = w_ref[0]
    for q in range(BLOCK):
        o_ref[q:q + 1, :] = w[:, BLOCK - 1 - q:3 * BLOCK - 1 - q]


def _tile_bias(rel_bias):
    dist = jnp.arange(BLOCK + 1)
    rows = jnp.stack([rel_bias[_t5_bucket(dist * dilation)] for _, dilation in DIL_PATTERNS])
    n_br, _, heads = rows.shape
    table = rows.astype(jnp.float32).transpose(0, 2, 1).reshape(n_br * heads, BLOCK + 1)
    w = jnp.pad(table[:, ::-1], ((0, 0), (BLOCK - 1, BLOCK)))[:, None, :]
    return pl.pallas_call(
        _tile_bias_kernel,
        grid=(n_br * heads,),
        in_specs=[pl.BlockSpec((1, 1, 3 * BLOCK), lambda s: (s, 0, 0))],
        out_specs=pl.BlockSpec((None, BLOCK, 2 * BLOCK), lambda s: (s, 0, 0)),
        out_shape=jax.ShapeDtypeStruct((n_br * heads, BLOCK, 2 * BLOCK), jnp.float32),
        compiler_params=pltpu.CompilerParams(dimension_semantics=("arbitrary",)),
        name="tile_bias",
    )(w)


def kernel(x, w_in, g_sb, g_dil, w_out, ln1_g, ln1_b, ln2_g, ln2_b, rel_bias,
           w_router, b_router, w_gate, w_up, w_down):
    batch, seq, d = x.shape
    depth = w_in.shape[0]
    alpha = (2.0 * depth) ** 0.25
    n = batch * seq
    x2d = x.reshape(n, d)

    tile_bias = _tile_bias(rel_bias)
    wrt =w_router.T.astype(jnp.float32)
    wrt_hi = _bf16(wrt)
    wrt_lo = _bf16(wrt - wrt_hi.astype(jnp.float32))
    br = b_router.astype(jnp.float32).reshape(N_EXPERTS, 1)

    for l in range(depth):
        w_in_l = _bf16(w_in[l])
        proj_sb, proj_dl = _in_proj(x2d, w_in_l)
        y_sb = _sb_attention(proj_sb, g_sb[l].reshape(N_PAIRS_SB, 1, LANES), batch, seq)
        y_dl = _dil_attention(proj_dl, g_dil[l].reshape(N_PAIRS_DIL, 1, LANES),
                              tile_bias, batch, seq)
        x1, x1t, meta, gcol, cnt = _out_proj(x2d, y_sb, y_dl, _bf16(w_out[l]),
                                             ln1_g[l].reshape(1, d), ln1_b[l].reshape(1, d),
                                             wrt_hi, wrt_lo, br, alpha)
        x2d = _moe(x1, x1t, meta, gcol, cnt, w_gate, w_up, w_down, l,
                   ln2_g[l].reshape(1, d), ln2_b[l].reshape(1, d), alpha)
    return x2d.reshape(batch, seq, d)
```

```python
import functools
import math

import jax
import jax.numpy as jnp
from jax import lax
from jax.experimental import pallas as pl
from jax.experimental.pallas import tpu as pltpu

D_MODEL = 1024
HEAD_DIM = 64
N_HEADS_SB = 8
N_HEADS_DIL = 8
D_SB = N_HEADS_SB * HEAD_DIM
D_DIL = N_HEADS_DIL * HEAD_DIM
DIL_PATTERNS = ((128, 1), (512, 4), (2048, 16))
BLOCK = 128
N_BUCKETS = 32
MAX_DISTANCE = 2048
N_EXPERTS = 16
N_GROUPS = 4
EXPERTS_PER_GROUP = N_EXPERTS // N_GROUPS
D_FF_EXPERT = 1024
LN_EPS = 1e-5
NEG_INF = -1e30

LANES = 128
N_PAIRS_SB = D_SB // LANES
N_PAIRS_DIL = D_DIL // LANES
VMEM_LIMIT = 56 * 1024 * 1024

ROW_TILE = 1024
SB_TQ = 256
SB_TK = 256
SB_TILES_PER_STEP = 4
SB_SKIP_BOUND = -110.0
DIL_SUPER = 2048
DIL_UNITS = DIL_SUPER // BLOCK
TOKEN_TILE_ROWS = D_MODEL // LANES
MOE_TM = 256
PERM_TP = 512
COMBINE_TM = 512
DMA_UNROLL = 8

_NT = (((1,), (1,)), ((), ()))


def _bf16(x):
    return x.astype(jnp.bfloat16)


def _split_bf16(x):
    hi = _bf16(x)
    lo = _bf16(x - hi.astype(jnp.float32))
    return hi, lo


def _dot(a, b):
    return jnp.dot(a, b, preferred_element_type=jnp.float32)


def _dot_nt(a, b):
    return lax.dot_general(a, b, _NT, preferred_element_type=jnp.float32)


def _head_rms_gain(o, g):
    r = lax.broadcasted_iota(jnp.int32, (LANES, LANES), 0) // HEAD_DIM
    c = lax.broadcasted_iota(jnp.int32, (LANES, LANES), 1) // HEAD_DIM
    same_head = jnp.where(r == c, 1.0 / HEAD_DIM, 0.0).astype(jnp.bfloat16)
    hi, lo = _split_bf16(o * o)
    ms = _dot(hi, same_head) + _dot(lo, same_head)
    return o * lax.rsqrt(ms + 1e-6) * g


def _layer_norm(x, g, b):
    mu = jnp.mean(x, axis=-1, keepdims=True)
    xc = x - mu
    var = jnp.mean(xc * xc, axis=-1, keepdims=True)
    return xc * lax.rsqrt(var + LN_EPS) * g + b


def _in_proj_kernel(x_ref, w_ref, sb_ref, dl_ref):
    x = _bf16(x_ref[...])
    scale = HEAD_DIM ** -0.5
    col = 0
    for o_ref, n_q in ((sb_ref, N_PAIRS_SB), (dl_ref, N_PAIRS_DIL)):
        for j in range(0, o_ref.shape[0], 2):
            res = _dot(x, w_ref[:, col:col + 2 * LANES])
            col += 2 * LANES
            for jj in range(2):
                blk = res[:, jj * LANES:(jj + 1) * LANES]
                if j + jj < n_q:
                    blk = blk * scale
                o_ref[j + jj] = blk.astype(o_ref.dtype)


def _in_proj(x2d, w):
    n, d = x2d.shape
    slabs = lambda k: pl.BlockSpec((k, ROW_TILE, LANES), lambda i: (0, i, 0))
    return pl.pallas_call(
        _in_proj_kernel,
        grid=(n // ROW_TILE,),
        in_specs=[pl.BlockSpec((ROW_TILE, d), lambda i: (i, 0)),
                  pl.BlockSpec((d, w.shape[1]), lambda i: (0, 0))],
        out_specs=[slabs(3 * N_PAIRS_SB), slabs(3 * N_PAIRS_DIL)],
        out_shape=[jax.ShapeDtypeStruct((3 * N_PAIRS_SB, n, LANES), jnp.bfloat16),
                   jax.ShapeDtypeStruct((3 * N_PAIRS_DIL, n, LANES), jnp.float32)],
        compiler_params=pltpu.CompilerParams(dimension_semantics=("arbitrary",),
                                             vmem_limit_bytes=VMEM_LIMIT),
        name="in_proj",
    )(x2d, w)


def _sb_kernel(q_ref, k_ref, v_ref, g_ref, o_ref, carry_ref, acc_ref):
    step = pl.program_id(2)
    lane = lax.broadcasted_iota(jnp.int32, (1, LANES), 1)
    first_head = lane < HEAD_DIM

    jj = lax.broadcasted_iota(jnp.int32, (SB_TK, SB_TK + LANES), 0)
    ss = lax.broadcasted_iota(jnp.int32, (SB_TK, SB_TK + LANES), 1)
    suffix = jnp.where((jj > ss) | (ss >= SB_TK), 1.0, 0.0).astype(jnp.bfloat16)

    tq_i = lax.broadcasted_iota(jnp.int32, (2 * SB_TQ, SB_TK), 0) % SB_TQ
    ts_i = lax.broadcasted_iota(jnp.int32, (2 * SB_TQ, SB_TK), 1)
    causal = ts_i < tq_i

    def block(q2, kb, diag, carry, acc, weight=None):
        off = pl.multiple_of(kb * SB_TK, SB_TK)
        kblk = k_ref[pl.ds(off, SB_TK), :]
        vblk = v_ref[pl.ds(off, SB_TK), :]
        z = _dot_nt(q2, kblk)
        sp = jnp.log(1.0 + jnp.exp(-jnp.abs(z)))
        log_beta = jnp.minimum(z, 0.0) - sp
        log_1m = log_beta - z
        if diag:
            log_1m = jnp.where(causal, log_1m, 0.0)
        sfx = _dot(_bf16(log_1m), suffix)
        between = sfx[:, :SB_TK] + jnp.concatenate([carry] * (SB_TK // LANES), axis=1)
        a = jnp.exp(log_beta + between)
        if diag:
            a = jnp.where(causal, a, 0.0)
        out = _dot(_bf16(a), vblk)
        tot = sfx[:, SB_TK:]
        if weight is not None:
            out = out * weight
            tot = tot * weight
        return carry + tot, acc + out

    zeros = jnp.zeros((2 * SB_TQ, LANES), jnp.float32)
    tiles = []
    for t in range(SB_TILES_PER_STEP):
        i = step * SB_TILES_PER_STEP + t
        q = q_ref[t * SB_TQ:(t + 1) * SB_TQ, :]
        zero = jnp.zeros_like(q)
        q2 = jnp.concatenate([jnp.where(first_head, q, zero), jnp.where(first_head, zero, q)], axis=0)
        carry, acc = block(q2, i, True, zeros, zeros)
        has_prev = jnp.where(i >= 1, 1.0, 0.0)
        carry, acc = block(q2, jnp.maximum(i - 1, 0), False, carry, acc, weight=has_prev)
        carry_ref[t] = carry
        acc_ref[t] = acc
        tiles.append((i, q2, jnp.max(carry)))

    for t, (i, q2, mx0) in enumerate(tiles):
        def cond(state):
            kb, mx = state
            return jnp.logical_and(kb >= 0, mx > SB_SKIP_BOUND)

        def body(state, t=t, q2=q2):
            kb, _ = state
            c, a = block(q2, kb, False, carry_ref[t], acc_ref[t])
            carry_ref[t] = c
            acc_ref[t] = a
            return kb - 1, jnp.max(c)

        lax.while_loop(cond, body, (i - 2, mx0))

        o = jnp.where(first_head, acc_ref[t, :SB_TQ, :], acc_ref[t, SB_TQ:, :])
        o_ref[t * SB_TQ:(t + 1) * SB_TQ, :] = _head_rms_gain(o, g_ref[...]).astype(o_ref.dtype)


def _sb_attention(proj, g, batch, seq):
    n = batch * seq
    rows = SB_TQ * SB_TILES_PER_STEP
    nq = seq // rows
    return pl.pallas_call(
        _sb_kernel,
        grid=(batch, N_PAIRS_SB, nq),
        in_specs=[
            pl.BlockSpec((None, rows, LANES), lambda b, p, i: (p, b * nq + i, 0)),
            pl.BlockSpec((None, seq, LANES), lambda b, p, i: (N_PAIRS_SB + p, b, 0)),
            pl.BlockSpec((None, seq, LANES), lambda b, p, i: (2 * N_PAIRS_SB + p, b, 0)),
            pl.BlockSpec((None, 1, LANES), lambda b, p, i: (p, 0, 0)),
        ],
        out_specs=pl.BlockSpec((None, rows, LANES), lambda b, p, i: (p, b * nq + i, 0)),
        out_shape=jax.ShapeDtypeStruct((N_PAIRS_SB, n, LANES), jnp.bfloat16),
        scratch_shapes=[pltpu.VMEM((SB_TILES_PER_STEP, 2 * SB_TQ, LANES), jnp.float32),
                        pltpu.VMEM((SB_TILES_PER_STEP, 2 * SB_TQ, LANES), jnp.float32)],
        compiler_params=pltpu.CompilerParams(
            dimension_semantics=("arbitrary", "arbitrary", "arbitrary"),
            vmem_limit_bytes=VMEM_LIMIT),
        name="sb_attention",
    )(proj, proj, proj, g)


def _dil_kernel(tbias_ref, q_ref, k_ref, v_ref, g_ref, o_ref,
                bias_ref, obr_ref, lse_ref):
    b = pl.program_id(0)
    p = pl.program_id(1)
    i = pl.program_id(2)
    n_br = len(DIL_PATTERNS)

    qi = lax.broadcasted_iota(jnp.int32, (BLOCK, 2 * BLOCK), 0)
    kj = lax.broadcasted_iota(jnp.int32, (BLOCK, 2 * BLOCK), 1)
    steps = qi + BLOCK - kj

    assert all(window // dilation == BLOCK for window, dilation in DIL_PATTERNS)
    valid = (steps >= 0) & (steps <= BLOCK)

    @pl.when((b == 0) & (p == 0) & (i == 0))
    def _():
        n_tiles = n_br * N_HEADS_DIL

        def per_tile(s, _):
            tile = tbias_ref[s]
            rows = pl.ds(pl.multiple_of(s * BLOCK, BLOCK), BLOCK)
            bias_ref[rows, :] = jnp.where(valid, tile, NEG_INF)
            rows = pl.ds(pl.multiple_of((n_tiles + s) * BLOCK, BLOCK), BLOCK)
            bias_ref[rows, :] = jnp.where(valid & (kj >= BLOCK), tile, NEG_INF)
            return 0
        lax.fori_loop(0, n_br * N_HEADS_DIL, per_tile, 0)

    lane = lax.broadcasted_iota(jnp.int32, (1, LANES), 1)
    first_head = lane < HEAD_DIM
    t0 = i * DIL_SUPER

    def strided(ref, start, dilation):
        if dilation == 1:
            return ref[pl.ds(start, BLOCK), :]
        return ref[pl.ds(start, BLOCK, stride=dilation), :]

    for gidx, (window, dilation) in enumerate(DIL_PATTERNS):
        units_per_residue = DIL_UNITS // dilation
        shift = units_per_residue.bit_length() - 1

        def unit(u, _, gidx=gidx, dilation=dilation, units_per_residue=units_per_residue, shift=shift):
            r = lax.shift_right_logical(u, shift)
            cb = u & (units_per_residue - 1)
            sq = r + cb * (dilation * BLOCK)
            cur = t0 + sq
            prev = cur - dilation * BLOCK
            prev_ok = prev >= 0
            prev_c = jnp.where(prev_ok, prev, cur)
            no_prev = jnp.where(prev_ok, 0, 1)

            qb = _bf16(strided(q_ref, sq, dilation))
            kk = _bf16(jnp.concatenate([strided(k_ref, prev_c, dilation),
                                        strided(k_ref, cur, dilation)], axis=0))
            vv = _bf16(jnp.concatenate([strided(v_ref, prev_c, dilation),
                                        strided(v_ref, cur, dilation)], axis=0))
            zero = jnp.zeros_like(qb)
            q2 = jnp.concatenate([jnp.where(first_head, qb, zero),
                                  jnp.where(first_head, zero, qb)], axis=0)
            z = _dot_nt(q2, kk)
            slot = no_prev * (n_br * N_HEADS_DIL) + gidx * N_HEADS_DIL + 2 * p
            lg = z + bias_ref[pl.ds(pl.multiple_of(slot * BLOCK, 2 * BLOCK), 2 * BLOCK), :]
            m = jnp.max(lg, axis=-1, keepdims=True)
            pe = jnp.exp(lg - m)
            res = _dot(_bf16(pe), jnp.concatenate([vv, jnp.ones_like(vv)], axis=1))
            den = res[:, LANES:]
            out = res[:, :LANES] / den
            lse = m + jnp.log(den)
            o_tile = jnp.where(first_head, out[:BLOCK], out[BLOCK:])
            l_tile = jnp.where(first_head, lse[:BLOCK], lse[BLOCK:])
            if dilation == 1:
                idx = pl.ds(sq, BLOCK)
            else:
                idx = pl.ds(sq, BLOCK, stride=dilation)
            obr_ref.at[gidx][idx, :] = o_tile
            lse_ref.at[gidx][idx, :] = l_tile
            return 0

        lax.fori_loop(0, DIL_UNITS, unit, 0, unroll=8)

    chunk = 256

    def combine(c, _):
        rows = pl.ds(pl.multiple_of(c * chunk, chunk), chunk)
        ls = [lse_ref[gidx, rows, :] for gidx in range(n_br)]
        m = functools.reduce(jnp.maximum, ls)
        es = [jnp.exp(l - m) for l in ls]
        num = sum(e * obr_ref[gidx, rows, :] for gidx, e in enumerate(es))
        o = num / sum(es)
        o_ref[rows, :] = _head_rms_gain(o, g_ref[...]).astype(o_ref.dtype)
        return 0

    lax.fori_loop(0, DIL_SUPER // chunk, combine, 0)


def _dil_attention(proj, g, tile_bias, batch, seq):
    n = batch * seq
    ns = seq // DIL_SUPER
    n_br = len(DIL_PATTERNS)
    return pl.pallas_call(
        _dil_kernel,
        grid=(batch, N_PAIRS_DIL, ns),
        in_specs=[
            pl.BlockSpec((n_br * N_HEADS_DIL, BLOCK, 2 * BLOCK), lambda b, p, i: (0, 0, 0)),
            pl.BlockSpec((None, DIL_SUPER, LANES), lambda b, p, i: (p, b * ns + i, 0)),
            pl.BlockSpec((None, seq, LANES), lambda b, p, i: (N_PAIRS_DIL + p, b, 0)),
            pl.BlockSpec((None, seq, LANES), lambda b, p, i: (2 * N_PAIRS_DIL + p, b, 0)),
            pl.BlockSpec((None, 1, LANES), lambda b, p, i: (p, 0, 0)),
        ],
        out_specs=pl.BlockSpec((None, DIL_SUPER, LANES), lambda b, p, i: (p, b * ns + i, 0)),
        out_shape=jax.ShapeDtypeStruct((N_PAIRS_DIL, n, LANES), jnp.bfloat16),
        scratch_shapes=[pltpu.VMEM((2 * n_br * N_HEADS_DIL * BLOCK, 2 * BLOCK), jnp.float32),
                        pltpu.VMEM((n_br, DIL_SUPER, LANES), jnp.float32),
                        pltpu.VMEM((n_br, DIL_SUPER, LANES), jnp.float32)],
        compiler_params=pltpu.CompilerParams(
            dimension_semantics=("arbitrary", "arbitrary", "arbitrary"),
            vmem_limit_bytes=VMEM_LIMIT),
        name="dilated_attention",
    )(tile_bias, proj, proj, proj, g)


def _router(x1, wrt_hi, wrt_lo, br):
    tm = x1.shape[0]
    xh, xl = _split_bf16(x1)
    lt = _dot_nt(wrt_hi, xh) + _dot_nt(wrt_hi, xl) + _dot_nt(wrt_lo, xh) + br
    rows = [lt[e:e + 1, :] for e in range(N_EXPERTS)]
    m = functools.reduce(jnp.maximum, rows)
    ex = [jnp.exp(r - m) for r in rows]
    tot = functools.reduce(lambda a, c: a + c, ex)
    probs = [e / tot for e in ex]

    keep, score = [], []
    for gi in range(N_GROUPS):
        members = range(gi * EXPERTS_PER_GROUP, (gi + 1) * EXPERTS_PER_GROUP)
        s = None
        for e in members:
            rank = None
            for o in members:
                if o == e:
                    continue
                beats = (probs[o] > probs[e]) | ((probs[o] == probs[e]) & (o < e))
                beats = beats.astype(jnp.int32)
                rank = beats if rank is None else rank + beats
            k = rank < 2
            keep.append(k)
            contrib = jnp.where(k, probs[e], 0.0)
            s = contrib if s is None else s + contrib
        score.append(s)

    sel, gates = [], []
    for gi in range(N_GROUPS):
        chosen = None
        for o in range(N_GROUPS):
            if o == gi:
                continue
            c = (score[gi] > score[o]) if o < gi else (score[gi] >= score[o])
            chosen = c if chosen is None else (chosen & c)
        for e in range(gi * EXPERTS_PER_GROUP, (gi + 1) * EXPERTS_PER_GROUP):
            s = keep[e] & chosen
            sel.append(s)
            gates.append(jnp.where(s, probs[e] / score[gi], 0.0))

    idx0 = functools.reduce(jnp.minimum, [jnp.where(sel[e], e, N_EXPERTS) for e in range(N_EXPERTS)])
    idx1 = functools.reduce(jnp.maximum, [jnp.where(sel[e], e, -1) for e in range(N_EXPERTS)])
    g0 = sum(jnp.where(idx0 == e, gates[e], 0.0) for e in range(N_EXPERTS))
    g1 = sum(jnp.where(idx1 == e, gates[e], 0.0) for e in range(N_EXPERTS))

    row_id = lax.broadcasted_iota(jnp.int32, (N_EXPERTS, tm), 0)
    selmat = jnp.zeros((N_EXPERTS, tm), jnp.float32)
    for e in range(N_EXPERTS):
        selmat = jnp.where((row_id == e) & sel[e], 1.0, selmat)
    return idx0, idx1, g0, g1, selmat


def _to_token_tiles(ref, x):
    rows = x.shape[0]
    for s in range(TOKEN_TILE_ROWS):
        ref[pl.ds(s, rows, stride=TOKEN_TILE_ROWS), :] = x[:, s * LANES:(s + 1) * LANES]


def _from_token_tiles(ref, rows):
    return jnp.concatenate([ref[pl.ds(s, rows, stride=TOKEN_TILE_ROWS), :]
                            for s in range(TOKEN_TILE_ROWS)], axis=1)


def _out_proj_kernel(x_ref, ysb_ref, ydl_ref, w_ref, lng_ref, lnb_ref,
                     wrh_ref, wrl_ref, br_ref,
                     x1_ref, x1t_ref, meta_ref, gcol_ref, cnt_ref, carry_ref, *, alpha):
    i = pl.program_id(0)
    tm = x_ref.shape[0]

    @pl.when(i == 0)
    def _():
        carry_ref[...] = jnp.zeros_like(carry_ref)

    y = jnp.concatenate([ysb_ref[j] for j in range(N_PAIRS_SB)]
                        + [ydl_ref[j] for j in range(N_PAIRS_DIL)], axis=1)
    h = _dot(y, w_ref[...])
    x1 = _layer_norm(alpha * x_ref[...] + h, lng_ref[...], lnb_ref[...])
    x1_ref[...] = x1
    _to_token_tiles(x1t_ref, x1)

    idx0, idx1, g0, g1, selmat = _router(x1, wrh_ref[...], wrl_ref[...], br_ref[...])

    jj = lax.broadcasted_iota(jnp.int32, (tm, tm + LANES), 0)
    ss = lax.broadcasted_iota(jnp.int32, (tm, tm + LANES), 1)
    before = jnp.where((jj < ss) | (ss >= tm), 1.0, 0.0).astype(jnp.bfloat16)
    pc = _dot(_bf16(selmat), before)
    carry = carry_ref[...]
    rankmat = pc[:, :tm] + jnp.concatenate([carry] * (tm // LANES), axis=1)
    carry = carry + pc[:, tm:]
    carry_ref[...] = carry
    cnt_ref[...] = carry

    rank0 = sum(jnp.where(idx0 == e, rankmat[e:e + 1, :], 0.0) for e in range(N_EXPERTS))
    rank1 = sum(jnp.where(idx1 == e, rankmat[e:e + 1, :], 0.0) for e in range(N_EXPERTS))

    row8 = lax.broadcasted_iota(jnp.int32, (8, tm), 0)
    meta = jnp.zeros((8, tm), jnp.int32)
    for r, v in enumerate((idx0, idx1, rank0.astype(jnp.int32), rank1.astype(jnp.int32))):
        meta = jnp.where(row8 == r, v, meta)
    meta_ref[...] = meta

    rowl = lax.broadcasted_iota(jnp.int32, (LANES, tm), 0)
    gt = jnp.where(rowl == 0, g0, jnp.where(rowl == 1, g1, 0.0))
    gcol_ref[...] = gt.T


def _out_proj(x2d, y_sb, y_dl, w_out, ln_g, ln_b, wrt_hi, wrt_lo, br, alpha):
    n, d = x2d.shape
    const = lambda i: (0, 0)
    return pl.pallas_call(
        functools.partial(_out_proj_kernel, alpha=alpha),
        grid=(n // ROW_TILE,),
        in_specs=[
            pl.BlockSpec((ROW_TILE, d), lambda i: (i, 0)),
            pl.BlockSpec((N_PAIRS_SB, ROW_TILE, LANES), lambda i: (0, i, 0)),
            pl.BlockSpec((N_PAIRS_DIL, ROW_TILE, LANES), lambda i: (0, i, 0)),
            pl.BlockSpec((d, d), const),
            pl.BlockSpec((1, d), const),
            pl.BlockSpec((1, d), const),
            pl.BlockSpec((N_EXPERTS, d), const),
            pl.BlockSpec((N_EXPERTS, d), const),
            pl.BlockSpec((N_EXPERTS, 1), const),
        ],
        out_specs=[pl.BlockSpec((ROW_TILE, d), lambda i: (i, 0)),
                   pl.BlockSpec((ROW_TILE * TOKEN_TILE_ROWS, LANES), lambda i: (i, 0)),
                   pl.BlockSpec((8, ROW_TILE), lambda i: (0, i)),
                   pl.BlockSpec((ROW_TILE, LANES), lambda i: (i, 0)),
                   pl.BlockSpec((N_EXPERTS, LANES), const)],
        out_shape=[jax.ShapeDtypeStruct((n, d), jnp.float32),
                   jax.ShapeDtypeStruct((n * TOKEN_TILE_ROWS, LANES), jnp.float32),
                   jax.ShapeDtypeStruct((8, n), jnp.int32),
                   jax.ShapeDtypeStruct((n, LANES), jnp.float32),
                   jax.ShapeDtypeStruct((N_EXPERTS, LANES), jnp.float32)],
        scratch_shapes=[pltpu.VMEM((N_EXPERTS, LANES), jnp.float32)],
        compiler_params=pltpu.CompilerParams(dimension_semantics=("arbitrary",),
                                             vmem_limit_bytes=VMEM_LIMIT),
        name="out_proj_ln_router",
    )(x2d, y_sb, y_dl, w_out, ln_g, ln_b, wrt_hi, wrt_lo, br)


def _segment_offsets(cnt_ref, off_ref):
    acc = jnp.int32(0)
    padded = []
    for e in range(N_EXPERTS):
        off_ref[e] = acc
        pc = lax.div(cnt_ref[e] + (MOE_TM - 1), MOE_TM) * MOE_TM
        padded.append(pc)
        acc = acc + pc
    return padded, acc


def _row_copy(src_ref, src_row, dst_ref, dst_row, sem):
    def tile(row):
        if isinstance(row, int):
            return pl.ds(row * TOKEN_TILE_ROWS, TOKEN_TILE_ROWS)
        return pl.ds(pl.multiple_of(row * TOKEN_TILE_ROWS, TOKEN_TILE_ROWS), TOKEN_TILE_ROWS)
    return pltpu.make_async_copy(src_ref.at[tile(src_row), :], dst_ref.at[tile(dst_row), :], sem)


def _pos_kernel(meta_ref, cnt_ref, pos_ref):
    cnt = cnt_ref[...]
    padded = jnp.floor((cnt + (MOE_TM - 1)) * (1.0 / MOE_TM)) * MOE_TM
    meta = meta_ref[...]
    rows = []
    for k in range(2):
        e = meta[k:k + 1, :]
        start = jnp.zeros(e.shape, jnp.float32)
        off = jnp.zeros((1, 1), jnp.float32)
        for ex in range(N_EXPERTS):
            start = jnp.where(e == ex, off, start)
            off = off + padded[ex:ex + 1, 0:1]
        rows.append(start.astype(jnp.int32) + meta[2 + k:3 + k, :])
    row8 = lax.broadcasted_iota(jnp.int32, meta.shape, 0)
    pos_ref[...] = jnp.where(row8 == 0, rows[0], jnp.where(row8 == 1, rows[1], 0))


def _positions(meta, cnt):
    n = meta.shape[1]
    tile = 2048
    return pl.pallas_call(
        _pos_kernel,
        grid=(n // tile,),
        in_specs=[pl.BlockSpec((8, tile), lambda i: (0, i)),
                  pl.BlockSpec((N_EXPERTS, LANES), lambda i: (0, 0))],
        out_specs=pl.BlockSpec((8, tile), lambda i: (0, i)),
        out_shape=jax.ShapeDtypeStruct((8, n), jnp.int32),
        compiler_params=pltpu.CompilerParams(dimension_semantics=("arbitrary",)),
        name="moe_pos",
    )(meta, cnt)


def _plan_kernel(cnt_ref, pos0_ref, pos1_ref, tok_ref, te_ref, off_ref, *, max_tiles):
    i = pl.program_id(0)
    tp = pos0_ref.shape[0]

    @pl.when(i == 0)
    def _():
        padded, total = _segment_offsets(cnt_ref, off_ref)

        def clear(k, c):
            tok_ref[k] = 0
            return c
        for e in range(N_EXPERTS):
            lax.fori_loop(off_ref[e] + cnt_ref[e], off_ref[e] + padded[e], clear, 0)
        lax.fori_loop(total, max_tiles * MOE_TM, clear, 0)

        n_tiles = lax.div(total, MOE_TM)
        for e in range(N_EXPERTS):
            first = lax.div(off_ref[e], MOE_TM)
            count = lax.div(padded[e], MOE_TM)

            def fill(k, c, first=first, e=e):
                te_ref[first + k] = e
                return c
            lax.fori_loop(0, count, fill, 0)
        last = te_ref[n_tiles - 1]

        def fill_tail(k, c):
            te_ref[k] = last
            return c
        lax.fori_loop(n_tiles, max_tiles, fill_tail, 0)
        te_ref[max_tiles] = n_tiles

    def place(j, c):
        token = i * tp + j
        tok_ref[pos0_ref[j]] = token
        tok_ref[pos1_ref[j]] = token
        return c
    lax.fori_loop(0, tp, place, 0, unroll=8)


def _plan(counts, pos0, pos1, max_tiles):
    n = pos0.shape[0]
    smem = pl.BlockSpec(memory_space=pltpu.SMEM)
    vec = pl.BlockSpec((PERM_TP,), lambda i: (i,), memory_space=pltpu.SMEM)
    return pl.pallas_call(
        functools.partial(_plan_kernel, max_tiles=max_tiles),
        grid=(n // PERM_TP,),
        in_specs=[smem, vec, vec],
        out_specs=[smem, smem],
        out_shape=[jax.ShapeDtypeStruct((max_tiles * MOE_TM,), jnp.int32),
                   jax.ShapeDtypeStruct((max_tiles + 1,), jnp.int32)],
        scratch_shapes=[pltpu.SMEM((N_EXPERTS,), jnp.int32)],
        compiler_params=pltpu.CompilerParams(dimension_semantics=("arbitrary",)),
        name="moe_plan",
    )(counts, pos0, pos1)


def _experts_kernel(tok_ref, te_ref, layer_ref, x_ref, wg_ref, wu_ref, wd_ref, ys_ref,
                    xbuf_ref, wgb_ref, wub_ref, wdb_ref, gsem, *, max_tiles):
    i = pl.program_id(0)
    n_tiles = te_ref[max_tiles]
    slot = i & 1
    tile_rows = MOE_TM * TOKEN_TILE_ROWS

    def gather(tile, s):
        base = tile * MOE_TM

        def body(j8, c):
            for u in range(DMA_UNROLL):
                j = j8 * DMA_UNROLL + u
                _row_copy(x_ref, tok_ref[base + j], xbuf_ref.at[s], j, gsem.at[s]).start(priority=u % 2)
            return c
        lax.fori_loop(0, MOE_TM // DMA_UNROLL, body, 0)

    @pl.when(i == 0)
    def _():
        gather(0, 0)

    @pl.when(i + 1 < n_tiles)
    def _():
        gather(i + 1, 1 - slot)

    @pl.when(i < n_tiles)
    def _():
        pltpu.make_async_copy(x_ref.at[pl.ds(0, tile_rows), :], xbuf_ref.at[slot],
                              gsem.at[slot]).wait()

        @pl.when((i == 0) | (te_ref[i] != te_ref[jnp.maximum(i - 1, 0)]))
        def _():
            wgb_ref[...] = _bf16(wg_ref[...])
            wub_ref[...] = _bf16(wu_ref[...])
            wdb_ref[...] = _bf16(wd_ref[...])

        xb = _bf16(_from_token_tiles(xbuf_ref.at[slot], MOE_TM))
        gt = _dot(xb, wgb_ref[...])
        up = _dot(xb, wub_ref[...])
        h = gt / (1.0 + jnp.exp(-gt)) * up
        _to_token_tiles(ys_ref, _dot(_bf16(h), wdb_ref[...]))

    @pl.when(i >= n_tiles)
    def _():
        ys_ref[...] = jnp.zeros_like(ys_ref)


def _experts(tok, te, x1t, w_gate, w_up, w_down, layer, max_tiles):
    d, f = w_gate.shape[-2:]
    rows = MOE_TM * TOKEN_TILE_ROWS
    expert = lambda i, tok, te, lay: (lay[0], te[i], 0, 0)
    return pl.pallas_call(
        functools.partial(_experts_kernel, max_tiles=max_tiles),
        grid_spec=pltpu.PrefetchScalarGridSpec(
            num_scalar_prefetch=3,
            grid=(max_tiles,),
            in_specs=[pl.BlockSpec(memory_space=pl.ANY),
                      pl.BlockSpec((None, None, d, f), expert),
                      pl.BlockSpec((None, None, d, f), expert),
                      pl.BlockSpec((None, None, f, d), expert)],
            out_specs=pl.BlockSpec((rows, LANES), lambda i, tok, te, lay: (i, 0)),
            scratch_shapes=[pltpu.VMEM((2, rows, LANES), jnp.float32),
                            pltpu.VMEM((d, f), jnp.bfloat16),
                            pltpu.VMEM((d, f), jnp.bfloat16),
                            pltpu.VMEM((f, d), jnp.bfloat16),
                            pltpu.SemaphoreType.DMA((2,))]),
        out_shape=jax.ShapeDtypeStruct((max_tiles * rows, LANES), jnp.float32),
        compiler_params=pltpu.CompilerParams(dimension_semantics=("arbitrary",),
                                             vmem_limit_bytes=VMEM_LIMIT),
        name="moe_experts",
    )(tok, te, jnp.full((1,), layer, jnp.int32), x1t, w_gate, w_up, w_down)


def _combine_kernel(pos0_ref, pos1_ref, x_ref, gcol_ref, lng_ref, lnb_ref, ys_ref, o_ref,
                    ybuf_ref, sem, *, alpha):
    i = pl.program_id(0)
    n_steps = pl.num_programs(0)
    tm = x_ref.shape[0]
    slot = i & 1
    tile_rows = tm * TOKEN_TILE_ROWS

    def gather(step, s):
        base = step * tm

        def body(j, c):
            _row_copy(ys_ref, pos0_ref[base + j], ybuf_ref.at[s, 0], j, sem.at[s]).start(priority=0)
            _row_copy(ys_ref, pos1_ref[base + j], ybuf_ref.at[s, 1], j, sem.at[s]).start(priority=1)
            return c
        lax.fori_loop(0, tm, body, 0, unroll=DMA_UNROLL)

    @pl.when(i == 0)
    def _():
        gather(0, 0)

    @pl.when(i + 1 < n_steps)
    def _():
        gather(i + 1, 1 - slot)

    for k in range(2):
        pltpu.make_async_copy(ys_ref.at[pl.ds(0, tile_rows), :], ybuf_ref.at[slot, k],
                              sem.at[slot]).wait()

    lane = lax.broadcasted_iota(jnp.int32, (1, LANES), 1)
    gc = gcol_ref[...]
    g0 = jnp.sum(jnp.where(lane == 0, gc, 0.0), axis=-1, keepdims=True)
    g1 = jnp.sum(jnp.where(lane == 1, gc, 0.0), axis=-1, keepdims=True)
    y = (g0 * _from_token_tiles(ybuf_ref.at[slot, 0], tm)
         + g1 * _from_token_tiles(ybuf_ref.at[slot, 1], tm))
    o_ref[...] = _layer_norm(alpha * x_ref[...] + y, lng_ref[...], lnb_ref[...])


def _combine(pos0, pos1, x1, ys, gcol, ln_g, ln_b, alpha):
    n, d = x1.shape
    const = lambda i, p0, p1: (0, 0)
    tile = lambda i, p0, p1: (i, 0)
    return pl.pallas_call(
        functools.partial(_combine_kernel, alpha=alpha),
        grid_spec=pltpu.PrefetchScalarGridSpec(
            num_scalar_prefetch=2,
            grid=(n // COMBINE_TM,),
            in_specs=[pl.BlockSpec((COMBINE_TM, d), tile),
                      pl.BlockSpec((COMBINE_TM, LANES), tile),
                      pl.BlockSpec((1, d), const), pl.BlockSpec((1, d), const),
                      pl.BlockSpec(memory_space=pl.ANY)],
            out_specs=pl.BlockSpec((COMBINE_TM, d), tile),
            scratch_shapes=[pltpu.VMEM((2, 2, COMBINE_TM * TOKEN_TILE_ROWS, LANES), jnp.float32),
                            pltpu.SemaphoreType.DMA((2,))]),
        out_shape=jax.ShapeDtypeStruct((n, d), jnp.float32),
        compiler_params=pltpu.CompilerParams(dimension_semantics=("arbitrary",),
                                             vmem_limit_bytes=VMEM_LIMIT),
        name="moe_combine_ln",
    )(pos0, pos1, x1, gcol, ln_g, ln_b, ys)


def _moe(x1, x1t, meta, gcol, cnt, w_gate, w_up, w_down, layer, ln_g, ln_b, alpha):
    n = x1.shape[0]
    max_tiles = (2 * n + N_EXPERTS * (MOE_TM - 1)) // MOE_TM
    counts = cnt[:, 0].astype(jnp.int32)
    pos = _positions(meta, cnt)
    pos0, pos1 = pos[0], pos[1]
    tok, te = _plan(counts, pos0, pos1, max_tiles)
    ys = _experts(tok, te, x1t, w_gate, w_up, w_down, layer, max_tiles)
    return _combine(pos0, pos1, x1, ys, gcol, ln_g, ln_b, alpha)


def _t5_bucket(dist):
    max_exact = N_BUCKETS // 2
    d = jnp.maximum(dist, 0)
    large = max_exact + (jnp.log(jnp.maximum(d, 1).astype(jnp.float32) / max_exact)
                         / math.log(MAX_DISTANCE / max_exact) * (N_BUCKETS - max_exact)).astype(jnp.int32)
    large = jnp.minimum(large, N_BUCKETS - 1)
    return jnp.where(d < max_exact, d, large)


def _tile_bias_kernel(w_ref, o_ref):
    w = w_ref[0]
    for q in range(BLOCK):
        o_ref[q:q + 1, :] = w[:, BLOCK - 1 - q:3 * BLOCK - 1 - q]


def _tile_bias(rel_bias):
    dist = jnp.arange(BLOCK + 1)
    rows = jnp.stack([rel_bias[_t5_bucket(dist * dilation)] for _, dilation in DIL_PATTERNS])
    n_br, _, heads = rows.shape
    table = rows.astype(jnp.float32).transpose(0, 2, 1).reshape(n_br * heads, BLOCK + 1)
    w = jnp.pad(table[:, ::-1], ((0, 0), (BLOCK - 1, BLOCK)))[:, None, :]
    return pl.pallas_call(
        _tile_bias_kernel,
        grid=(n_br * heads,),
        in_specs=[pl.BlockSpec((1, 1, 3 * BLOCK), lambda s: (s, 0, 0))],
        out_specs=pl.BlockSpec((None, BLOCK, 2 * BLOCK), lambda s: (s, 0, 0)),
        out_shape=jax.ShapeDtypeStruct((n_br * heads, BLOCK, 2 * BLOCK), jnp.float32),
        compiler_params=pltpu.CompilerParams(dimension_semantics=("arbitrary",)),
        name="tile_bias",
    )(w)


def kernel(x, w_in, g_sb, g_dil, w_out, ln1_g, ln1_b, ln2_g, ln2_b, rel_bias,
           w_router, b_router, w_gate, w_up, w_down):
    batch, seq, d = x.shape
    depth = w_in.shape[0]
    alpha = (2.0 * depth) ** 0.25
    n = batch * seq
    x2d = x.reshape(n, d)

    tile_bias = _tile_bias(rel_bias)
    wrt =w_router.T.astype(jnp.float32)
    wrt_hi = _bf16(wrt)
    wrt_lo = _bf16(wrt - wrt_hi.astype(jnp.float32))
    br = b_router.astype(jnp.float32).reshape(N_EXPERTS, 1)

    for l in range(depth):
        w_in_l = _bf16(w_in[l])
        proj_sb, proj_dl = _in_proj(x2d, w_in_l)
        y_sb = _sb_attention(proj_sb, g_sb[l].reshape(N_PAIRS_SB, 1, LANES), batch, seq)
        y_dl = _dil_attention(proj_dl, g_dil[l].reshape(N_PAIRS_DIL, 1, LANES),
                              tile_bias, batch, seq)
        x1, x1t, meta, gcol, cnt = _out_proj(x2d, y_sb, y_dl, _bf16(w_out[l]),
                                             ln1_g[l].reshape(1, d), ln1_b[l].reshape(1, d),
                                             wrt_hi, wrt_lo, br, alpha)
        x2d = _moe(x1, x1t, meta, gcol, cnt, w_gate, w_up, w_down, l,
                   ln2_g[l].reshape(1, d), ln2_b[l].reshape(1, d), alpha)
    return x2d.reshape(batch, seq, d)
```

```python
import functools
import math

import jax
import jax.numpy as jnp
from jax import lax
from jax.experimental import pallas as pl
from jax.experimental.pallas import tpu as pltpu

D_MODEL = 1024
HEAD_DIM = 64
N_HEADS_SB = 8
N_HEADS_DIL = 8
D_SB = N_HEADS_SB * HEAD_DIM
D_DIL = N_HEADS_DIL * HEAD_DIM
DIL_PATTERNS = ((128, 1), (512, 4), (2048, 16))
BLOCK = 128
N_BUCKETS = 32
MAX_DISTANCE = 2048
N_EXPERTS = 16
N_GROUPS = 4
EXPERTS_PER_GROUP = N_EXPERTS // N_GROUPS
D_FF_EXPERT = 1024
LN_EPS = 1e-5
NEG_INF = -1e30

LANES = 128
N_PAIRS_SB = D_SB // LANES
N_PAIRS_DIL = D_DIL // LANES
VMEM_LIMIT = 56 * 1024 * 1024

ROW_TILE = 1024
SB_TQ = 256
SB_TK = 256
SB_TILES_PER_STEP = 4
SB_SKIP_BOUND = -110.0
DIL_SUPER = 2048
DIL_UNITS = DIL_SUPER // BLOCK
TOKEN_TILE_ROWS = D_MODEL // LANES
MOE_TM = 256
PERM_TP = 2048
COMBINE_TM = 512
DMA_UNROLL = 8

_NT = (((1,), (1,)), ((), ()))


def _bf16(x):
    return x.astype(jnp.bfloat16)


def _split_bf16(x):
    hi = _bf16(x)
    lo = _bf16(x - hi.astype(jnp.float32))
    return hi, lo


def _dot(a, b):
    return jnp.dot(a, b, preferred_element_type=jnp.float32)


def _dot_nt(a, b):
    return lax.dot_general(a, b, _NT, preferred_element_type=jnp.float32)


def _head_rms_gain(o, g):
    r = lax.broadcasted_iota(jnp.int32, (LANES, LANES), 0) // HEAD_DIM
    c = lax.broadcasted_iota(jnp.int32, (LANES, LANES), 1) // HEAD_DIM
    same_head = jnp.where(r == c, 1.0 / HEAD_DIM, 0.0).astype(jnp.bfloat16)
    hi, lo = _split_bf16(o * o)
    ms = _dot(hi, same_head) + _dot(lo, same_head)
    return o * lax.rsqrt(ms + 1e-6) * g


def _layer_norm(x, g, b):
    mu = jnp.mean(x, axis=-1, keepdims=True)
    xc = x - mu
    var = jnp.mean(xc * xc, axis=-1, keepdims=True)
    return xc * lax.rsqrt(var + LN_EPS) * g + b


def _in_proj_kernel(x_ref, w_ref, sb_ref, dl_ref):
    x = _bf16(x_ref[...])
    scale = HEAD_DIM ** -0.5
    col = 0
    for o_ref, n_q in ((sb_ref, N_PAIRS_SB), (dl_ref, N_PAIRS_DIL)):
        for j in range(0, o_ref.shape[0], 2):
            res = _dot(x, w_ref[:, col:col + 2 * LANES])
            col += 2 * LANES
            for jj in range(2):
                blk = res[:, jj * LANES:(jj + 1) * LANES]
                if j + jj < n_q:
                    blk = blk * scale
                o_ref[j + jj] = blk.astype(o_ref.dtype)


def _in_proj(x2d, w):
    n, d = x2d.shape
    slabs = lambda k: pl.BlockSpec((k, ROW_TILE, LANES), lambda i: (0, i, 0))
    return pl.pallas_call(
        _in_proj_kernel,
        grid=(n // ROW_TILE,),
        in_specs=[pl.BlockSpec((ROW_TILE, d), lambda i: (i, 0)),
                  pl.BlockSpec((d, w.shape[1]), lambda i: (0, 0))],
        out_specs=[slabs(3 * N_PAIRS_SB), slabs(3 * N_PAIRS_DIL)],
        out_shape=[jax.ShapeDtypeStruct((3 * N_PAIRS_SB, n, LANES), jnp.bfloat16),
                   jax.ShapeDtypeStruct((3 * N_PAIRS_DIL, n, LANES), jnp.float32)],
        compiler_params=pltpu.CompilerParams(dimension_semantics=("arbitrary",),
                                             vmem_limit_bytes=VMEM_LIMIT),
        name="in_proj",
    )(x2d, w)


def _sb_kernel(q_ref, k_ref, v_ref, g_ref, o_ref, carry_ref, acc_ref):
    step = pl.program_id(2)
    lane = lax.broadcasted_iota(jnp.int32, (1, LANES), 1)
    first_head = lane < HEAD_DIM

    jj = lax.broadcasted_iota(jnp.int32, (SB_TK, SB_TK + LANES), 0)
    ss = lax.broadcasted_iota(jnp.int32, (SB_TK, SB_TK + LANES), 1)
    suffix = jnp.where((jj > ss) | (ss >= SB_TK), 1.0, 0.0).astype(jnp.bfloat16)

    tq_i = lax.broadcasted_iota(jnp.int32, (2 * SB_TQ, SB_TK), 0) % SB_TQ
    ts_i = lax.broadcasted_iota(jnp.int32, (2 * SB_TQ, SB_TK), 1)
    causal = ts_i < tq_i

    def block(q2, kb, diag, carry, acc, weight=None):
        off = pl.multiple_of(kb * SB_TK, SB_TK)
        kblk = k_ref[pl.ds(off, SB_TK), :]
        vblk = v_ref[pl.ds(off, SB_TK), :]
        z = _dot_nt(q2, kblk)
        sp = jnp.log(1.0 + jnp.exp(-jnp.abs(z)))
        log_beta = jnp.minimum(z, 0.0) - sp
        log_1m = log_beta - z
        if diag:
            log_1m = jnp.where(causal, log_1m, 0.0)
        sfx = _dot(_bf16(log_1m), suffix)
        between = sfx[:, :SB_TK] + jnp.concatenate([carry] * (SB_TK // LANES), axis=1)
        a = jnp.exp(log_beta + between)
        if diag:
            a = jnp.where(causal, a, 0.0)
        out = _dot(_bf16(a), vblk)
        tot = sfx[:, SB_TK:]
        if weight is not None:
            out = out * weight
            tot = tot * weight
        return carry + tot, acc + out

    zeros = jnp.zeros((2 * SB_TQ, LANES), jnp.float32)
    tiles = []
    for t in range(SB_TILES_PER_STEP):
        i = step * SB_TILES_PER_STEP + t
        q = q_ref[t * SB_TQ:(t + 1) * SB_TQ, :]
        zero = jnp.zeros_like(q)
        q2 = jnp.concatenate([jnp.where(first_head, q, zero), jnp.where(first_head, zero, q)], axis=0)
        carry, acc = block(q2, i, True, zeros, zeros)
        has_prev = jnp.where(i >= 1, 1.0, 0.0)
        carry, acc = block(q2, jnp.maximum(i - 1, 0), False, carry, acc, weight=has_prev)
        carry_ref[t] = carry
        acc_ref[t] = acc
        tiles.append((i, q2, jnp.max(carry)))

    for t, (i, q2, mx0) in enumerate(tiles):
        def cond(state):
            kb, mx = state
            return jnp.logical_and(kb >= 0, mx > SB_SKIP_BOUND)

        def body(state, t=t, q2=q2):
            kb, _ = state
            c, a = block(q2, kb, False, carry_ref[t], acc_ref[t])
            carry_ref[t] = c
            acc_ref[t] = a
            return kb - 1, jnp.max(c)

        lax.while_loop(cond, body, (i - 2, mx0))

        o = jnp.where(first_head, acc_ref[t, :SB_TQ, :], acc_ref[t, SB_TQ:, :])
        o_ref[t * SB_TQ:(t + 1) * SB_TQ, :] = _head_rms_gain(o, g_ref[...]).astype(o_ref.dtype)


def _sb_attention(proj, g, batch, seq):
    n = batch * seq
    rows = SB_TQ * SB_TILES_PER_STEP
    nq = seq // rows
    return pl.pallas_call(
        _sb_kernel,
        grid=(batch, N_PAIRS_SB, nq),
        in_specs=[
            pl.BlockSpec((None, rows, LANES), lambda b, p, i: (p, b * nq + i, 0)),
            pl.BlockSpec((None, seq, LANES), lambda b, p, i: (N_PAIRS_SB + p, b, 0)),
            pl.BlockSpec((None, seq, LANES), lambda b, p, i: (2 * N_PAIRS_SB + p, b, 0)),
            pl.BlockSpec((None, 1, LANES), lambda b, p, i: (p, 0, 0)),
        ],
        out_specs=pl.BlockSpec((None, rows, LANES), lambda b, p, i: (p, b * nq + i, 0)),
        out_shape=jax.ShapeDtypeStruct((N_PAIRS_SB, n, LANES), jnp.bfloat16),
        scratch_shapes=[pltpu.VMEM((SB_TILES_PER_STEP, 2 * SB_TQ, LANES), jnp.float32),
                        pltpu.VMEM((SB_TILES_PER_STEP, 2 * SB_TQ, LANES), jnp.float32)],
        compiler_params=pltpu.CompilerParams(
            dimension_semantics=("arbitrary", "arbitrary", "arbitrary"),
            vmem_limit_bytes=VMEM_LIMIT),
        name="sb_attention",
    )(proj, proj, proj, g)


def _dil_kernel(tbias_ref, q_ref, k_ref, v_ref, g_ref, o_ref,
                bias_ref, obr_ref, lse_ref):
    b = pl.program_id(0)
    p = pl.program_id(1)
    i = pl.program_id(2)
    n_br = len(DIL_PATTERNS)

    qi = lax.broadcasted_iota(jnp.int32, (BLOCK, 2 * BLOCK), 0)
    kj = lax.broadcasted_iota(jnp.int32, (BLOCK, 2 * BLOCK), 1)
    steps = qi + BLOCK - kj

    assert all(window // dilation == BLOCK for window, dilation in DIL_PATTERNS)
    valid = (steps >= 0) & (steps <= BLOCK)

    @pl.when((b == 0) & (p == 0) & (i == 0))
    def _():
        n_tiles = n_br * N_HEADS_DIL

        def per_tile(s, _):
            tile = tbias_ref[s]
            rows = pl.ds(pl.multiple_of(s * BLOCK, BLOCK), BLOCK)
            bias_ref[rows, :] = jnp.where(valid, tile, NEG_INF)
            rows = pl.ds(pl.multiple_of((n_tiles + s) * BLOCK, BLOCK), BLOCK)
            bias_ref[rows, :] = jnp.where(valid & (kj >= BLOCK), tile, NEG_INF)
            return 0
        lax.fori_loop(0, n_br * N_HEADS_DIL, per_tile, 0)

    lane = lax.broadcasted_iota(jnp.int32, (1, LANES), 1)
    first_head = lane < HEAD_DIM
    t0 = i * DIL_SUPER

    def strided(ref, start, dilation):
        if dilation == 1:
            return ref[pl.ds(start, BLOCK), :]
        return ref[pl.ds(start, BLOCK, stride=dilation), :]

    for gidx, (window, dilation) in enumerate(DIL_PATTERNS):
        units_per_residue = DIL_UNITS // dilation
        shift = units_per_residue.bit_length() - 1

        def unit(u, _, gidx=gidx, dilation=dilation, units_per_residue=units_per_residue, shift=shift):
            r = lax.shift_right_logical(u, shift)
            cb = u & (units_per_residue - 1)
            sq = r + cb * (dilation * BLOCK)
            cur = t0 + sq
            prev = cur - dilation * BLOCK
            prev_ok = prev >= 0
            prev_c = jnp.where(prev_ok, prev, cur)
            no_prev = jnp.where(prev_ok, 0, 1)

            qb = _bf16(strided(q_ref, sq, dilation))
            kk = _bf16(jnp.concatenate([strided(k_ref, prev_c, dilation),
                                        strided(k_ref, cur, dilation)], axis=0))
            vv = _bf16(jnp.concatenate([strided(v_ref, prev_c, dilation),
                                        strided(v_ref, cur, dilation)], axis=0))
            zero = jnp.zeros_like(qb)
            q2 = jnp.concatenate([jnp.where(first_head, qb, zero),
                                  jnp.where(first_head, zero, qb)], axis=0)
            z = _dot_nt(q2, kk)
            slot = no_prev * (n_br * N_HEADS_DIL) + gidx * N_HEADS_DIL + 2 * p
            lg = z + bias_ref[pl.ds(pl.multiple_of(slot * BLOCK, 2 * BLOCK), 2 * BLOCK), :]
            m = jnp.max(lg, axis=-1, keepdims=True)
            pe = jnp.exp(lg - m)
            res = _dot(_bf16(pe), jnp.concatenate([vv, jnp.ones_like(vv)], axis=1))
            den = res[:, LANES:]
            out = res[:, :LANES] / den
            lse = m + jnp.log(den)
            o_tile = jnp.where(first_head, out[:BLOCK], out[BLOCK:])
            l_tile = jnp.where(first_head, lse[:BLOCK], lse[BLOCK:])
            if dilation == 1:
                idx = pl.ds(sq, BLOCK)
            else:
                idx = pl.ds(sq, BLOCK, stride=dilation)
            obr_ref.at[gidx][idx, :] = o_tile
            lse_ref.at[gidx][idx, :] = l_tile
            return 0

        lax.fori_loop(0, DIL_UNITS, unit, 0, unroll=8)

    chunk = 256

    def combine(c, _):
        rows = pl.ds(pl.multiple_of(c * chunk, chunk), chunk)
        ls = [lse_ref[gidx, rows, :] for gidx in range(n_br)]
        m = functools.reduce(jnp.maximum, ls)
        es = [jnp.exp(l - m) for l in ls]
        num = sum(e * obr_ref[gidx, rows, :] for gidx, e in enumerate(es))
        o = num / sum(es)
        o_ref[rows, :] = _head_rms_gain(o, g_ref[...]).astype(o_ref.dtype)
        return 0

    lax.fori_loop(0, DIL_SUPER // chunk, combine, 0)


def _dil_attention(proj, g, tile_bias, batch, seq):
    n = batch * seq
    ns = seq // DIL_SUPER
    n_br = len(DIL_PATTERNS)
    return pl.pallas_call(
        _dil_kernel,
        grid=(batch, N_PAIRS_DIL, ns),
        in_specs=[
            pl.BlockSpec((n_br * N_HEADS_DIL, BLOCK, 2 * BLOCK), lambda b, p, i: (0, 0, 0)),
            pl.BlockSpec((None, DIL_SUPER, LANES), lambda b, p, i: (p, b * ns + i, 0)),
            pl.BlockSpec((None, seq, LANES), lambda b, p, i: (N_PAIRS_DIL + p, b, 0)),
            pl.BlockSpec((None, seq, LANES), lambda b, p, i: (2 * N_PAIRS_DIL + p, b, 0)),
            pl.BlockSpec((None, 1, LANES), lambda b, p, i: (p, 0, 0)),
        ],
        out_specs=pl.BlockSpec((None, DIL_SUPER, LANES), lambda b, p, i: (p, b * ns + i, 0)),
        out_shape=jax.ShapeDtypeStruct((N_PAIRS_DIL, n, LANES), jnp.bfloat16),
        scratch_shapes=[pltpu.VMEM((2 * n_br * N_HEADS_DIL * BLOCK, 2 * BLOCK), jnp.float32),
                        pltpu.VMEM((n_br, DIL_SUPER, LANES), jnp.float32),
                        pltpu.VMEM((n_br, DIL_SUPER, LANES), jnp.float32)],
        compiler_params=pltpu.CompilerParams(
            dimension_semantics=("arbitrary", "arbitrary", "arbitrary"),
            vmem_limit_bytes=VMEM_LIMIT),
        name="dilated_attention",
    )(tile_bias, proj, proj, proj, g)


def _router(x1, wrt_hi, wrt_lo, br):
    tm = x1.shape[0]
    xh, xl = _split_bf16(x1)
    lt = _dot_nt(wrt_hi, xh) + _dot_nt(wrt_hi, xl) + _dot_nt(wrt_lo, xh) + br
    rows = [lt[e:e + 1, :] for e in range(N_EXPERTS)]
    m = functools.reduce(jnp.maximum, rows)
    ex = [jnp.exp(r - m) for r in rows]
    tot = functools.reduce(lambda a, c: a + c, ex)
    probs = [e / tot for e in ex]

    keep, score = [], []
    for gi in range(N_GROUPS):
        members = range(gi * EXPERTS_PER_GROUP, (gi + 1) * EXPERTS_PER_GROUP)
        s = None
        for e in members:
            rank = None
            for o in members:
                if o == e:
                    continue
                beats = (probs[o] > probs[e]) | ((probs[o] == probs[e]) & (o < e))
                beats = beats.astype(jnp.int32)
                rank = beats if rank is None else rank + beats
            k = rank < 2
            keep.append(k)
            contrib = jnp.where(k, probs[e], 0.0)
            s = contrib if s is None else s + contrib
        score.append(s)

    sel, gates = [], []
    for gi in range(N_GROUPS):
        chosen = None
        for o in range(N_GROUPS):
            if o == gi:
                continue
            c = (score[gi] > score[o]) if o < gi else (score[gi] >= score[o])
            chosen = c if chosen is None else (chosen & c)
        for e in range(gi * EXPERTS_PER_GROUP, (gi + 1) * EXPERTS_PER_GROUP):
            s = keep[e] & chosen
            sel.append(s)
            gates.append(jnp.where(s, probs[e] / score[gi], 0.0))

    idx0 = functools.reduce(jnp.minimum, [jnp.where(sel[e], e, N_EXPERTS) for e in range(N_EXPERTS)])
    idx1 = functools.reduce(jnp.maximum, [jnp.where(sel[e], e, -1) for e in range(N_EXPERTS)])
    g0 = sum(jnp.where(idx0 == e, gates[e], 0.0) for e in range(N_EXPERTS))
    g1 = sum(jnp.where(idx1 == e, gates[e], 0.0) for e in range(N_EXPERTS))

    row_id = lax.broadcasted_iota(jnp.int32, (N_EXPERTS, tm), 0)
    selmat = jnp.zeros((N_EXPERTS, tm), jnp.float32)
    for e in range(N_EXPERTS):
        selmat = jnp.where((row_id == e) & sel[e], 1.0, selmat)
    return idx0, idx1, g0, g1, selmat


def _to_token_tiles(ref, x):
    rows = x.shape[0]
    for s in range(TOKEN_TILE_ROWS):
        ref[pl.ds(s, rows, stride=TOKEN_TILE_ROWS), :] = x[:, s * LANES:(s + 1) * LANES]


def _from_token_tiles(ref, rows):
    return jnp.concatenate([ref[pl.ds(s, rows, stride=TOKEN_TILE_ROWS), :]
                            for s in range(TOKEN_TILE_ROWS)], axis=1)


def _out_proj_kernel(x_ref, ysb_ref, ydl_ref, w_ref, lng_ref, lnb_ref,
                     wrh_ref, wrl_ref, br_ref,
                     x1_ref, x1t_ref, meta_ref, gcol_ref, cnt_ref, carry_ref, *, alpha):
    i = pl.program_id(0)
    tm = x_ref.shape[0]

    @pl.when(i == 0)
    def _():
        carry_ref[...] = jnp.zeros_like(carry_ref)

    y = jnp.concatenate([ysb_ref[j] for j in range(N_PAIRS_SB)]
                        + [ydl_ref[j] for j in range(N_PAIRS_DIL)], axis=1)
    h = _dot(y, w_ref[...])
    x1 = _layer_norm(alpha * x_ref[...] + h, lng_ref[...], lnb_ref[...])
    x1_ref[...] = x1
    _to_token_tiles(x1t_ref, x1)

    idx0, idx1, g0, g1, selmat = _router(x1, wrh_ref[...], wrl_ref[...], br_ref[...])

    jj = lax.broadcasted_iota(jnp.int32, (tm, tm + LANES), 0)
    ss = lax.broadcasted_iota(jnp.int32, (tm, tm + LANES), 1)
    before = jnp.where((jj < ss) | (ss >= tm), 1.0, 0.0).astype(jnp.bfloat16)
    pc = _dot(_bf16(selmat), before)
    carry = carry_ref[...]
    rankmat = pc[:, :tm] + jnp.concatenate([carry] * (tm // LANES), axis=1)
    carry = carry + pc[:, tm:]
    carry_ref[...] = carry
    cnt_ref[...] = carry

    rank0 = sum(jnp.where(idx0 == e, rankmat[e:e + 1, :], 0.0) for e in range(N_EXPERTS))
    rank1 = sum(jnp.where(idx1 == e, rankmat[e:e + 1, :], 0.0) for e in range(N_EXPERTS))

    row8 = lax.broadcasted_iota(jnp.int32, (8, tm), 0)
    meta = jnp.zeros((8, tm), jnp.int32)
    for r, v in enumerate((idx0, idx1, rank0.astype(jnp.int32), rank1.astype(jnp.int32))):
        meta = jnp.where(row8 == r, v, meta)
    meta_ref[...] = meta

    rowl = lax.broadcasted_iota(jnp.int32, (LANES, tm), 0)
    gt = jnp.where(rowl == 0, g0, jnp.where(rowl == 1, g1, 0.0))
    gcol_ref[...] = gt.T


def _out_proj(x2d, y_sb, y_dl, w_out, ln_g, ln_b, wrt_hi, wrt_lo, br, alpha):
    n, d = x2d.shape
    const = lambda i: (0, 0)
    return pl.pallas_call(
        functools.partial(_out_proj_kernel, alpha=alpha),
        grid=(n // ROW_TILE,),
        in_specs=[
            pl.BlockSpec((ROW_TILE, d), lambda i: (i, 0)),
            pl.BlockSpec((N_PAIRS_SB, ROW_TILE, LANES), lambda i: (0, i, 0)),
            pl.BlockSpec((N_PAIRS_DIL, ROW_TILE, LANES), lambda i: (0, i, 0)),
            pl.BlockSpec((d, d), const),
            pl.BlockSpec((1, d), const),
            pl.BlockSpec((1, d), const),
            pl.BlockSpec((N_EXPERTS, d), const),
            pl.BlockSpec((N_EXPERTS, d), const),
            pl.BlockSpec((N_EXPERTS, 1), const),
        ],
        out_specs=[pl.BlockSpec((ROW_TILE, d), lambda i: (i, 0)),
                   pl.BlockSpec((ROW_TILE * TOKEN_TILE_ROWS, LANES), lambda i: (i, 0)),
                   pl.BlockSpec((8, ROW_TILE), lambda i: (0, i)),
                   pl.BlockSpec((ROW_TILE, LANES), lambda i: (i, 0)),
                   pl.BlockSpec((N_EXPERTS, LANES), const)],
        out_shape=[jax.ShapeDtypeStruct((n, d), jnp.float32),
                   jax.ShapeDtypeStruct((n * TOKEN_TILE_ROWS, LANES), jnp.float32),
                   jax.ShapeDtypeStruct((8, n), jnp.int32),
                   jax.ShapeDtypeStruct((n, LANES), jnp.float32),
                   jax.ShapeDtypeStruct((N_EXPERTS, LANES), jnp.float32)],
        scratch_shapes=[pltpu.VMEM((N_EXPERTS, LANES), jnp.float32)],
        compiler_params=pltpu.CompilerParams(dimension_semantics=("arbitrary",),
                                             vmem_limit_bytes=VMEM_LIMIT),
        name="out_proj_ln_router",
    )(x2d, y_sb, y_dl, w_out, ln_g, ln_b, wrt_hi, wrt_lo, br)


def _segment_offsets(cnt_ref, off_ref):
    acc = jnp.int32(0)
    padded = []
    for e in range(N_EXPERTS):
        off_ref[e] = acc
        pc = lax.div(cnt_ref[e] + (MOE_TM - 1), MOE_TM) * MOE_TM
        padded.append(pc)
        acc = acc + pc
    return padded, acc


def _row_copy(src_ref, src_row, dst_ref, dst_row, sem):
    def tile(row):
        if isinstance(row, int):
            return pl.ds(row * TOKEN_TILE_ROWS, TOKEN_TILE_ROWS)
        return pl.ds(pl.multiple_of(row * TOKEN_TILE_ROWS, TOKEN_TILE_ROWS), TOKEN_TILE_ROWS)
    return pltpu.make_async_copy(src_ref.at[tile(src_row), :], dst_ref.at[tile(dst_row), :], sem)


def _pos_kernel(meta_ref, cnt_ref, pos_ref):
    cnt = cnt_ref[...]
    padded = jnp.floor((cnt + (MOE_TM - 1)) * (1.0 / MOE_TM)) * MOE_TM
    meta = meta_ref[...]
    rows = []
    for k in range(2):
        e = meta[k:k + 1, :]
        start = jnp.zeros(e.shape, jnp.float32)
        off = jnp.zeros((1, 1), jnp.float32)
        for ex in range(N_EXPERTS):
            start = jnp.where(e == ex, off, start)
            off = off + padded[ex:ex + 1, 0:1]
        rows.append(start.astype(jnp.int32) + meta[2 + k:3 + k, :])
    row8 = lax.broadcasted_iota(jnp.int32, meta.shape, 0)
    pos_ref[...] = jnp.where(row8 == 0, rows[0], jnp.where(row8 == 1, rows[1], 0))


def _positions(meta, cnt):
    n = meta.shape[1]
    tile = 2048
    return pl.pallas_call(
        _pos_kernel,
        grid=(n // tile,),
        in_specs=[pl.BlockSpec((8, tile), lambda i: (0, i)),
                  pl.BlockSpec((N_EXPERTS, LANES), lambda i: (0, 0))],
        out_specs=pl.BlockSpec((8, tile), lambda i: (0, i)),
        out_shape=jax.ShapeDtypeStruct((8, n), jnp.int32),
        compiler_params=pltpu.CompilerParams(dimension_semantics=("arbitrary",)),
        name="moe_pos",
    )(meta, cnt)


def _plan_kernel(cnt_ref, pos0_ref, pos1_ref, tok_ref, te_ref, off_ref, *, max_tiles):
    i = pl.program_id(0)
    tp = pos0_ref.shape[0]

    @pl.when(i == 0)
    def _():
        padded, total = _segment_offsets(cnt_ref, off_ref)

        def clear(k, c):
            tok_ref[k] = 0
            return c
        for e in range(N_EXPERTS):
            lax.fori_loop(off_ref[e] + cnt_ref[e], off_ref[e] + padded[e], clear, 0)
        lax.fori_loop(total, max_tiles * MOE_TM, clear, 0)

        n_tiles = lax.div(total, MOE_TM)
        for e in range(N_EXPERTS):
            first = lax.div(off_ref[e], MOE_TM)
            count = lax.div(padded[e], MOE_TM)

            def fill(k, c, first=first, e=e):
                te_ref[first + k] = e
                return c
            lax.fori_loop(0, count, fill, 0)
        last = te_ref[n_tiles - 1]

        def fill_tail(k, c):
            te_ref[k] = last
            return c
        lax.fori_loop(n_tiles, max_tiles, fill_tail, 0)
        te_ref[max_tiles] = n_tiles

    def place(j, c):
        token = i * tp + j
        tok_ref[pos0_ref[j]] = token
        tok_ref[pos1_ref[j]] = token
        return c
    lax.fori_loop(0, tp, place, 0, unroll=8)


def _plan(counts, pos0, pos1, max_tiles):
    n = pos0.shape[0]
    smem = pl.BlockSpec(memory_space=pltpu.SMEM)
    vec = pl.BlockSpec((PERM_TP,), lambda i: (i,), memory_space=pltpu.SMEM)
    return pl.pallas_call(
        functools.partial(_plan_kernel, max_tiles=max_tiles),
        grid=(n // PERM_TP,),
        in_specs=[smem, vec, vec],
        out_specs=[smem, smem],
        out_shape=[jax.ShapeDtypeStruct((max_tiles * MOE_TM,), jnp.int32),
                   jax.ShapeDtypeStruct((max_tiles + 1,), jnp.int32)],
        scratch_shapes=[pltpu.SMEM((N_EXPERTS,), jnp.int32)],
        compiler_params=pltpu.CompilerParams(dimension_semantics=("arbitrary",)),
        name="moe_plan",
    )(counts, pos0, pos1)


def _experts_kernel(tok_ref, te_ref, layer_ref, x_ref, wg_ref, wu_ref, wd_ref, ys_ref,
                    xbuf_ref, wgb_ref, wub_ref, wdb_ref, gsem, *, max_tiles):
    i = pl.program_id(0)
    n_tiles = te_ref[max_tiles]
    slot = i & 1
    tile_rows = MOE_TM * TOKEN_TILE_ROWS

    def gather(tile, s):
        base = tile * MOE_TM

        def body(j8, c):
            for u in range(DMA_UNROLL):
                j = j8 * DMA_UNROLL + u
                _row_copy(x_ref, tok_ref[base + j], xbuf_ref.at[s], j, gsem.at[s]).start(priority=1)
            return c
        lax.fori_loop(0, MOE_TM // DMA_UNROLL, body, 0)

    @pl.when(i == 0)
    def _():
        gather(0, 0)

    @pl.when(i + 1 < n_tiles)
    def _():
        gather(i + 1, 1 - slot)

    @pl.when(i < n_tiles)
    def _():
        pltpu.make_async_copy(x_ref.at[pl.ds(0, tile_rows), :], xbuf_ref.at[slot],
                              gsem.at[slot]).wait()

        @pl.when((i == 0) | (te_ref[i] != te_ref[jnp.maximum(i - 1, 0)]))
        def _():
            wgb_ref[...] = _bf16(wg_ref[...])
            wub_ref[...] = _bf16(wu_ref[...])
            wdb_ref[...] = _bf16(wd_ref[...])

        xb = _bf16(_from_token_tiles(xbuf_ref.at[slot], MOE_TM))
        gt = _dot(xb, wgb_ref[...])
        up = _dot(xb, wub_ref[...])
        h = gt / (1.0 + jnp.exp(-gt)) * up
        _to_token_tiles(ys_ref, _dot(_bf16(h), wdb_ref[...]))

    @pl.when(i >= n_tiles)
    def _():
        ys_ref[...] = jnp.zeros_like(ys_ref)


def _experts(tok, te, x1t, w_gate, w_up, w_down, layer, max_tiles):
    d, f = w_gate.shape[-2:]
    rows = MOE_TM * TOKEN_TILE_ROWS
    expert = lambda i, tok, te, lay: (lay[0], te[i], 0, 0)
    return pl.pallas_call(
        functools.partial(_experts_kernel, max_tiles=max_tiles),
        grid_spec=pltpu.PrefetchScalarGridSpec(
            num_scalar_prefetch=3,
            grid=(max_tiles,),
            in_specs=[pl.BlockSpec(memory_space=pl.ANY),
                      pl.BlockSpec((None, None, d, f), expert),
                      pl.BlockSpec((None, None, d, f), expert),
                      pl.BlockSpec((None, None, f, d), expert)],
            out_specs=pl.BlockSpec((rows, LANES), lambda i, tok, te, lay: (i, 0)),
            scratch_shapes=[pltpu.VMEM((2, rows, LANES), jnp.float32),
                            pltpu.VMEM((d, f), jnp.bfloat16),
                            pltpu.VMEM((d, f), jnp.bfloat16),
                            pltpu.VMEM((f, d), jnp.bfloat16),
                            pltpu.SemaphoreType.DMA((2,))]),
        out_shape=jax.ShapeDtypeStruct((max_tiles * rows, LANES), jnp.float32),
        compiler_params=pltpu.CompilerParams(dimension_semantics=("arbitrary",),
                                             vmem_limit_bytes=VMEM_LIMIT),
        name="moe_experts",
    )(tok, te, jnp.full((1,), layer, jnp.int32), x1t, w_gate, w_up, w_down)


def _combine_kernel(pos0_ref, pos1_ref, x_ref, gcol_ref, lng_ref, lnb_ref, ys_ref, o_ref,
                    ybuf_ref, sem, *, alpha):
    i = pl.program_id(0)
    n_steps = pl.num_programs(0)
    tm = x_ref.shape[0]
    slot = i & 1
    tile_rows = tm * TOKEN_TILE_ROWS

    def gather(step, s):
        base = step * tm

        def body(j, c):
            _row_copy(ys_ref, pos0_ref[base + j], ybuf_ref.at[s, 0], j, sem.at[s]).start(priority=0)
            _row_copy(ys_ref, pos1_ref[base + j], ybuf_ref.at[s, 1], j, sem.at[s]).start(priority=1)
            return c
        lax.fori_loop(0, tm, body, 0, unroll=DMA_UNROLL)

    @pl.when(i == 0)
    def _():
        gather(0, 0)

    @pl.when(i + 1 < n_steps)
    def _():
        gather(i + 1, 1 - slot)

    for k in range(2):
        pltpu.make_async_copy(ys_ref.at[pl.ds(0, tile_rows), :], ybuf_ref.at[slot, k],
                              sem.at[slot]).wait()

    lane = lax.broadcasted_iota(jnp.int32, (1, LANES), 1)
    gc = gcol_ref[...]
    g0 = jnp.sum(jnp.where(lane == 0, gc, 0.0), axis=-1, keepdims=True)
    g1 = jnp.sum(jnp.where(lane == 1, gc, 0.0), axis=-1, keepdims=True)
    y = (g0 * _from_token_tiles(ybuf_ref.at[slot, 0], tm)
         + g1 * _from_token_tiles(ybuf_ref.at[slot, 1], tm))
    o_ref[...] = _layer_norm(alpha * x_ref[...] + y, lng_ref[...], lnb_ref[...])


def _combine(pos0, pos1, x1, ys, gcol, ln_g, ln_b, alpha):
    n, d = x1.shape
    const = lambda i, p0, p1: (0, 0)
    tile = lambda i, p0, p1: (i, 0)
    return pl.pallas_call(
        functools.partial(_combine_kernel, alpha=alpha),
        grid_spec=pltpu.PrefetchScalarGridSpec(
            num_scalar_prefetch=2,
            grid=(n // COMBINE_TM,),
            in_specs=[pl.BlockSpec((COMBINE_TM, d), tile),
                      pl.BlockSpec((COMBINE_TM, LANES), tile),
                      pl.BlockSpec((1, d), const), pl.BlockSpec((1, d), const),
                      pl.BlockSpec(memory_space=pl.ANY)],
            out_specs=pl.BlockSpec((COMBINE_TM, d), tile),
            scratch_shapes=[pltpu.VMEM((2, 2, COMBINE_TM * TOKEN_TILE_ROWS, LANES), jnp.float32),
                            pltpu.SemaphoreType.DMA((2,))]),
        out_shape=jax.ShapeDtypeStruct((n, d), jnp.float32),
        compiler_params=pltpu.CompilerParams(dimension_semantics=("arbitrary",),
                                             vmem_limit_bytes=VMEM_LIMIT),
        name="moe_combine_ln",
    )(pos0, pos1, x1, gcol, ln_g, ln_b, ys)


def _moe(x1, x1t, meta, gcol, cnt, w_gate, w_up, w_down, layer, ln_g, ln_b, alpha):
    n = x1.shape[0]
    max_tiles = (2 * n + N_EXPERTS * (MOE_TM - 1)) // MOE_TM
    counts = cnt[:, 0].astype(jnp.int32)
    pos = _positions(meta, cnt)
    pos0, pos1 = pos[0], pos[1]
    tok, te = _plan(counts, pos0, pos1, max_tiles)
    ys = _experts(tok, te, x1t, w_gate, w_up, w_down, layer, max_tiles)
    return _combine(pos0, pos1, x1, ys, gcol, ln_g, ln_b, alpha)


def _t5_bucket(dist):
    max_exact = N_BUCKETS // 2
    d = jnp.maximum(dist, 0)
    large = max_exact + (jnp.log(jnp.maximum(d, 1).astype(jnp.float32) / max_exact)
                         / math.log(MAX_DISTANCE / max_exact) * (N_BUCKETS - max_exact)).astype(jnp.int32)
    large = jnp.minimum(large, N_BUCKETS - 1)
    return jnp.where(d < max_exact, d, large)


def _tile_bias_kernel(w_ref, o_ref):
    w = w_ref[0]
    for q in range(BLOCK):
        o_ref[q:q + 1, :] = w[:, BLOCK - 1 - q:3 * BLOCK - 1 - q]


def _tile_bias(rel_bias):
    dist = jnp.arange(BLOCK + 1)
    rows = jnp.stack([rel_bias[_t5_bucket(dist * dilation)] for _, dilation in DIL_PATTERNS])
    n_br, _, heads = rows.shape
    table = rows.astype(jnp.float32).transpose(0, 2, 1).reshape(n_br * heads, BLOCK + 1)
    w = jnp.pad(table[:, ::-1], ((0, 0), (BLOCK - 1, BLOCK)))[:, None, :]
    return pl.pallas_call(
        _tile_bias_kernel,
        grid=(n_br * heads,),
        in_specs=[pl.BlockSpec((1, 1, 3 * BLOCK), lambda s: (s, 0, 0))],
        out_specs=pl.BlockSpec((None, BLOCK, 2 * BLOCK), lambda s: (s, 0, 0)),
        out_shape=jax.ShapeDtypeStruct((n_br * heads, BLOCK, 2 * BLOCK), jnp.float32),
        compiler_params=pltpu.CompilerParams(dimension_semantics=("arbitrary",)),
        name="tile_bias",
    )(w)


def kernel(x, w_in, g_sb, g_dil, w_out, ln1_g, ln1_b, ln2_g, ln2_b, rel_bias,
           w_router, b_router, w_gate, w_up, w_down):
    batch, seq, d = x.shape
    depth = w_in.shape[0]
    alpha = (2.0 * depth) ** 0.25
    n = batch * seq
    x2d = x.reshape(n, d)

    tile_bias = _tile_bias(rel_bias)
    wrt =w_router.T.astype(jnp.float32)
    wrt_hi = _bf16(wrt)
    wrt_lo = _bf16(wrt - wrt_hi.astype(jnp.float32))
    br = b_router.astype(jnp.float32).reshape(N_EXPERTS, 1)

    for l in range(depth):
        w_in_l = _bf16(w_in[l])
        proj_sb, proj_dl = _in_proj(x2d, w_in_l)
        y_sb = _sb_attention(proj_sb, g_sb[l].reshape(N_PAIRS_SB, 1, LANES), batch, seq)
        y_dl = _dil_attention(proj_dl, g_dil[l].reshape(N_PAIRS_DIL, 1, LANES),
                              tile_bias, batch, seq)
        x1, x1t, meta, gcol, cnt = _out_proj(x2d, y_sb, y_dl, _bf16(w_out[l]),
                                             ln1_g[l].reshape(1, d), ln1_b[l].reshape(1, d),
                                             wrt_hi, wrt_lo, br, alpha)
        x2d = _moe(x1, x1t, meta, gcol, cnt, w_gate, w_up, w_down, l,
                   ln2_g[l].reshape(1, d), ln2_b[l].reshape(1, d), alpha)
    return x2d.reshape(batch, seq, d)
```
